```python
import math
import jax, jax.numpy as jnp
from jax import lax
import numpy as np

D_MODEL = 1024
BATCH = 16
SEQ = 2048
DEPTH = 1

ATTN_HEADS = 8
ATTN_HEAD_DIM = 64
ATTN_V_DIM = 2 * ATTN_HEAD_DIM
Q_BLOCK = 128
NUM_BUCKETS = 32
MAX_DISTANCE = 128
SSM_EXPAND = 2
SSM_D_INNER = SSM_EXPAND * D_MODEL
SSM_HEAD_DIM = 64
SSM_HEADS = SSM_D_INNER // SSM_HEAD_DIM
SSM_GROUPS = 4
SSM_HEADS_PER_GROUP = SSM_HEADS // SSM_GROUPS
SSM_STATE = 128
SSM_CONV = 4
SSM_CHUNK = 128
D_FF = 2816
FFN_CONV = 3
RMS_EPS = 1e-6
SUBLN_EPS = 1e-5

Q_COLS = ATTN_HEADS * 2 * ATTN_HEAD_DIM
K_COLS = ATTN_HEADS * 2 * ATTN_HEAD_DIM
V_COLS = ATTN_HEADS * ATTN_V_DIM
Z_COLS = SSM_D_INNER
BC_COLS = SSM_GROUPS * SSM_STATE
XBC_COLS = SSM_D_INNER + 2 * BC_COLS
DT_COLS = SSM_HEADS
GATE_COLS = 2 * D_MODEL
IN_COLS = Q_COLS + K_COLS + V_COLS + Z_COLS + XBC_COLS + DT_COLS + GATE_COLS

kernel_name = "hybrid_diffattn_mamba2_gated_convffn"


def rms_norm(x, g, eps=RMS_EPS):
    xf = x.astype(jnp.float32)
    y = xf * lax.rsqrt(jnp.mean(xf * xf, axis=-1, keepdims=True) + eps)
    return (y * g.astype(jnp.float32)).astype(x.dtype)


def causal_dwconv(x, w, b):
    k = w.shape[0]
    y = lax.conv_general_dilated(
        x, w[:, None, :].astype(x.dtype), window_strides=(1,), padding=[(k - 1, 0)],
        dimension_numbers=('NWC', 'WIO', 'NWC'), feature_group_count=x.shape[-1])
    return y + b.astype(x.dtype)


def t5_causal_bucket(qpos, kpos):
    n = jnp.maximum(qpos[:, None] - kpos[None, :], 0)
    max_exact = NUM_BUCKETS // 2
    nf = jnp.maximum(n, 1).astype(jnp.float32)
    large = max_exact + (jnp.log(nf / max_exact) / math.log(MAX_DISTANCE / max_exact)
                         * (NUM_BUCKETS - max_exact)).astype(jnp.int32)
    large = jnp.minimum(large, NUM_BUCKETS - 1)
    return jnp.where(n < max_exact, n, large)


def diff_attention(q, k, v, lam, rel_bias):
    b, h, _, s, _ = q.shape
    n_blocks = s // Q_BLOCK
    kpos = jnp.arange(s, dtype=jnp.int32)

    def one_block(i):
        start = i * Q_BLOCK
        qb = lax.dynamic_slice_in_dim(q, start, Q_BLOCK, axis=3)
        qpos = start + jnp.arange(Q_BLOCK, dtype=jnp.int32)
        bias = jnp.take(rel_bias, t5_causal_bucket(qpos, kpos), axis=0)
        bias = jnp.transpose(bias.astype(jnp.float32), (2, 0, 1))
        logits = jnp.einsum('bhmqd,bhmkd->bhmqk', qb, k).astype(jnp.float32)
        logits = logits + bias[None, :, None]
        logits = jnp.where((qpos[:, None] >= kpos[None, :])[None, None, None], logits, -jnp.inf)
        p = jax.nn.softmax(logits, axis=-1)
        a = p[:, :, 0] - lam * p[:, :, 1]
        return jnp.einsum('bhqk,bhkv->bhqv', a.astype(v.dtype), v)

    out = lax.map(one_block, jnp.arange(n_blocks, dtype=jnp.int32))
    return jnp.transpose(out, (1, 2, 0, 3, 4)).reshape(b, h, s, v.shape[-1])


def ssd_chunked_scan(x, dt, a, bmat, cmat):
    b, s = x.shape[:2]
    nc = s // SSM_CHUNK
    g, k, p, n = SSM_GROUPS, SSM_HEADS_PER_GROUP, SSM_HEAD_DIM, SSM_STATE

    def to_chunks(t):
        return jnp.swapaxes(t.reshape((b, nc, SSM_CHUNK) + t.shape[2:]), 0, 1)

    xs = to_chunks((x * dt[..., None]).reshape(b, s, g, k, p))
    adts = to_chunks((dt * a).reshape(b, s, g, k))
    bs, cs = to_chunks(bmat), to_chunks(cmat)
    causal = jnp.tril(jnp.ones((SSM_CHUNK, SSM_CHUNK), dtype=bool))[None, :, :, None, None]

    def step(state, inp):
        xc, ac, bc, cc = inp
        cum = jnp.cumsum(ac, axis=1)
        seg = cum[:, :, None] - cum[:, None, :]
        decay = jnp.exp(jnp.where(causal, seg, -jnp.inf))
        cb = jnp.einsum('blgn,bsgn->blsg', cc, bc)
        y_diag = jnp.einsum('blsgk,bsgkp->blgkp', cb[..., None] * decay, xc)
        y_off = jnp.einsum('blgn,bgkpn->blgkp', cc, state) * jnp.exp(cum)[..., None]
        total = cum[:, -1]
        w = jnp.exp(total[:, None] - cum)
        new_state = state * jnp.exp(total)[..., None, None] + jnp.einsum(
            'bsgn,bsgkp->bgkpn', bc, xc * w[..., None])
        return new_state, y_diag + y_off

    init = jnp.zeros((b, g, k, p, n), dtype=jnp.float32)
    _, ys = lax.scan(step, init, (xs, adts, bs, cs))
    return jnp.swapaxes(ys, 0, 1).reshape(b, s, SSM_HEADS, p)


def setup_inputs(seed: int = 0) -> dict:
    key = jax.random.key(seed)
    ks = jax.random.split(key, 32)
    f32 = jnp.float32
    nrm = lambda k, shape, scale: jax.random.normal(k, shape, f32) * scale
    gain = lambda k, shape: 1.0 + 0.02 * jax.random.normal(k, shape, f32)
    L = DEPTH
    dt0 = jnp.exp(jax.random.uniform(ks[10], (L, SSM_HEADS), f32, math.log(1e-3), math.log(1e-1)))
    return {
        "x": nrm(ks[0], (BATCH, SEQ, D_MODEL), 1.0),
        "rel_bias": nrm(ks[1], (NUM_BUCKETS, ATTN_HEADS), 0.5),
        "norm_mix_g": gain(ks[2], (L, D_MODEL)),
        "w_in": nrm(ks[3], (L, D_MODEL, IN_COLS), D_MODEL ** -0.5),
        "q_norm_g": gain(ks[4], (L, ATTN_HEAD_DIM)),
        "k_norm_g": gain(ks[5], (L, ATTN_HEAD_DIM)),
        "lambda_q1": nrm(ks[6], (L, ATTN_HEAD_DIM), 0.1),
        "lambda_k1": nrm(ks[7], (L, ATTN_HEAD_DIM), 0.1),
        "lambda_q2": nrm(ks[8], (L, ATTN_HEAD_DIM), 0.1),
        "lambda_k2": nrm(ks[9], (L, ATTN_HEAD_DIM), 0.1),
        "attn_subln_g": gain(ks[11], (L, ATTN_V_DIM)),
        "conv_ssm_w": nrm(ks[12], (L, SSM_CONV, XBC_COLS), SSM_CONV ** -0.5),
        "conv_ssm_b": nrm(ks[13], (L, XBC_COLS), 0.02),
        "dt_bias": dt0 + jnp.log(-jnp.expm1(-dt0)),
        "a_log": jnp.log(jax.random.uniform(ks[14], (L, SSM_HEADS), f32, 1.0, 16.0)),
        "d_skip": gain(ks[15], (L, SSM_HEADS)),
        "ssm_norm_g": gain(ks[16], (L, SSM_D_INNER)),
        "w_proj_attn": nrm(ks[17], (L, V_COLS, D_MODEL), V_COLS ** -0.5),
        "w_proj_ssm": nrm(ks[18], (L, SSM_D_INNER, D_MODEL), SSM_D_INNER ** -0.5),
        "w_out": nrm(ks[19], (L, D_MODEL, D_MODEL), D_MODEL ** -0.5),
        "norm_ffn_g": gain(ks[20], (L, D_MODEL)),
        "w_up": nrm(ks[21], (L, D_MODEL, 2 * D_FF), D_MODEL ** -0.5),
        "conv_ffn_w": nrm(ks[22], (L, FFN_CONV, 2 * D_FF), FFN_CONV ** -0.5),
        "conv_ffn_b": nrm(ks[23], (L, 2 * D_FF), 0.02),
        "w_down": nrm(ks[24], (L, D_FF, D_MODEL), D_FF ** -0.5),
    }


def reference(x, rel_bias, norm_mix_g, w_in, q_norm_g, k_norm_g, lambda_q1, lambda_k1,
              lambda_q2, lambda_k2, attn_subln_g, conv_ssm_w, conv_ssm_b, dt_bias, a_log,
              d_skip, ssm_norm_g, w_proj_attn, w_proj_ssm, w_out, norm_ffn_g, w_up,
              conv_ffn_w, conv_ffn_b, w_down):
    b, s, _ = x.shape
    f32 = jnp.float32
    sizes = [Q_COLS, K_COLS, V_COLS, Z_COLS, XBC_COLS, DT_COLS]
    offsets = list(np.cumsum(sizes))
    for l in range(DEPTH):
        h = rms_norm(x, norm_mix_g[l])
        proj = h @ w_in[l]
        q, k, v, z, xbc, dt_raw, gate_logits = jnp.split(proj, offsets, axis=-1)

        scale = ATTN_HEAD_DIM ** -0.5
        q = rms_norm(q.reshape(b, s, ATTN_HEADS, 2, ATTN_HEAD_DIM), q_norm_g[l]) * scale
        k = rms_norm(k.reshape(b, s, ATTN_HEADS, 2, ATTN_HEAD_DIM), k_norm_g[l])
        q = jnp.transpose(q, (0, 2, 3, 1, 4))
        k = jnp.transpose(k, (0, 2, 3, 1, 4))
        v = jnp.transpose(v.reshape(b, s, ATTN_HEADS, ATTN_V_DIM), (0, 2, 1, 3))
        lam_init = 0.8 - 0.6 * math.exp(-0.3 * l)
        lam = (jnp.exp(jnp.sum(lambda_q1[l].astype(f32) * lambda_k1[l].astype(f32)))
               - jnp.exp(jnp.sum(lambda_q2[l].astype(f32) * lambda_k2[l].astype(f32)))
               + lam_init)
        o = diff_attention(q, k, v, lam, rel_bias)
        o = rms_norm(o, attn_subln_g[l], SUBLN_EPS) * (1.0 - lam_init)
        y_attn = jnp.transpose(o, (0, 2, 1, 3)).reshape(b, s, V_COLS)

        xbc = jax.nn.silu(causal_dwconv(xbc, conv_ssm_w[l], conv_ssm_b[l]))
        xs, bm, cm = jnp.split(xbc, [SSM_D_INNER, SSM_D_INNER + BC_COLS], axis=-1)
        xs = xs.reshape(b, s, SSM_HEADS, SSM_HEAD_DIM).astype(f32)
        dt = jax.nn.softplus(dt_raw.astype(f32) + dt_bias[l].astype(f32))
        a = -jnp.exp(a_log[l].astype(f32))
        y = ssd_chunked_scan(xs, dt, a,
                             bm.reshape(b, s, SSM_GROUPS, SSM_STATE).astype(f32),
                             cm.reshape(b, s, SSM_GROUPS, SSM_STATE).astype(f32))
        y = y + xs * d_skip[l].astype(f32)[:, None]
        y = y.reshape(b, s, SSM_D_INNER) * jax.nn.silu(z.astype(f32))
        y = rms_norm(y.reshape(b, s, SSM_GROUPS, SSM_D_INNER // SSM_GROUPS),
                     ssm_norm_g[l].reshape(SSM_GROUPS, SSM_D_INNER // SSM_GROUPS), SUBLN_EPS)
        y_ssm = y.reshape(b, s, SSM_D_INNER).astype(x.dtype)

        gates = jax.nn.sigmoid(gate_logits.astype(f32))
        g_attn, g_ssm = gates[..., :D_MODEL], gates[..., D_MODEL:]
        mixed = (g_attn * (y_attn @ w_proj_attn[l]).astype(f32)
                 + g_ssm * (y_ssm @ w_proj_ssm[l]).astype(f32)).astype(x.dtype)
        x = x + mixed @ w_out[l]

        h = rms_norm(x, norm_ffn_g[l])
        u = causal_dwconv(h @ w_up[l], conv_ffn_w[l], conv_ffn_b[l])
        u_gate, u_val = u[..., :D_FF], u[..., D_FF:]
        x = x + (jax.nn.silu(u_gate) * u_val) @ w_down[l]
    return x
```

```python
import functools
import math

import jax
import jax.numpy as jnp
from jax import lax
from jax.experimental import pallas as pl
from jax.experimental.pallas import tpu as pltpu

F32 = jnp.float32
BF16 = jnp.bfloat16

D_MODEL = 1024
ATTN_HEADS = 8
ATTN_HEAD_DIM = 64
ATTN_V_DIM = 2 * ATTN_HEAD_DIM
NUM_BUCKETS = 32
MAX_DISTANCE = 128
SSM_D_INNER = 2048
SSM_HEAD_DIM = 64
SSM_HEADS = 32
SSM_GROUPS = 4
SSM_HEADS_PER_GROUP = SSM_HEADS // SSM_GROUPS
SSM_STATE = 128
SSM_CONV = 4
D_FF = 2816
FFN_CONV = 3
RMS_EPS = 1e-6
SUBLN_EPS = 1e-5
LAM_INIT = 0.8 - 0.6 * math.exp(0.0)

Q_COLS = ATTN_HEADS * 2 * ATTN_HEAD_DIM
V_COLS = ATTN_HEADS * ATTN_V_DIM
BC_COLS = SSM_GROUPS * SSM_STATE
GROUP_COLS = SSM_D_INNER // SSM_GROUPS

LANES = 128
SUBLANES = 8
VMEM_LIMIT = 56 * 1024 * 1024

LOG2E = math.log2(math.e)
NEG_BIG = -1e30

OFF_XS = 0
OFF_Z = OFF_XS + SSM_D_INNER
OFF_GATE = OFF_Z + SSM_D_INNER
OFF_Q = OFF_GATE + 2 * D_MODEL
OFF_K = OFF_Q + Q_COLS
OFF_V = OFF_K + Q_COLS
OFF_B = OFF_V + V_COLS
OFF_C = OFF_B + BC_COLS
PROJ_COLS = OFF_C + BC_COLS

IN_TM = 1024
IN_TN = 1024
ATTN_T = 256
SSD_L = 128
MERGE_TM = 512
FFN_TM = 512
FFN_CHUNKS = 2
FFN_TF = D_FF // FFN_CHUNKS


def _split3(v):
    hi = v.astype(BF16)
    r1 = v - hi.astype(F32)
    mid = r1.astype(BF16)
    lo = (r1 - mid.astype(F32)).astype(BF16)
    return hi, mid, lo


def _dot(a, b):
    return jnp.dot(a, b, preferred_element_type=F32)


def _dot_nt(a, b):
    return lax.dot_general(a, b, (((1,), (1,)), ((), ())), preferred_element_type=F32)


def _dot_exact_rhs(parts, sel):
    out = _dot(parts[0], sel)
    for p in parts[1:]:
        out = out + _dot(p, sel)
    return out


def _dot_exact_lhs(sel, parts):
    out = _dot(sel, parts[0])
    for p in parts[1:]:
        out = out + _dot(sel, p)
    return out


def _silu(v):
    return v * (1.0 / (1.0 + jnp.exp(-v)))


def _in_proj_kernel(x_ref, g_ref, w_ref, wdt_ref, dtb_ref, qkg_ref, bd_ref, o_ref, dt_ref, h_scr,
                    *, q_tile, k_tile):
    j = pl.program_id(1)

    @pl.when(j == 0)
    def _():
        x = x_ref[...]
        ms = jnp.mean(x * x, axis=-1, keepdims=True)
        h = x * lax.rsqrt(ms + RMS_EPS) * g_ref[...]
        h_scr[...] = h.astype(BF16)
        t = _dot(h_scr[...], wdt_ref[...]) + dtb_ref[...]
        dt_ref[...] = jnp.maximum(t, 0.0) + jnp.log1p(jnp.exp(-jnp.abs(t)))

    acc = _dot(h_scr[...], w_ref[...])
    is_qk = jnp.logical_or(j == q_tile, j == k_tile)

    @pl.when(is_qk)
    def _():
        bd = bd_ref[...]
        width = bd.shape[0]
        for cblk in range(IN_TN // width):
            a = acc[:, cblk * width:(cblk + 1) * width]
            sq = a * a
            hi = sq.astype(BF16)
            lo = (sq - hi.astype(F32)).astype(BF16)
            ssum = _dot(hi, bd) + _dot(lo, bd)
            y = a * lax.rsqrt(ssum * (1.0 / ATTN_HEAD_DIM) + RMS_EPS)
            y = y * qkg_ref[0, :, cblk * width:(cblk + 1) * width]
            o_ref[:, cblk * width:(cblk + 1) * width] = y.astype(BF16)

    @pl.when(jnp.logical_not(is_qk))
    def _():
        o_ref[...] = acc.astype(BF16)


def _in_proj(x2, g, w_main, w_dt, dt_bias, qk_gain, bd):
    m = x2.shape[0]
    nj = PROJ_COLS // IN_TN
    q_tile, k_tile = OFF_Q // IN_TN, OFF_K // IN_TN
    kern = functools.partial(_in_proj_kernel, q_tile=q_tile, k_tile=k_tile)
    return pl.pallas_call(
        kern,
        grid=(m // IN_TM, nj),
        in_specs=[
            pl.BlockSpec((IN_TM, D_MODEL), lambda i, j: (i, 0)),
            pl.BlockSpec((1, D_MODEL), lambda i, j: (0, 0)),
            pl.BlockSpec((D_MODEL, IN_TN), lambda i, j: (0, j)),
            pl.BlockSpec((D_MODEL, LANES), lambda i, j: (0, 0)),
            pl.BlockSpec((1, LANES), lambda i, j: (0, 0)),
            pl.BlockSpec((1, 1, IN_TN), lambda i, j: (jnp.where(j == k_tile, 1, 0), 0, 0)),
            pl.BlockSpec(bd.shape, lambda i, j: (0, 0)),
        ],
        out_specs=[
            pl.BlockSpec((IN_TM, IN_TN), lambda i, j: (i, j)),
            pl.BlockSpec((IN_TM, LANES), lambda i, j: (i, 0)),
        ],
        out_shape=[
            jax.ShapeDtypeStruct((m, PROJ_COLS), BF16),
            jax.ShapeDtypeStruct((m, LANES), F32),
        ],
        scratch_shapes=[pltpu.VMEM((IN_TM, D_MODEL), BF16)],
        compiler_params=pltpu.CompilerParams(
            dimension_semantics=("parallel", "arbitrary"), vmem_limit_bytes=VMEM_LIMIT),
        name="in_proj",
    )(x2, g, w_main, w_dt, dt_bias, qk_gain, bd)


def _bias_kernel(rel_ref, lq1_ref, lk1_ref, lq2_ref, lk2_ref, bias_ref, far_ref, lam_ref, *, tile):
    h = pl.program_id(0)
    row = lax.broadcasted_iota(jnp.int32, (tile, tile), 0)
    col = lax.broadcasted_iota(jnp.int32, (tile, tile), 1)
    max_exact = NUM_BUCKETS // 2
    for t in range(2):
        d = row - col + t * tile
        n = jnp.maximum(d, 0)
        nf = jnp.maximum(n, 1).astype(F32)
        large = max_exact + (jnp.log(nf / max_exact) / math.log(MAX_DISTANCE / max_exact)
                             * (NUM_BUCKETS - max_exact)).astype(jnp.int32)
        large = jnp.minimum(large, NUM_BUCKETS - 1)
        bucket = jnp.where(n < max_exact, n, large)
        bias = jnp.zeros((tile, tile), F32)
        for bk in range(NUM_BUCKETS):
            bias = jnp.where(bucket == bk, rel_ref[bk, h], bias)
        bias = bias * LOG2E
        if t == 0:
            bias = jnp.where(d >= 0, bias, NEG_BIG)
        bias_ref[0, t] = bias
    far_ref[...] = jnp.full(far_ref.shape, rel_ref[NUM_BUCKETS - 1, h] * LOG2E, F32)
    s1 = jnp.sum(lq1_ref[...] * lk1_ref[...], axis=-1, keepdims=True)
    s2 = jnp.sum(lq2_ref[...] * lk2_ref[...], axis=-1, keepdims=True)
    lam = jnp.exp(s1) - jnp.exp(s2) + LAM_INIT
    lam_ref[...] = jnp.broadcast_to(lam, lam_ref.shape)


def _bias_build(rel_bias, lq1, lk1, lq2, lk2, tile):
    vec = pl.BlockSpec((1, ATTN_HEAD_DIM), lambda h: (0, 0))
    return pl.pallas_call(
        functools.partial(_bias_kernel, tile=tile),
        grid=(ATTN_HEADS,),
        in_specs=[pl.BlockSpec(memory_space=pltpu.SMEM), vec, vec, vec, vec],
        out_specs=[
            pl.BlockSpec((1, 2, tile, tile), lambda h: (h, 0, 0, 0)),
            pl.BlockSpec((1, SUBLANES, LANES), lambda h: (h, 0, 0)),
            pl.BlockSpec((SUBLANES, LANES), lambda h: (0, 0)),
        ],
        out_shape=[
            jax.ShapeDtypeStruct((ATTN_HEADS, 2, tile, tile), F32),
            jax.ShapeDtypeStruct((ATTN_HEADS, SUBLANES, LANES), F32),
            jax.ShapeDtypeStruct((SUBLANES, LANES), F32),
        ],
        compiler_params=pltpu.CompilerParams(dimension_semantics=("arbitrary",)),
        name="bias_build",
    )(rel_bias, lq1, lk1, lq2, lk2)


def _attn_kernel(lam_ref, far_ref, q_ref, k_ref, v_ref, bias_ref, g_ref, o_ref, *, tile, nq):
    h = pl.program_id(1)
    lam = lam_ref[0]
    far = far_ref[h]
    lane = lax.broadcasted_iota(jnp.int32, (tile, LANES), 1)
    lo_half = lane < ATTN_HEAD_DIM
    gain = g_ref[...]

    def q_block(qi, carry_unused):
        qs = pl.multiple_of(qi * tile, tile)
        q = q_ref[0, pl.ds(qs, tile), :]
        zero = jnp.zeros_like(q)
        q1 = jnp.where(lo_half, q, zero)
        q2 = jnp.where(lo_half, zero, q)

        def update(s, shift, state, v):
            m, l, acc = state
            m_new = jnp.maximum(m, jnp.max(s, axis=-1, keepdims=True) + shift)
            alpha = jnp.exp2(m - m_new)
            p = jnp.exp2(s - (m_new - shift))
            l = alpha * l + jnp.sum(p, axis=-1, keepdims=True)
            acc = alpha * acc + _dot(p.astype(BF16), v)
            return m_new, l, acc

        def load_kv(kj):
            ks = pl.multiple_of(kj * tile, tile)
            return k_ref[0, pl.ds(ks, tile), :], v_ref[0, pl.ds(ks, tile), :]

        def near_body(t, carry):
            st1, st2 = carry
            k, v = load_kv(qi - t)
            b = bias_ref[0, t]
            st1 = update(_dot_nt(q1, k) + b, 0.0, st1, v)
            st2 = update(_dot_nt(q2, k) + b, 0.0, st2, v)
            return st1, st2

        def far_body(kj, carry):
            st1, st2 = carry
            k, v = load_kv(kj)
            st1 = update(_dot_nt(q1, k), far, st1, v)
            st2 = update(_dot_nt(q2, k), far, st2, v)
            return st1, st2

        init = (jnp.full((tile, 1), NEG_BIG, F32), jnp.zeros((tile, 1), F32),
                jnp.zeros((tile, LANES), F32))
        carry = lax.fori_loop(0, jnp.minimum(qi + 1, 2), near_body, (init, init))
        carry = lax.fori_loop(0, jnp.maximum(qi - 1, 0), far_body, carry)
        (_, l1, acc1), (_, l2, acc2) = carry
        o = acc1 / l1 - lam * (acc2 / l2)
        ms = jnp.mean(o * o, axis=-1, keepdims=True)
        o = o * lax.rsqrt(ms + SUBLN_EPS) * gain
        o_ref[0, pl.ds(qs, tile), :] = o.astype(o_ref.dtype)
        return carry_unused

    lax.fori_loop(0, nq, q_block, 0)


def _diff_attn(lam, far, proj3, bias, gain):
    b, s, _ = proj3.shape
    tile = bias.shape[-1]
    nq = s // tile
    qb, kb, vb = OFF_Q // LANES, OFF_K // LANES, OFF_V // LANES
    kern = functools.partial(_attn_kernel, tile=tile, nq=nq)
    return pl.pallas_call(
        kern,
        grid=(b, ATTN_HEADS),
        in_specs=[
            pl.BlockSpec(memory_space=pltpu.SMEM),
            pl.BlockSpec(memory_space=pltpu.SMEM),
            pl.BlockSpec((1, s, LANES), lambda i, h: (i, 0, qb + h)),
            pl.BlockSpec((1, s, LANES), lambda i, h: (i, 0, kb + h)),
            pl.BlockSpec((1, s, LANES), lambda i, h: (i, 0, vb + h)),
            pl.BlockSpec((1, 2, tile, tile), lambda i, h: (h, 0, 0, 0)),
            pl.BlockSpec((1, LANES), lambda i, h: (0, 0)),
        ],
        out_specs=pl.BlockSpec((1, s, LANES), lambda i, h: (i, 0, h)),
        out_shape=jax.ShapeDtypeStruct((b, s, V_COLS), BF16),
        compiler_params=pltpu.CompilerParams(
            dimension_semantics=("parallel", "parallel"), vmem_limit_bytes=VMEM_LIMIT),
        name="diff_attn",
    )(lam, far, proj3, proj3, proj3, bias, gain)


def _conv_silu(main_ref, halo_ref, w_ref, b_ref, scr, first, length):
    halo = halo_ref[0].astype(F32)
    scr[0:SUBLANES, :] = jnp.where(first, jnp.zeros_like(halo), halo)
    scr[SUBLANES:SUBLANES + length, :] = main_ref[0].astype(F32)
    taps = w_ref.shape[0]
    acc = b_ref[...] + w_ref[taps - 1:taps, :] * scr[SUBLANES:SUBLANES + length, :]
    for k in range(taps - 1):
        off = SUBLANES - (taps - 1) + k
        acc = acc + w_ref[k:k + 1, :] * scr[off:off + length, :]
    return _silu(acc)


def _ssd_kernel(xs_ref, xh_ref, bm_ref, bh_ref, cm_ref, ch_ref, z_ref, dt_ref, alog_ref,
                wx_ref, bx_ref, wb_ref, bb_ref, wc_ref, bc_ref, dskip_ref, ng_ref, e_ref,
                y_ref, state_scr, xscr, bscr, cscr, *, length):
    c = pl.program_id(1)
    first = c == 0

    @pl.when(first)
    def _():
        state_scr[...] = jnp.zeros_like(state_scr)

    xs = _conv_silu(xs_ref, xh_ref, wx_ref, bx_ref, xscr, first, length)
    bm = _conv_silu(bm_ref, bh_ref, wb_ref, bb_ref, bscr, first, length)
    cm = _conv_silu(cm_ref, ch_ref, wc_ref, bc_ref, cscr, first, length)

    dt = dt_ref[0]
    a = -jnp.exp(alog_ref[...])
    adt = dt * a
    row = lax.broadcasted_iota(jnp.int32, (length, length), 0)
    col = lax.broadcasted_iota(jnp.int32, (length, length), 1)
    causal = row >= col
    tri = causal.astype(BF16)
    cum = _dot_exact_lhs(tri, _split3(adt))
    cum_t = cum.T
    dt_t = dt.T
    total = cum[length - 1:length, :]
    sel = e_ref[...]
    ecum_x = _dot_exact_rhs(_split3(jnp.exp(cum)), sel)
    edtw_x = _dot_exact_rhs(_split3(dt * jnp.exp(total - cum)), sel)
    etot_x = ecum_x[length - 1:length, :]

    lane = lax.broadcasted_iota(jnp.int32, (length, LANES), 1)
    lo_half = lane < SSM_HEAD_DIM

    for g in range(SSM_GROUPS):
        gs = slice(g * SSM_STATE, (g + 1) * SSM_STATE)
        cs = slice(g * GROUP_COLS, (g + 1) * GROUP_COLS)
        b_g = bm[:, gs].astype(BF16)
        c_g = cm[:, gs].astype(BF16)
        cb = jnp.where(causal, _dot_nt(c_g, b_g), 0.0)
        st = state_scr[g]
        y_g = _dot(c_g, st.astype(BF16)) * ecum_x[:, cs]
        xw = (xs[:, cs] * edtw_x[:, cs]).astype(BF16)
        state_scr[g] = st * etot_x[:, cs] + _dot(bm[:, gs].T.astype(BF16), xw)
        y_parts = []
        for pair in range(SSM_HEADS_PER_GROUP // 2):
            ms = []
            for hh in range(2):
                h = g * SSM_HEADS_PER_GROUP + pair * 2 + hh
                seg = cum[:, h:h + 1] - cum_t[h:h + 1, :]
                dec = jnp.exp(jnp.minimum(seg, 0.0))
                ms.append((cb * dec * dt_t[h:h + 1, :]).astype(BF16))
            x_pair = xs[:, g * GROUP_COLS + pair * LANES: g * GROUP_COLS + (pair + 1) * LANES]
            zero = jnp.zeros_like(x_pair)
            rhs = jnp.concatenate([jnp.where(lo_half, x_pair, zero),
                                   jnp.where(lo_half, zero, x_pair)], axis=0).astype(BF16)
            y_parts.append(_dot(jnp.concatenate(ms, axis=1), rhs))
        y_g = y_g + jnp.concatenate(y_parts, axis=1)
        y_g = y_g + xs[:, cs] * dskip_ref[:, cs]
        y_g = y_g * _silu(z_ref[0, :, cs].astype(F32))
        msq = jnp.mean(y_g * y_g, axis=-1, keepdims=True)
        y_g = y_g * lax.rsqrt(msq + SUBLN_EPS) * ng_ref[:, cs]
        y_ref[0, :, cs] = y_g.astype(y_ref.dtype)


def _ssd(proj3, dt3, alog, wx, bx, wb, bb, wc, bc, dskip_x, ng, sel):
    b, s, _ = proj3.shape
    length = SSD_L
    nc = s // length
    hb = length // SUBLANES
    xs_blk, z_blk = OFF_XS // SSM_D_INNER, OFF_Z // SSM_D_INNER
    b_blk, c_blk = OFF_B // BC_COLS, OFF_C // BC_COLS

    def main(width, blk):
        return pl.BlockSpec((1, length, width), lambda i, c: (i, c, blk))

    def halo(width, blk):
        return pl.BlockSpec((1, SUBLANES, width), lambda i, c: (i, jnp.maximum(c * hb - 1, 0), blk))

    def const(shape):
        return pl.BlockSpec(shape, lambda i, c: (0,) * len(shape))

    kern = functools.partial(_ssd_kernel, length=length)
    return pl.pallas_call(
        kern,
        grid=(b, nc),
        in_specs=[
            main(SSM_D_INNER, xs_blk), halo(SSM_D_INNER, xs_blk),
            main(BC_COLS, b_blk), halo(BC_COLS, b_blk),
            main(BC_COLS, c_blk), halo(BC_COLS, c_blk),
            main(SSM_D_INNER, z_blk),
            pl.BlockSpec((1, length, LANES), lambda i, c: (i, c, 0)),
            const((1, LANES)),
            const((SSM_CONV, SSM_D_INNER)), const((1, SSM_D_INNER)),
            const((SSM_CONV, BC_COLS)), const((1, BC_COLS)),
            const((SSM_CONV, BC_COLS)), const((1, BC_COLS)),
            const((1, SSM_D_INNER)), const((1, SSM_D_INNER)),
            const((LANES, SSM_D_INNER)),
        ],
        out_specs=pl.BlockSpec((1, length, SSM_D_INNER), lambda i, c: (i, c, 0)),
        out_shape=jax.ShapeDtypeStruct((b, s, SSM_D_INNER), BF16),
        scratch_shapes=[
            pltpu.VMEM((SSM_GROUPS, SSM_STATE, GROUP_COLS), F32),
            pltpu.VMEM((length + SUBLANES, SSM_D_INNER), F32),
            pltpu.VMEM((length + SUBLANES, BC_COLS), F32),
            pltpu.VMEM((length + SUBLANES, BC_COLS), F32),
        ],
        compiler_params=pltpu.CompilerParams(
            dimension_semantics=("parallel", "arbitrary"), vmem_limit_bytes=VMEM_LIMIT),
        name="ssd",
    )(proj3, proj3, proj3, proj3, proj3, proj3, proj3, dt3, alog,
      wx, bx, wb, bb, wc, bc, dskip_x, ng, sel)


def _merge_kernel(x_ref, ya_ref, ys_ref, ga_ref, gs_ref, wpa_ref, wps_ref, wo_ref, g_ref,
                  x1_ref, h2_ref):
    pa = _dot(ya_ref[...], wpa_ref[...])
    ps = _dot(ys_ref[...], wps_ref[...])
    ga = 1.0 / (1.0 + jnp.exp(-ga_ref[...].astype(F32)))
    gs = 1.0 / (1.0 + jnp.exp(-gs_ref[...].astype(F32)))
    mixed = (ga * pa + gs * ps).astype(BF16)
    x1 = x_ref[...] + _dot(mixed, wo_ref[...])
    x1_ref[...] = x1
    ms = jnp.mean(x1 * x1, axis=-1, keepdims=True)
    h2_ref[...] = (x1 * lax.rsqrt(ms + RMS_EPS) * g_ref[...]).astype(BF16)


def _merge(x2, y_attn, y_ssm, proj, wpa, wps, wo, g):
    m = x2.shape[0]
    tm = MERGE_TM
    ga_blk = OFF_GATE // D_MODEL

    def rows(width, blk=0):
        return pl.BlockSpec((tm, width), lambda i: (i, blk))

    def const(shape):
        return pl.BlockSpec(shape, lambda i: (0, 0))

    return pl.pallas_call(
        _merge_kernel,
        grid=(m // tm,),
        in_specs=[
            rows(D_MODEL), rows(V_COLS), rows(SSM_D_INNER),
            rows(D_MODEL, ga_blk), rows(D_MODEL, ga_blk + 1),
            const((V_COLS, D_MODEL)), const((SSM_D_INNER, D_MODEL)), const((D_MODEL, D_MODEL)),
            const((1, D_MODEL)),
        ],
        out_specs=[rows(D_MODEL), rows(D_MODEL)],
        out_shape=[jax.ShapeDtypeStruct((m, D_MODEL), F32),
                   jax.ShapeDtypeStruct((m, D_MODEL), BF16)],
        compiler_params=pltpu.CompilerParams(
            dimension_semantics=("parallel",), vmem_limit_bytes=VMEM_LIMIT),
        name="merge",
    )(x2, y_attn, y_ssm, proj, proj, wpa, wps, wo, g)


def _ffn_kernel(x1_ref, h_ref, hh_ref, wg_ref, wv_ref, cwg_ref, cbg_ref, cwv_ref, cbv_ref, wd_ref,
                o_ref, acc_scr, gscr, vscr, *, tm, tiles_per_seq):
    i = pl.program_id(0)
    f = pl.program_id(1)
    first = (i % tiles_per_seq) == 0
    h = h_ref[...]
    halo = hh_ref[...]
    halo = jnp.where(first, jnp.zeros_like(halo), halo)

    def branch(w_ref, cw_ref, cb_ref, scr):
        w = w_ref[...]
        scr[0:SUBLANES, :] = _dot(halo, w)
        scr[SUBLANES:SUBLANES + tm, :] = _dot(h, w)
        taps = cw_ref.shape[0]
        out = cb_ref[...] + cw_ref[taps - 1:taps, :] * scr[SUBLANES:SUBLANES + tm, :]
        for k in range(taps - 1):
            off = SUBLANES - (taps - 1) + k
            out = out + cw_ref[k:k + 1, :] * scr[off:off + tm, :]
        return out

    ug = branch(wg_ref, cwg_ref, cbg_ref, gscr)
    uv = branch(wv_ref, cwv_ref, cbv_ref, vscr)
    act = (_silu(ug) * uv).astype(BF16)
    part = _dot(act, wd_ref[...])

    @pl.when(f == 0)
    def _():
        acc_scr[...] = x1_ref[...] + part

    @pl.when(f != 0)
    def _():
        acc_scr[...] += part

    @pl.when(f == pl.num_programs(1) - 1)
    def _():
        o_ref[...] = acc_scr[...]


def _ffn(x1, h2, w_up, cw, cb, w_down, seq):
    m = x1.shape[0]
    tm = FFN_TM
    tf = FFN_TF
    nf = FFN_CHUNKS
    hb = tm // SUBLANES
    kern = functools.partial(_ffn_kernel, tm=tm, tiles_per_seq=seq // tm)
    return pl.pallas_call(
        kern,
        grid=(m // tm, nf),
        in_specs=[
            pl.BlockSpec((tm, D_MODEL), lambda i, f: (i, 0)),
            pl.BlockSpec((tm, D_MODEL), lambda i, f: (i, 0)),
            pl.BlockSpec((SUBLANES, D_MODEL), lambda i, f: (jnp.maximum(i * hb - 1, 0), 0)),
            pl.BlockSpec((D_MODEL, tf), lambda i, f: (0, f)),
            pl.BlockSpec((D_MODEL, tf), lambda i, f: (0, nf + f)),
            pl.BlockSpec((FFN_CONV, tf), lambda i, f: (0, f)),
            pl.BlockSpec((1, tf), lambda i, f: (0, f)),
            pl.BlockSpec((FFN_CONV, tf), lambda i, f: (0, nf + f)),
            pl.BlockSpec((1, tf), lambda i, f: (0, nf + f)),
            pl.BlockSpec((tf, D_MODEL), lambda i, f: (f, 0)),
        ],
        out_specs=pl.BlockSpec((tm, D_MODEL), lambda i, f: (i, 0)),
        out_shape=jax.ShapeDtypeStruct((m, D_MODEL), F32),
        scratch_shapes=[
            pltpu.VMEM((tm, D_MODEL), F32),
            pltpu.VMEM((tm + SUBLANES, tf), F32),
            pltpu.VMEM((tm + SUBLANES, tf), F32),
        ],
        compiler_params=pltpu.CompilerParams(
            dimension_semantics=("parallel", "arbitrary"), vmem_limit_bytes=VMEM_LIMIT),
        name="ffn",
    )(x1, h2, h2, w_up, w_up, cw, cb, cw, cb, w_down)


def _block_diag_ones(width, chunk):
    r = lax.broadcasted_iota(jnp.int32, (width, width), 0) // chunk
    c = lax.broadcasted_iota(jnp.int32, (width, width), 1) // chunk
    return (r == c).astype(BF16)


def _head_expand(rows, cols, chunk):
    r = lax.broadcasted_iota(jnp.int32, (rows, cols), 0)
    c = lax.broadcasted_iota(jnp.int32, (rows, cols), 1) // chunk
    return (r == c).astype(BF16)


def _layer(x, rel_bias, norm_mix_g, w_in, q_norm_g, k_norm_g, lq1, lk1, lq2, lk2, attn_subln_g,
           conv_ssm_w, conv_ssm_b, dt_bias, a_log, d_skip, ssm_norm_g, w_proj_attn, w_proj_ssm,
           w_out, norm_ffn_g, w_up, conv_ffn_w, conv_ffn_b, w_down):
    b, s, d = x.shape
    m = b * s
    x2 = x.reshape(m, d)

    o_q, o_k, o_v = 0, Q_COLS, 2 * Q_COLS
    o_z = o_v + V_COLS
    o_xs = o_z + SSM_D_INNER
    o_b = o_xs + SSM_D_INNER
    o_c = o_b + BC_COLS
    o_dt = o_c + BC_COLS
    o_gate = o_dt + SSM_HEADS

    def cols(start, width):
        return w_in[:, start:start + width]

    w_main = jnp.concatenate([
        cols(o_xs, SSM_D_INNER), cols(o_z, SSM_D_INNER), cols(o_gate, 2 * D_MODEL),
        cols(o_q, Q_COLS), cols(o_k, Q_COLS), cols(o_v, V_COLS),
        cols(o_b, BC_COLS), cols(o_c, BC_COLS)], axis=1).astype(BF16)
    pad_h = LANES - SSM_HEADS
    w_dt = jnp.pad(cols(o_dt, SSM_HEADS), ((0, 0), (0, pad_h))).astype(BF16)
    dt_b = jnp.pad(dt_bias.astype(F32), (0, pad_h)).reshape(1, LANES)
    reps = Q_COLS // ATTN_HEAD_DIM
    qk_gain = jnp.stack([
        jnp.tile(q_norm_g.astype(F32) * (ATTN_HEAD_DIM ** -0.5 * LOG2E), reps),
        jnp.tile(k_norm_g.astype(F32), reps)]).reshape(2, 1, Q_COLS)
    bd = _block_diag_ones(2 * LANES, ATTN_HEAD_DIM)

    proj, dt = _in_proj(x2, norm_mix_g.astype(F32).reshape(1, d), w_main, w_dt, dt_b, qk_gain, bd)
    proj3 = proj.reshape(b, s, PROJ_COLS)

    vec = lambda v: v.astype(F32).reshape(1, -1)
    bias, far, lam = _bias_build(rel_bias.astype(F32), vec(lq1), vec(lk1), vec(lq2), vec(lk2), ATTN_T)
    y_attn = _diff_attn(lam[0, :1], far[:, 0, 0], proj3, bias,
                        vec(attn_subln_g) * (1.0 - LAM_INIT))

    cw = conv_ssm_w.astype(F32)
    cbias = conv_ssm_b.astype(F32).reshape(1, -1)
    xs_sl = slice(0, SSM_D_INNER)
    b_sl = slice(SSM_D_INNER, SSM_D_INNER + BC_COLS)
    c_sl = slice(SSM_D_INNER + BC_COLS, SSM_D_INNER + 2 * BC_COLS)
    y_ssm = _ssd(
        proj3, dt.reshape(b, s, LANES),
        jnp.pad(a_log.astype(F32), (0, pad_h)).reshape(1, LANES),
        cw[:, xs_sl], cbias[:, xs_sl], cw[:, b_sl], cbias[:, b_sl], cw[:, c_sl], cbias[:, c_sl],
        jnp.repeat(d_skip.astype(F32), SSM_HEAD_DIM).reshape(1, SSM_D_INNER),
        vec(ssm_norm_g), _head_expand(LANES, SSM_D_INNER, SSM_HEAD_DIM))

    x1, h2 = _merge(x2, y_attn.reshape(m, V_COLS), y_ssm.reshape(m, SSM_D_INNER), proj,
                    w_proj_attn.astype(BF16), w_proj_ssm.astype(BF16), w_out.astype(BF16),
                    vec(norm_ffn_g))

    out = _ffn(x1, h2, w_up.astype(BF16), conv_ffn_w.astype(F32),
               conv_ffn_b.astype(F32).reshape(1, -1), w_down.astype(BF16), s)
    return out.reshape(b, s, d)


def kernel(x, rel_bias, norm_mix_g, w_in, q_norm_g, k_norm_g, lambda_q1, lambda_k1, lambda_q2,
           lambda_k2, attn_subln_g, conv_ssm_w, conv_ssm_b, dt_bias, a_log, d_skip, ssm_norm_g,
           w_proj_attn, w_proj_ssm, w_out, norm_ffn_g, w_up, conv_ffn_w, conv_ffn_b, w_down):
    depth = w_in.shape[0]
    assert depth == 1, "lambda_init and the stacked-parameter layout are specialised to depth 1"
    return _layer(x, rel_bias, norm_mix_g[0], w_in[0], q_norm_g[0], k_norm_g[0], lambda_q1[0],
                  lambda_k1[0], lambda_q2[0], lambda_k2[0], attn_subln_g[0], conv_ssm_w[0],
                  conv_ssm_b[0], dt_bias[0], a_log[0], d_skip[0], ssm_norm_g[0], w_proj_attn[0],
                  w_proj_ssm[0], w_out[0], norm_ffn_g[0], w_up[0], conv_ffn_w[0], conv_ffn_b[0],
                  w_down[0])
```

```python
import functools
import math

import jax
import jax.numpy as jnp
from jax import lax
from jax.experimental import pallas as pl
from jax.experimental.pallas import tpu as pltpu

F32 = jnp.float32
BF16 = jnp.bfloat16

D_MODEL = 1024
ATTN_HEADS = 8
ATTN_HEAD_DIM = 64
ATTN_V_DIM = 2 * ATTN_HEAD_DIM
NUM_BUCKETS = 32
MAX_DISTANCE = 128
SSM_D_INNER = 2048
SSM_HEAD_DIM = 64
SSM_HEADS = 32
SSM_GROUPS = 4
SSM_HEADS_PER_GROUP = SSM_HEADS // SSM_GROUPS
SSM_STATE = 128
SSM_CONV = 4
D_FF = 2816
FFN_CONV = 3
RMS_EPS = 1e-6
SUBLN_EPS = 1e-5
LAM_INIT = 0.8 - 0.6 * math.exp(0.0)

Q_COLS = ATTN_HEADS * 2 * ATTN_HEAD_DIM
V_COLS = ATTN_HEADS * ATTN_V_DIM
BC_COLS = SSM_GROUPS * SSM_STATE
GROUP_COLS = SSM_D_INNER // SSM_GROUPS

LANES = 128
SUBLANES = 8
VMEM_LIMIT = 56 * 1024 * 1024

LOG2E = math.log2(math.e)
NEG_BIG = -1e30

OFF_XS = 0
OFF_Z = OFF_XS + SSM_D_INNER
OFF_GATE = OFF_Z + SSM_D_INNER
OFF_Q = OFF_GATE + 2 * D_MODEL
OFF_K = OFF_Q + Q_COLS
OFF_V = OFF_K + Q_COLS
OFF_B = OFF_V + V_COLS
OFF_C = OFF_B + BC_COLS
PROJ_COLS = OFF_C + BC_COLS

IN_TM = 1024
IN_TN = 1024
ATTN_T = 256
SSD_L = 128
MERGE_TM = 512
FFN_TM = 512
FFN_CHUNKS = 2
FFN_TF = D_FF // FFN_CHUNKS


def _split3(v):
    hi = v.astype(BF16)
    r1 = v - hi.astype(F32)
    mid = r1.astype(BF16)
    lo = (r1 - mid.astype(F32)).astype(BF16)
    return hi, mid, lo


def _dot(a, b):
    return jnp.dot(a, b, preferred_element_type=F32)


def _dot_nt(a, b):
    return lax.dot_general(a, b, (((1,), (1,)), ((), ())), preferred_element_type=F32)


def _dot_exact_rhs(parts, sel):
    out = _dot(parts[0], sel)
    for p in parts[1:]:
        out = out + _dot(p, sel)
    return out


def _dot_exact_lhs(sel, parts):
    out = _dot(sel, parts[0])
    for p in parts[1:]:
        out = out + _dot(sel, p)
    return out


def _silu(v):
    return v * (1.0 / (1.0 + jnp.exp(-v)))


def _in_proj_kernel(x_ref, g_ref, w_ref, wdt_ref, dtb_ref, qkg_ref, bd_ref, o_ref, dt_ref, h_scr,
                    *, q_tile, k_tile):
    j = pl.program_id(1)

    @pl.when(j == 0)
    def _():
        x = x_ref[...]
        ms = jnp.mean(x * x, axis=-1, keepdims=True)
        h = x * lax.rsqrt(ms + RMS_EPS) * g_ref[...]
        h_scr[...] = h.astype(BF16)
        t = _dot(h_scr[...], wdt_ref[...]) + dtb_ref[...]
        dt_ref[...] = jnp.maximum(t, 0.0) + jnp.log1p(jnp.exp(-jnp.abs(t)))

    acc = _dot(h_scr[...], w_ref[...])
    is_qk = jnp.logical_or(j == q_tile, j == k_tile)

    @pl.when(is_qk)
    def _():
        bd = bd_ref[...]
        width = bd.shape[0]
        for cblk in range(IN_TN // width):
            a = acc[:, cblk * width:(cblk + 1) * width]
            sq = a * a
            hi = sq.astype(BF16)
            lo = (sq - hi.astype(F32)).astype(BF16)
            ssum = _dot(hi, bd) + _dot(lo, bd)
            y = a * lax.rsqrt(ssum * (1.0 / ATTN_HEAD_DIM) + RMS_EPS)
            y = y * qkg_ref[0, :, cblk * width:(cblk + 1) * width]
            o_ref[:, cblk * width:(cblk + 1) * width] = y.astype(BF16)

    @pl.when(jnp.logical_not(is_qk))
    def _():
        o_ref[...] = acc.astype(BF16)


def _in_proj(x2, g, w_main, w_dt, dt_bias, qk_gain, bd):
    m = x2.shape[0]
    nj = PROJ_COLS // IN_TN
    q_tile, k_tile = OFF_Q // IN_TN, OFF_K // IN_TN
    kern = functools.partial(_in_proj_kernel, q_tile=q_tile, k_tile=k_tile)
    return pl.pallas_call(
        kern,
        grid=(m // IN_TM, nj),
        in_specs=[
            pl.BlockSpec((IN_TM, D_MODEL), lambda i, j: (i, 0)),
            pl.BlockSpec((1, D_MODEL), lambda i, j: (0, 0)),
            pl.BlockSpec((D_MODEL, IN_TN), lambda i, j: (0, j)),
            pl.BlockSpec((D_MODEL, LANES), lambda i, j: (0, 0)),
            pl.BlockSpec((1, LANES), lambda i, j: (0, 0)),
            pl.BlockSpec((1, 1, IN_TN), lambda i, j: (jnp.where(j == k_tile, 1, 0), 0, 0)),
            pl.BlockSpec(bd.shape, lambda i, j: (0, 0)),
        ],
        out_specs=[
            pl.BlockSpec((IN_TM, IN_TN), lambda i, j: (i, j)),
            pl.BlockSpec((IN_TM, LANES), lambda i, j: (i, 0)),
        ],
        out_shape=[
            jax.ShapeDtypeStruct((m, PROJ_COLS), BF16),
            jax.ShapeDtypeStruct((m, LANES), F32),
        ],
        scratch_shapes=[pltpu.VMEM((IN_TM, D_MODEL), BF16)],
        compiler_params=pltpu.CompilerParams(
            dimension_semantics=("parallel", "arbitrary"), vmem_limit_bytes=VMEM_LIMIT),
        name="in_proj",
    )(x2, g, w_main, w_dt, dt_bias, qk_gain, bd)


def _bias_kernel(rel_ref, lq1_ref, lk1_ref, lq2_ref, lk2_ref, bias_ref, far_ref, lam_ref, *, tile):
    h = pl.program_id(0)
    row = lax.broadcasted_iota(jnp.int32, (tile, tile), 0)
    col = lax.broadcasted_iota(jnp.int32, (tile, tile), 1)
    max_exact = NUM_BUCKETS // 2
    for t in range(2):
        d = row - col + t * tile
        n = jnp.maximum(d, 0)
        nf = jnp.maximum(n, 1).astype(F32)
        large = max_exact + (jnp.log(nf / max_exact) / math.log(MAX_DISTANCE / max_exact)
                             * (NUM_BUCKETS - max_exact)).astype(jnp.int32)
        large = jnp.minimum(large, NUM_BUCKETS - 1)
        bucket = jnp.where(n < max_exact, n, large)
        bias = jnp.zeros((tile, tile), F32)
        for bk in range(NUM_BUCKETS):
            bias = jnp.where(bucket == bk, rel_ref[bk, h], bias)
        bias = bias * LOG2E
        if t == 0:
            bias = jnp.where(d >= 0, bias, NEG_BIG)
        bias_ref[0, t] = bias
    far_ref[...] = jnp.full(far_ref.shape, rel_ref[NUM_BUCKETS - 1, h] * LOG2E, F32)
    s1 = jnp.sum(lq1_ref[...] * lk1_ref[...], axis=-1, keepdims=True)
    s2 = jnp.sum(lq2_ref[...] * lk2_ref[...], axis=-1, keepdims=True)
    lam = jnp.exp(s1) - jnp.exp(s2) + LAM_INIT
    lam_ref[...] = jnp.broadcast_to(lam, lam_ref.shape)


def _bias_build(rel_bias, lq1, lk1, lq2, lk2, tile):
    vec = pl.BlockSpec((1, ATTN_HEAD_DIM), lambda h: (0, 0))
    return pl.pallas_call(
        functools.partial(_bias_kernel, tile=tile),
        grid=(ATTN_HEADS,),
        in_specs=[pl.BlockSpec(memory_space=pltpu.SMEM), vec, vec, vec, vec],
        out_specs=[
            pl.BlockSpec((1, 2, tile, tile), lambda h: (h, 0, 0, 0)),
            pl.BlockSpec((1, SUBLANES, LANES), lambda h: (h, 0, 0)),
            pl.BlockSpec((SUBLANES, LANES), lambda h: (0, 0)),
        ],
        out_shape=[
            jax.ShapeDtypeStruct((ATTN_HEADS, 2, tile, tile), F32),
            jax.ShapeDtypeStruct((ATTN_HEADS, SUBLANES, LANES), F32),
            jax.ShapeDtypeStruct((SUBLANES, LANES), F32),
        ],
        compiler_params=pltpu.CompilerParams(dimension_semantics=("arbitrary",)),
        name="bias_build",
    )(rel_bias, lq1, lk1, lq2, lk2)


def _attn_kernel(lam_ref, far_ref, q_ref, k_ref, v_ref, bias_ref, g_ref, o_ref, *, tile, nq):
    h = pl.program_id(1)
    lam = lam_ref[0]
    far = far_ref[h]
    lane = lax.broadcasted_iota(jnp.int32, (tile, LANES), 1)
    lo_half = lane < ATTN_HEAD_DIM
    gain = g_ref[...]

    for qi in range(nq):
        nk = (qi + 1) * tile
        n_near = min(qi + 1, 2) * tile
        n_far = nk - n_near
        q = q_ref[0, qi * tile:(qi + 1) * tile, :]
        zero = jnp.zeros_like(q)
        keys = k_ref[0, 0:nk, :]
        vals = v_ref[0, 0:nk, :]
        if qi == 0:
            near_bias = bias_ref[0, 0]
        else:
            near_bias = jnp.concatenate([bias_ref[0, 1], bias_ref[0, 0]], axis=1)
        outs = []
        for q_map in (jnp.where(lo_half, q, zero), jnp.where(lo_half, zero, q)):
            s = _dot_nt(q_map, keys)
            s_near = s[:, n_far:] + near_bias
            m = jnp.max(s_near, axis=-1, keepdims=True)
            if n_far:
                s_far = s[:, :n_far]
                m = jnp.maximum(m, jnp.max(s_far, axis=-1, keepdims=True) + far)
                p = jnp.concatenate([jnp.exp2(s_far - (m - far)), jnp.exp2(s_near - m)], axis=1)
            else:
                p = jnp.exp2(s_near - m)
            l = jnp.sum(p, axis=-1, keepdims=True)
            outs.append(_dot(p.astype(BF16), vals) / l)
        o = outs[0] - lam * outs[1]
        ms = jnp.mean(o * o, axis=-1, keepdims=True)
        o = o * lax.rsqrt(ms + SUBLN_EPS) * gain
        o_ref[0, qi * tile:(qi + 1) * tile, :] = o.astype(o_ref.dtype)


def _diff_attn(lam, far, proj3, bias, gain):
    b, s, _ = proj3.shape
    tile = bias.shape[-1]
    nq = s // tile
    qb, kb, vb = OFF_Q // LANES, OFF_K // LANES, OFF_V // LANES
    kern = functools.partial(_attn_kernel, tile=tile, nq=nq)
    return pl.pallas_call(
        kern,
        grid=(b, ATTN_HEADS),
        in_specs=[
            pl.BlockSpec(memory_space=pltpu.SMEM),
            pl.BlockSpec(memory_space=pltpu.SMEM),
            pl.BlockSpec((1, s, LANES), lambda i, h: (i, 0, qb + h)),
            pl.BlockSpec((1, s, LANES), lambda i, h: (i, 0, kb + h)),
            pl.BlockSpec((1, s, LANES), lambda i, h: (i, 0, vb + h)),
            pl.BlockSpec((1, 2, tile, tile), lambda i, h: (h, 0, 0, 0)),
            pl.BlockSpec((1, LANES), lambda i, h: (0, 0)),
        ],
        out_specs=pl.BlockSpec((1, s, LANES), lambda i, h: (i, 0, h)),
        out_shape=jax.ShapeDtypeStruct((b, s, V_COLS), BF16),
        compiler_params=pltpu.CompilerParams(
            dimension_semantics=("parallel", "parallel"), vmem_limit_bytes=VMEM_LIMIT),
        name="diff_attn",
    )(lam, far, proj3, proj3, proj3, bias, gain)


def _conv_silu(main_ref, halo_ref, w_ref, b_ref, scr, first, length):
    halo = halo_ref[0].astype(F32)
    scr[0:SUBLANES, :] = jnp.where(first, jnp.zeros_like(halo), halo)
    scr[SUBLANES:SUBLANES + length, :] = main_ref[0].astype(F32)
    taps = w_ref.shape[0]
    acc = b_ref[...] + w_ref[taps - 1:taps, :] * scr[SUBLANES:SUBLANES + length, :]
    for k in range(taps - 1):
        off = SUBLANES - (taps - 1) + k
        acc = acc + w_ref[k:k + 1, :] * scr[off:off + length, :]
    return _silu(acc)


def _ssd_kernel(xs_ref, xh_ref, bm_ref, bh_ref, cm_ref, ch_ref, z_ref, dt_ref, alog_ref,
                wx_ref, bx_ref, wb_ref, bb_ref, wc_ref, bc_ref, dskip_ref, ng_ref, e_ref,
                y_ref, state_scr, xscr, bscr, cscr, *, length):
    c = pl.program_id(1)
    first = c == 0

    @pl.when(first)
    def _():
        state_scr[...] = jnp.zeros_like(state_scr)

    xs = _conv_silu(xs_ref, xh_ref, wx_ref, bx_ref, xscr, first, length)
    bm = _conv_silu(bm_ref, bh_ref, wb_ref, bb_ref, bscr, first, length)
    cm = _conv_silu(cm_ref, ch_ref, wc_ref, bc_ref, cscr, first, length)

    dt = dt_ref[0]
    a = -jnp.exp(alog_ref[...])
    adt = dt * a
    row = lax.broadcasted_iota(jnp.int32, (length, length), 0)
    col = lax.broadcasted_iota(jnp.int32, (length, length), 1)
    causal = row >= col
    tri = causal.astype(BF16)
    cum = _dot_exact_lhs(tri, _split3(adt))
    cum_t = cum.T
    dt_t = dt.T
    total = cum[length - 1:length, :]
    sel = e_ref[...]
    ecum_x = _dot_exact_rhs(_split3(jnp.exp(cum)), sel)
    edtw_x = _dot_exact_rhs(_split3(dt * jnp.exp(total - cum)), sel)
    etot_x = ecum_x[length - 1:length, :]

    lane = lax.broadcasted_iota(jnp.int32, (length, LANES), 1)
    lo_half = lane < SSM_HEAD_DIM

    for g in range(SSM_GROUPS):
        gs = slice(g * SSM_STATE, (g + 1) * SSM_STATE)
        cs = slice(g * GROUP_COLS, (g + 1) * GROUP_COLS)
        b_g = bm[:, gs].astype(BF16)
        c_g = cm[:, gs].astype(BF16)
        cb = jnp.where(causal, _dot_nt(c_g, b_g), 0.0)
        st = state_scr[g]
        y_g = _dot(c_g, st.astype(BF16)) * ecum_x[:, cs]
        xw = (xs[:, cs] * edtw_x[:, cs]).astype(BF16)
        state_scr[g] = st * etot_x[:, cs] + _dot(bm[:, gs].T.astype(BF16), xw)
        y_parts = []
        for pair in range(SSM_HEADS_PER_GROUP // 2):
            ms = []
            for hh in range(2):
                h = g * SSM_HEADS_PER_GROUP + pair * 2 + hh
                seg = cum[:, h:h + 1] - cum_t[h:h + 1, :]
                dec = jnp.exp(jnp.minimum(seg, 0.0))
                ms.append((cb * dec * dt_t[h:h + 1, :]).astype(BF16))
            x_pair = xs[:, g * GROUP_COLS + pair * LANES: g * GROUP_COLS + (pair + 1) * LANES]
            zero = jnp.zeros_like(x_pair)
            rhs = jnp.concatenate([jnp.where(lo_half, x_pair, zero),
                                   jnp.where(lo_half, zero, x_pair)], axis=0).astype(BF16)
            y_parts.append(_dot(jnp.concatenate(ms, axis=1), rhs))
        y_g = y_g + jnp.concatenate(y_parts, axis=1)
        y_g = y_g + xs[:, cs] * dskip_ref[:, cs]
        y_g = y_g * _silu(z_ref[0, :, cs].astype(F32))
        msq = jnp.mean(y_g * y_g, axis=-1, keepdims=True)
        y_g = y_g * lax.rsqrt(msq + SUBLN_EPS) * ng_ref[:, cs]
        y_ref[0, :, cs] = y_g.astype(y_ref.dtype)


def _ssd(proj3, dt3, alog, wx, bx, wb, bb, wc, bc, dskip_x, ng, sel):
    b, s, _ = proj3.shape
    length = SSD_L
    nc = s // length
    hb = length // SUBLANES
    xs_blk, z_blk = OFF_XS // SSM_D_INNER, OFF_Z // SSM_D_INNER
    b_blk, c_blk = OFF_B // BC_COLS, OFF_C // BC_COLS

    def main(width, blk):
        return pl.BlockSpec((1, length, width), lambda i, c: (i, c, blk))

    def halo(width, blk):
        return pl.BlockSpec((1, SUBLANES, width), lambda i, c: (i, jnp.maximum(c * hb - 1, 0), blk))

    def const(shape):
        return pl.BlockSpec(shape, lambda i, c: (0,) * len(shape))

    kern = functools.partial(_ssd_kernel, length=length)
    return pl.pallas_call(
        kern,
        grid=(b, nc),
        in_specs=[
            main(SSM_D_INNER, xs_blk), halo(SSM_D_INNER, xs_blk),
            main(BC_COLS, b_blk), halo(BC_COLS, b_blk),
            main(BC_COLS, c_blk), halo(BC_COLS, c_blk),
            main(SSM_D_INNER, z_blk),
            pl.BlockSpec((1, length, LANES), lambda i, c: (i, c, 0)),
            const((1, LANES)),
            const((SSM_CONV, SSM_D_INNER)), const((1, SSM_D_INNER)),
            const((SSM_CONV, BC_COLS)), const((1, BC_COLS)),
            const((SSM_CONV, BC_COLS)), const((1, BC_COLS)),
            const((1, SSM_D_INNER)), const((1, SSM_D_INNER)),
            const((LANES, SSM_D_INNER)),
        ],
        out_specs=pl.BlockSpec((1, length, SSM_D_INNER), lambda i, c: (i, c, 0)),
        out_shape=jax.ShapeDtypeStruct((b, s, SSM_D_INNER), BF16),
        scratch_shapes=[
            pltpu.VMEM((SSM_GROUPS, SSM_STATE, GROUP_COLS), F32),
            pltpu.VMEM((length + SUBLANES, SSM_D_INNER), F32),
            pltpu.VMEM((length + SUBLANES, BC_COLS), F32),
            pltpu.VMEM((length + SUBLANES, BC_COLS), F32),
        ],
        compiler_params=pltpu.CompilerParams(
            dimension_semantics=("parallel", "arbitrary"), vmem_limit_bytes=VMEM_LIMIT),
        name="ssd",
    )(proj3, proj3, proj3, proj3, proj3, proj3, proj3, dt3, alog,
      wx, bx, wb, bb, wc, bc, dskip_x, ng, sel)


def _merge_kernel(x_ref, ya_ref, ys_ref, ga_ref, gs_ref, wpa_ref, wps_ref, wo_ref, g_ref,
                  x1_ref, h2_ref):
    pa = _dot(ya_ref[...], wpa_ref[...])
    ps = _dot(ys_ref[...], wps_ref[...])
    ga = 1.0 / (1.0 + jnp.exp(-ga_ref[...].astype(F32)))
    gs = 1.0 / (1.0 + jnp.exp(-gs_ref[...].astype(F32)))
    mixed = (ga * pa + gs * ps).astype(BF16)
    x1 = x_ref[...] + _dot(mixed, wo_ref[...])
    x1_ref[...] = x1
    ms = jnp.mean(x1 * x1, axis=-1, keepdims=True)
    h2_ref[...] = (x1 * lax.rsqrt(ms + RMS_EPS) * g_ref[...]).astype(BF16)


def _merge(x2, y_attn, y_ssm, proj, wpa, wps, wo, g):
    m = x2.shape[0]
    tm = MERGE_TM
    ga_blk = OFF_GATE // D_MODEL

    def rows(width, blk=0):
        return pl.BlockSpec((tm, width), lambda i: (i, blk))

    def const(shape):
        return pl.BlockSpec(shape, lambda i: (0, 0))

    return pl.pallas_call(
        _merge_kernel,
        grid=(m // tm,),
        in_specs=[
            rows(D_MODEL), rows(V_COLS), rows(SSM_D_INNER),
            rows(D_MODEL, ga_blk), rows(D_MODEL, ga_blk + 1),
            const((V_COLS, D_MODEL)), const((SSM_D_INNER, D_MODEL)), const((D_MODEL, D_MODEL)),
            const((1, D_MODEL)),
        ],
        out_specs=[rows(D_MODEL), rows(D_MODEL)],
        out_shape=[jax.ShapeDtypeStruct((m, D_MODEL), F32),
                   jax.ShapeDtypeStruct((m, D_MODEL), BF16)],
        compiler_params=pltpu.CompilerParams(
            dimension_semantics=("parallel",), vmem_limit_bytes=VMEM_LIMIT),
        name="merge",
    )(x2, y_attn, y_ssm, proj, proj, wpa, wps, wo, g)


def _ffn_kernel(x1_ref, h_ref, hh_ref, wg_ref, wv_ref, cwg_ref, cbg_ref, cwv_ref, cbv_ref, wd_ref,
                o_ref, acc_scr, gscr, vscr, *, tm, tiles_per_seq):
    i = pl.program_id(0)
    f = pl.program_id(1)
    first = (i % tiles_per_seq) == 0
    h = h_ref[...]
    halo = hh_ref[...]
    halo = jnp.where(first, jnp.zeros_like(halo), halo)

    def branch(w_ref, cw_ref, cb_ref, scr):
        w = w_ref[...]
        scr[0:SUBLANES, :] = _dot(halo, w)
        scr[SUBLANES:SUBLANES + tm, :] = _dot(h, w)
        taps = cw_ref.shape[0]
        out = cb_ref[...] + cw_ref[taps - 1:taps, :] * scr[SUBLANES:SUBLANES + tm, :]
        for k in range(taps - 1):
            off = SUBLANES - (taps - 1) + k
            out = out + cw_ref[k:k + 1, :] * scr[off:off + tm, :]
        return out

    ug = branch(wg_ref, cwg_ref, cbg_ref, gscr)
    uv = branch(wv_ref, cwv_ref, cbv_ref, vscr)
    act = (_silu(ug) * uv).astype(BF16)
    part = _dot(act, wd_ref[...])

    @pl.when(f == 0)
    def _():
        acc_scr[...] = x1_ref[...] + part

    @pl.when(f != 0)
    def _():
        acc_scr[...] += part

    @pl.when(f == pl.num_programs(1) - 1)
    def _():
        o_ref[...] = acc_scr[...]


def _ffn(x1, h2, w_up, cw, cb, w_down, seq):
    m = x1.shape[0]
    tm = FFN_TM
    tf = FFN_TF
    nf = FFN_CHUNKS
    hb = tm // SUBLANES
    kern = functools.partial(_ffn_kernel, tm=tm, tiles_per_seq=seq // tm)
    return pl.pallas_call(
        kern,
        grid=(m // tm, nf),
        in_specs=[
            pl.BlockSpec((tm, D_MODEL), lambda i, f: (i, 0)),
            pl.BlockSpec((tm, D_MODEL), lambda i, f: (i, 0)),
            pl.BlockSpec((SUBLANES, D_MODEL), lambda i, f: (jnp.maximum(i * hb - 1, 0), 0)),
            pl.BlockSpec((D_MODEL, tf), lambda i, f: (0, f)),
            pl.BlockSpec((D_MODEL, tf), lambda i, f: (0, nf + f)),
            pl.BlockSpec((FFN_CONV, tf), lambda i, f: (0, f)),
            pl.BlockSpec((1, tf), lambda i, f: (0, f)),
            pl.BlockSpec((FFN_CONV, tf), lambda i, f: (0, nf + f)),
            pl.BlockSpec((1, tf), lambda i, f: (0, nf + f)),
            pl.BlockSpec((tf, D_MODEL), lambda i, f: (f, 0)),
        ],
        out_specs=pl.BlockSpec((tm, D_MODEL), lambda i, f: (i, 0)),
        out_shape=jax.ShapeDtypeStruct((m, D_MODEL), F32),
        scratch_shapes=[
            pltpu.VMEM((tm, D_MODEL), F32),
            pltpu.VMEM((tm + SUBLANES, tf), F32),
            pltpu.VMEM((tm + SUBLANES, tf), F32),
        ],
        compiler_params=pltpu.CompilerParams(
            dimension_semantics=("parallel", "arbitrary"), vmem_limit_bytes=VMEM_LIMIT),
        name="ffn",
    )(x1, h2, h2, w_up, w_up, cw, cb, cw, cb, w_down)


def _block_diag_ones(width, chunk):
    r = lax.broadcasted_iota(jnp.int32, (width, width), 0) // chunk
    c = lax.broadcasted_iota(jnp.int32, (width, width), 1) // chunk
    return (r == c).astype(BF16)


def _head_expand(rows, cols, chunk):
    r = lax.broadcasted_iota(jnp.int32, (rows, cols), 0)
    c = lax.broadcasted_iota(jnp.int32, (rows, cols), 1) // chunk
    return (r == c).astype(BF16)


def _layer(x, rel_bias, norm_mix_g, w_in, q_norm_g, k_norm_g, lq1, lk1, lq2, lk2, attn_subln_g,
           conv_ssm_w, conv_ssm_b, dt_bias, a_log, d_skip, ssm_norm_g, w_proj_attn, w_proj_ssm,
           w_out, norm_ffn_g, w_up, conv_ffn_w, conv_ffn_b, w_down):
    b, s, d = x.shape
    m = b * s
    x2 = x.reshape(m, d)

    o_q, o_k, o_v = 0, Q_COLS, 2 * Q_COLS
    o_z = o_v + V_COLS
    o_xs = o_z + SSM_D_INNER
    o_b = o_xs + SSM_D_INNER
    o_c = o_b + BC_COLS
    o_dt = o_c + BC_COLS
    o_gate = o_dt + SSM_HEADS

    def cols(start, width):
        return w_in[:, start:start + width]

    w_main = jnp.concatenate([
        cols(o_xs, SSM_D_INNER), cols(o_z, SSM_D_INNER), cols(o_gate, 2 * D_MODEL),
        cols(o_q, Q_COLS), cols(o_k, Q_COLS), cols(o_v, V_COLS),
        cols(o_b, BC_COLS), cols(o_c, BC_COLS)], axis=1).astype(BF16)
    pad_h = LANES - SSM_HEADS
    w_dt = jnp.pad(cols(o_dt, SSM_HEADS), ((0, 0), (0, pad_h))).astype(BF16)
    dt_b = jnp.pad(dt_bias.astype(F32), (0, pad_h)).reshape(1, LANES)
    reps = Q_COLS // ATTN_HEAD_DIM
    qk_gain = jnp.stack([
        jnp.tile(q_norm_g.astype(F32) * (ATTN_HEAD_DIM ** -0.5 * LOG2E), reps),
        jnp.tile(k_norm_g.astype(F32), reps)]).reshape(2, 1, Q_COLS)
    bd = _block_diag_ones(2 * LANES, ATTN_HEAD_DIM)

    proj, dt = _in_proj(x2, norm_mix_g.astype(F32).reshape(1, d), w_main, w_dt, dt_b, qk_gain, bd)
    proj3 = proj.reshape(b, s, PROJ_COLS)

    vec = lambda v: v.astype(F32).reshape(1, -1)
    bias, far, lam = _bias_build(rel_bias.astype(F32), vec(lq1), vec(lk1), vec(lq2), vec(lk2), ATTN_T)
    y_attn = _diff_attn(lam[0, :1], far[:, 0, 0], proj3, bias,
                        vec(attn_subln_g) * (1.0 - LAM_INIT))

    cw = conv_ssm_w.astype(F32)
    cbias = conv_ssm_b.astype(F32).reshape(1, -1)
    xs_sl = slice(0, SSM_D_INNER)
    b_sl = slice(SSM_D_INNER, SSM_D_INNER + BC_COLS)
    c_sl = slice(SSM_D_INNER + BC_COLS, SSM_D_INNER + 2 * BC_COLS)
    y_ssm = _ssd(
        proj3, dt.reshape(b, s, LANES),
        jnp.pad(a_log.astype(F32), (0, pad_h)).reshape(1, LANES),
        cw[:, xs_sl], cbias[:, xs_sl], cw[:, b_sl], cbias[:, b_sl], cw[:, c_sl], cbias[:, c_sl],
        jnp.repeat(d_skip.astype(F32), SSM_HEAD_DIM).reshape(1, SSM_D_INNER),
        vec(ssm_norm_g), _head_expand(LANES, SSM_D_INNER, SSM_HEAD_DIM))

    x1, h2 = _merge(x2, y_attn.reshape(m, V_COLS), y_ssm.reshape(m, SSM_D_INNER), proj,
                    w_proj_attn.astype(BF16), w_proj_ssm.astype(BF16), w_out.astype(BF16),
                    vec(norm_ffn_g))

    out = _ffn(x1, h2, w_up.astype(BF16), conv_ffn_w.astype(F32),
               conv_ffn_b.astype(F32).reshape(1, -1), w_down.astype(BF16), s)
    return out.reshape(b, s, d)


def kernel(x, rel_bias, norm_mix_g, w_in, q_norm_g, k_norm_g, lambda_q1, lambda_k1, lambda_q2,
           lambda_k2, attn_subln_g, conv_ssm_w, conv_ssm_b, dt_bias, a_log, d_skip, ssm_norm_g,
           w_proj_attn, w_proj_ssm, w_out, norm_ffn_g, w_up, conv_ffn_w, conv_ffn_b, w_down):
    depth = w_in.shape[0]
    assert depth == 1, "lambda_init and the stacked-parameter layout are specialised to depth 1"
    return _layer(x, rel_bias, norm_mix_g[0], w_in[0], q_norm_g[0], k_norm_g[0], lambda_q1[0],
                  lambda_k1[0], lambda_q2[0], lambda_k2[0], attn_subln_g[0], conv_ssm_w[0],
                  conv_ssm_b[0], dt_bias[0], a_log[0], d_skip[0], ssm_norm_g[0], w_proj_attn[0],
                  w_proj_ssm[0], w_out[0], norm_ffn_g[0], w_up[0], conv_ffn_w[0], conv_ffn_b[0],
                  w_down[0])
```

```python
import functools
import math

import jax
import jax.numpy as jnp
from jax import lax
from jax.experimental import pallas as pl
from jax.experimental.pallas import tpu as pltpu

F32 = jnp.float32
BF16 = jnp.bfloat16

D_MODEL = 1024
ATTN_HEADS = 8
ATTN_HEAD_DIM = 64
ATTN_V_DIM = 2 * ATTN_HEAD_DIM
NUM_BUCKETS = 32
MAX_DISTANCE = 128
SSM_D_INNER = 2048
SSM_HEAD_DIM = 64
SSM_HEADS = 32
SSM_GROUPS = 4
SSM_HEADS_PER_GROUP = SSM_HEADS // SSM_GROUPS
SSM_STATE = 128
SSM_CONV = 4
D_FF = 2816
FFN_CONV = 3
RMS_EPS = 1e-6
SUBLN_EPS = 1e-5
LAM_INIT = 0.8 - 0.6 * math.exp(0.0)

Q_COLS = ATTN_HEADS * 2 * ATTN_HEAD_DIM
V_COLS = ATTN_HEADS * ATTN_V_DIM
BC_COLS = SSM_GROUPS * SSM_STATE
GROUP_COLS = SSM_D_INNER // SSM_GROUPS

LANES = 128
SUBLANES = 8
VMEM_LIMIT = 56 * 1024 * 1024

LOG2E = math.log2(math.e)
NEG_BIG = -1e30

OFF_XS = 0
OFF_Z = OFF_XS + SSM_D_INNER
OFF_GATE = OFF_Z + SSM_D_INNER
OFF_Q = OFF_GATE + 2 * D_MODEL
OFF_K = OFF_Q + Q_COLS
OFF_V = OFF_K + Q_COLS
OFF_B = OFF_V + V_COLS
OFF_C = OFF_B + BC_COLS
PROJ_COLS = OFF_C + BC_COLS

IN_TM = 1024
IN_TN = 1024
ATTN_T = 256
SSD_L = 128
MERGE_TM = 512
FFN_TM = 512
FFN_CHUNKS = 2
FFN_TF = D_FF // FFN_CHUNKS


def _split3(v):
    hi = v.astype(BF16)
    r1 = v - hi.astype(F32)
    mid = r1.astype(BF16)
    lo = (r1 - mid.astype(F32)).astype(BF16)
    return hi, mid, lo


def _dot(a, b):
    return jnp.dot(a, b, preferred_element_type=F32)


def _dot_nt(a, b):
    return lax.dot_general(a, b, (((1,), (1,)), ((), ())), preferred_element_type=F32)


def _dot_exact_rhs(parts, sel):
    out = _dot(parts[0], sel)
    for p in parts[1:]:
        out = out + _dot(p, sel)
    return out


def _dot_exact_lhs(sel, parts):
    out = _dot(sel, parts[0])
    for p in parts[1:]:
        out = out + _dot(sel, p)
    return out


def _silu(v):
    return v * (1.0 / (1.0 + jnp.exp(-v)))


def _in_proj_kernel(x_ref, g_ref, w_ref, wdt_ref, dtb_ref, qkg_ref, bd_ref, o_ref, dt_ref, h_scr,
                    *, q_tile, k_tile):
    j = pl.program_id(1)

    @pl.when(j == 0)
    def _():
        x = x_ref[...]
        ms = jnp.mean(x * x, axis=-1, keepdims=True)
        h = x * lax.rsqrt(ms + RMS_EPS) * g_ref[...]
        h_scr[...] = h.astype(BF16)
        t = _dot(h_scr[...], wdt_ref[...]) + dtb_ref[...]
        dt_ref[...] = jnp.maximum(t, 0.0) + jnp.log1p(jnp.exp(-jnp.abs(t)))

    acc = _dot(h_scr[...], w_ref[...])
    is_qk = jnp.logical_or(j == q_tile, j == k_tile)

    @pl.when(is_qk)
    def _():
        bd = bd_ref[...]
        width = bd.shape[0]
        for cblk in range(IN_TN // width):
            a = acc[:, cblk * width:(cblk + 1) * width]
            sq = a * a
            hi = sq.astype(BF16)
            lo = (sq - hi.astype(F32)).astype(BF16)
            ssum = _dot(hi, bd) + _dot(lo, bd)
            y = a * lax.rsqrt(ssum * (1.0 / ATTN_HEAD_DIM) + RMS_EPS)
            y = y * qkg_ref[0, :, cblk * width:(cblk + 1) * width]
            o_ref[:, cblk * width:(cblk + 1) * width] = y.astype(BF16)

    @pl.when(jnp.logical_not(is_qk))
    def _():
        o_ref[...] = acc.astype(BF16)


def _in_proj(x2, g, w_main, w_dt, dt_bias, qk_gain, bd):
    m = x2.shape[0]
    nj = PROJ_COLS // IN_TN
    q_tile, k_tile = OFF_Q // IN_TN, OFF_K // IN_TN
    kern = functools.partial(_in_proj_kernel, q_tile=q_tile, k_tile=k_tile)
    return pl.pallas_call(
        kern,
        grid=(m // IN_TM, nj),
        in_specs=[
            pl.BlockSpec((IN_TM, D_MODEL), lambda i, j: (i, 0)),
            pl.BlockSpec((1, D_MODEL), lambda i, j: (0, 0)),
            pl.BlockSpec((D_MODEL, IN_TN), lambda i, j: (0, j)),
            pl.BlockSpec((D_MODEL, LANES), lambda i, j: (0, 0)),
            pl.BlockSpec((1, LANES), lambda i, j: (0, 0)),
            pl.BlockSpec((1, 1, IN_TN), lambda i, j: (jnp.where(j == k_tile, 1, 0), 0, 0)),
            pl.BlockSpec(bd.shape, lambda i, j: (0, 0)),
        ],
        out_specs=[
            pl.BlockSpec((IN_TM, IN_TN), lambda i, j: (i, j)),
            pl.BlockSpec((IN_TM, LANES), lambda i, j: (i, 0)),
        ],
        out_shape=[
            jax.ShapeDtypeStruct((m, PROJ_COLS), BF16),
            jax.ShapeDtypeStruct((m, LANES), F32),
        ],
        scratch_shapes=[pltpu.VMEM((IN_TM, D_MODEL), BF16)],
        compiler_params=pltpu.CompilerParams(
            dimension_semantics=("parallel", "arbitrary"), vmem_limit_bytes=VMEM_LIMIT),
        name="in_proj",
    )(x2, g, w_main, w_dt, dt_bias, qk_gain, bd)


def _bias_kernel(rel_ref, lq1_ref, lk1_ref, lq2_ref, lk2_ref, bias_ref, far_ref, lam_ref, *, tile):
    h = pl.program_id(0)
    row = lax.broadcasted_iota(jnp.int32, (tile, tile), 0)
    col = lax.broadcasted_iota(jnp.int32, (tile, tile), 1)
    max_exact = NUM_BUCKETS // 2
    for t in range(2):
        d = row - col + t * tile
        n = jnp.maximum(d, 0)
        nf = jnp.maximum(n, 1).astype(F32)
        large = max_exact + (jnp.log(nf / max_exact) / math.log(MAX_DISTANCE / max_exact)
                             * (NUM_BUCKETS - max_exact)).astype(jnp.int32)
        large = jnp.minimum(large, NUM_BUCKETS - 1)
        bucket = jnp.where(n < max_exact, n, large)
        bias = jnp.zeros((tile, tile), F32)
        for bk in range(NUM_BUCKETS):
            bias = jnp.where(bucket == bk, rel_ref[bk, h], bias)
        bias = bias * LOG2E
        if t == 0:
            bias = jnp.where(d >= 0, bias, NEG_BIG)
        bias_ref[0, t] = bias
    far_ref[...] = jnp.full(far_ref.shape, rel_ref[NUM_BUCKETS - 1, h] * LOG2E, F32)
    s1 = jnp.sum(lq1_ref[...] * lk1_ref[...], axis=-1, keepdims=True)
    s2 = jnp.sum(lq2_ref[...] * lk2_ref[...], axis=-1, keepdims=True)
    lam = jnp.exp(s1) - jnp.exp(s2) + LAM_INIT
    lam_ref[...] = jnp.broadcast_to(lam, lam_ref.shape)


def _bias_build(rel_bias, lq1, lk1, lq2, lk2, tile):
    vec = pl.BlockSpec((1, ATTN_HEAD_DIM), lambda h: (0, 0))
    return pl.pallas_call(
        functools.partial(_bias_kernel, tile=tile),
        grid=(ATTN_HEADS,),
        in_specs=[pl.BlockSpec(memory_space=pltpu.SMEM), vec, vec, vec, vec],
        out_specs=[
            pl.BlockSpec((1, 2, tile, tile), lambda h: (h, 0, 0, 0)),
            pl.BlockSpec((1, SUBLANES, LANES), lambda h: (h, 0, 0)),
            pl.BlockSpec((SUBLANES, LANES), lambda h: (0, 0)),
        ],
        out_shape=[
            jax.ShapeDtypeStruct((ATTN_HEADS, 2, tile, tile), F32),
            jax.ShapeDtypeStruct((ATTN_HEADS, SUBLANES, LANES), F32),
            jax.ShapeDtypeStruct((SUBLANES, LANES), F32),
        ],
        compiler_params=pltpu.CompilerParams(dimension_semantics=("arbitrary",)),
        name="bias_build",
    )(rel_bias, lq1, lk1, lq2, lk2)


def _attn_kernel(lam_ref, far_ref, q_ref, k_ref, v_ref, bias_ref, g_ref, o_ref, *, tile, nq):
    h = pl.program_id(1)
    lam = lam_ref[0]
    far = far_ref[h]
    lane = lax.broadcasted_iota(jnp.int32, (tile, LANES), 1)
    lo_half = lane < ATTN_HEAD_DIM
    gain = g_ref[...]

    def scores(qi, which):
        q = q_ref[0, qi * tile:(qi + 1) * tile, :]
        zero = jnp.zeros_like(q)
        q_map = jnp.where(lo_half, q, zero) if which == 0 else jnp.where(lo_half, zero, q)
        return _dot_nt(q_map, k_ref[0, 0:(qi + 1) * tile, :])

    def softmax(qi, s):
        nk = (qi + 1) * tile
        n_far = nk - min(qi + 1, 2) * tile
        if qi == 0:
            near_bias = bias_ref[0, 0]
        else:
            near_bias = jnp.concatenate([bias_ref[0, 1], bias_ref[0, 0]], axis=1)
        s_near = s[:, n_far:] + near_bias
        m = jnp.max(s_near, axis=-1, keepdims=True)
        if n_far:
            s_far = s[:, :n_far]
            m = jnp.maximum(m, jnp.max(s_far, axis=-1, keepdims=True) + far)
            p = jnp.concatenate([jnp.exp2(s_far - (m - far)), jnp.exp2(s_near - m)], axis=1)
        else:
            p = jnp.exp2(s_near - m)
        return p.astype(BF16), jnp.sum(p, axis=-1, keepdims=True)

    def weighted_values(qi, p, l):
        return _dot(p, v_ref[0, 0:(qi + 1) * tile, :]) / l

    units = [(qi, which) for qi in range(nq) for which in range(2)]
    n_units = len(units)
    s_q, p_q, outs = {}, {}, {}
    for step in range(n_units + 2):
        if step < n_units:
            s_q[step] = scores(*units[step])
        if 0 <= step - 1 < n_units:
            p_q[step - 1] = softmax(units[step - 1][0], s_q.pop(step - 1))
        if 0 <= step - 2 < n_units:
            qi, which = units[step - 2]
            outs[which] = weighted_values(qi, *p_q.pop(step - 2))
            if which == 1:
                o = outs[0] - lam * outs[1]
                ms = jnp.mean(o * o, axis=-1, keepdims=True)
                o = o * lax.rsqrt(ms + SUBLN_EPS) * gain
                o_ref[0, qi * tile:(qi + 1) * tile, :] = o.astype(o_ref.dtype)


def _diff_attn(lam, far, proj3, bias, gain):
    b, s, _ = proj3.shape
    tile = bias.shape[-1]
    nq = s // tile
    qb, kb, vb = OFF_Q // LANES, OFF_K // LANES, OFF_V // LANES
    kern = functools.partial(_attn_kernel, tile=tile, nq=nq)
    return pl.pallas_call(
        kern,
        grid=(b, ATTN_HEADS),
        in_specs=[
            pl.BlockSpec(memory_space=pltpu.SMEM),
            pl.BlockSpec(memory_space=pltpu.SMEM),
            pl.BlockSpec((1, s, LANES), lambda i, h: (i, 0, qb + h)),
            pl.BlockSpec((1, s, LANES), lambda i, h: (i, 0, kb + h)),
            pl.BlockSpec((1, s, LANES), lambda i, h: (i, 0, vb + h)),
            pl.BlockSpec((1, 2, tile, tile), lambda i, h: (h, 0, 0, 0)),
            pl.BlockSpec((1, LANES), lambda i, h: (0, 0)),
        ],
        out_specs=pl.BlockSpec((1, s, LANES), lambda i, h: (i, 0, h)),
        out_shape=jax.ShapeDtypeStruct((b, s, V_COLS), BF16),
        compiler_params=pltpu.CompilerParams(
            dimension_semantics=("parallel", "parallel"), vmem_limit_bytes=VMEM_LIMIT),
        name="diff_attn",
    )(lam, far, proj3, proj3, proj3, bias, gain)


def _conv_silu(main_ref, halo_ref, w_ref, b_ref, scr, first, length):
    halo = halo_ref[0].astype(F32)
    scr[0:SUBLANES, :] = jnp.where(first, jnp.zeros_like(halo), halo)
    scr[SUBLANES:SUBLANES + length, :] = main_ref[0].astype(F32)
    taps = w_ref.shape[0]
    acc = b_ref[...] + w_ref[taps - 1:taps, :] * scr[SUBLANES:SUBLANES + length, :]
    for k in range(taps - 1):
        off = SUBLANES - (taps - 1) + k
        acc = acc + w_ref[k:k + 1, :] * scr[off:off + length, :]
    return _silu(acc)


def _ssd_kernel(xs_ref, xh_ref, bm_ref, bh_ref, cm_ref, ch_ref, z_ref, dt_ref, alog_ref,
                wx_ref, bx_ref, wb_ref, bb_ref, wc_ref, bc_ref, dskip_ref, ng_ref, e_ref,
                y_ref, state_scr, xscr, bscr, cscr, *, length):
    c = pl.program_id(1)
    first = c == 0

    @pl.when(first)
    def _():
        state_scr[...] = jnp.zeros_like(state_scr)

    xs = _conv_silu(xs_ref, xh_ref, wx_ref, bx_ref, xscr, first, length)
    bm = _conv_silu(bm_ref, bh_ref, wb_ref, bb_ref, bscr, first, length)
    cm = _conv_silu(cm_ref, ch_ref, wc_ref, bc_ref, cscr, first, length)

    dt = dt_ref[0]
    a = -jnp.exp(alog_ref[...])
    adt = dt * a
    row = lax.broadcasted_iota(jnp.int32, (length, length), 0)
    col = lax.broadcasted_iota(jnp.int32, (length, length), 1)
    causal = row >= col
    tri = causal.astype(BF16)
    cum = _dot_exact_lhs(tri, _split3(adt))
    cum_t = cum.T
    dt_t = dt.T
    total = cum[length - 1:length, :]
    sel = e_ref[...]
    ecum_x = _dot_exact_rhs(_split3(jnp.exp(cum)), sel)
    edtw_x = _dot_exact_rhs(_split3(dt * jnp.exp(total - cum)), sel)
    etot_x = ecum_x[length - 1:length, :]

    lane = lax.broadcasted_iota(jnp.int32, (length, LANES), 1)
    lo_half = lane < SSM_HEAD_DIM

    for g in range(SSM_GROUPS):
        gs = slice(g * SSM_STATE, (g + 1) * SSM_STATE)
        cs = slice(g * GROUP_COLS, (g + 1) * GROUP_COLS)
        b_g = bm[:, gs].astype(BF16)
        c_g = cm[:, gs].astype(BF16)
        cb = jnp.where(causal, _dot_nt(c_g, b_g), 0.0)
        st = state_scr[g]
        y_g = _dot(c_g, st.astype(BF16)) * ecum_x[:, cs]
        xw = (xs[:, cs] * edtw_x[:, cs]).astype(BF16)
        state_scr[g] = st * etot_x[:, cs] + _dot(bm[:, gs].T.astype(BF16), xw)
        y_parts = []
        for pair in range(SSM_HEADS_PER_GROUP // 2):
            ms = []
            for hh in range(2):
                h = g * SSM_HEADS_PER_GROUP + pair * 2 + hh
                seg = cum[:, h:h + 1] - cum_t[h:h + 1, :]
                dec = jnp.exp(jnp.minimum(seg, 0.0))
                ms.append((cb * dec * dt_t[h:h + 1, :]).astype(BF16))
            x_pair = xs[:, g * GROUP_COLS + pair * LANES: g * GROUP_COLS + (pair + 1) * LANES]
            zero = jnp.zeros_like(x_pair)
            rhs = jnp.concatenate([jnp.where(lo_half, x_pair, zero),
                                   jnp.where(lo_half, zero, x_pair)], axis=0).astype(BF16)
            y_parts.append(_dot(jnp.concatenate(ms, axis=1), rhs))
        y_g = y_g + jnp.concatenate(y_parts, axis=1)
        y_g = y_g + xs[:, cs] * dskip_ref[:, cs]
        y_g = y_g * _silu(z_ref[0, :, cs].astype(F32))
        msq = jnp.mean(y_g * y_g, axis=-1, keepdims=True)
        y_g = y_g * lax.rsqrt(msq + SUBLN_EPS) * ng_ref[:, cs]
        y_ref[0, :, cs] = y_g.astype(y_ref.dtype)


def _ssd(proj3, dt3, alog, wx, bx, wb, bb, wc, bc, dskip_x, ng, sel):
    b, s, _ = proj3.shape
    length = SSD_L
    nc = s // length
    hb = length // SUBLANES
    xs_blk, z_blk = OFF_XS // SSM_D_INNER, OFF_Z // SSM_D_INNER
    b_blk, c_blk = OFF_B // BC_COLS, OFF_C // BC_COLS

    def main(width, blk):
        return pl.BlockSpec((1, length, width), lambda i, c: (i, c, blk))

    def halo(width, blk):
        return pl.BlockSpec((1, SUBLANES, width), lambda i, c: (i, jnp.maximum(c * hb - 1, 0), blk))

    def const(shape):
        return pl.BlockSpec(shape, lambda i, c: (0,) * len(shape))

    kern = functools.partial(_ssd_kernel, length=length)
    return pl.pallas_call(
        kern,
        grid=(b, nc),
        in_specs=[
            main(SSM_D_INNER, xs_blk), halo(SSM_D_INNER, xs_blk),
            main(BC_COLS, b_blk), halo(BC_COLS, b_blk),
            main(BC_COLS, c_blk), halo(BC_COLS, c_blk),
            main(SSM_D_INNER, z_blk),
            pl.BlockSpec((1, length, LANES), lambda i, c: (i, c, 0)),
            const((1, LANES)),
            const((SSM_CONV, SSM_D_INNER)), const((1, SSM_D_INNER)),
            const((SSM_CONV, BC_COLS)), const((1, BC_COLS)),
            const((SSM_CONV, BC_COLS)), const((1, BC_COLS)),
            const((1, SSM_D_INNER)), const((1, SSM_D_INNER)),
            const((LANES, SSM_D_INNER)),
        ],
        out_specs=pl.BlockSpec((1, length, SSM_D_INNER), lambda i, c: (i, c, 0)),
        out_shape=jax.ShapeDtypeStruct((b, s, SSM_D_INNER), BF16),
        scratch_shapes=[
            pltpu.VMEM((SSM_GROUPS, SSM_STATE, GROUP_COLS), F32),
            pltpu.VMEM((length + SUBLANES, SSM_D_INNER), F32),
            pltpu.VMEM((length + SUBLANES, BC_COLS), F32),
            pltpu.VMEM((length + SUBLANES, BC_COLS), F32),
        ],
        compiler_params=pltpu.CompilerParams(
            dimension_semantics=("parallel", "arbitrary"), vmem_limit_bytes=VMEM_LIMIT),
        name="ssd",
    )(proj3, proj3, proj3, proj3, proj3, proj3, proj3, dt3, alog,
      wx, bx, wb, bb, wc, bc, dskip_x, ng, sel)


def _merge_kernel(x_ref, ya_ref, ys_ref, ga_ref, gs_ref, wpa_ref, wps_ref, wo_ref, g_ref,
                  x1_ref, h2_ref):
    pa = _dot(ya_ref[...], wpa_ref[...])
    ps = _dot(ys_ref[...], wps_ref[...])
    ga = 1.0 / (1.0 + jnp.exp(-ga_ref[...].astype(F32)))
    gs = 1.0 / (1.0 + jnp.exp(-gs_ref[...].astype(F32)))
    mixed = (ga * pa + gs * ps).astype(BF16)
    x1 = x_ref[...] + _dot(mixed, wo_ref[...])
    x1_ref[...] = x1
    ms = jnp.mean(x1 * x1, axis=-1, keepdims=True)
    h2_ref[...] = (x1 * lax.rsqrt(ms + RMS_EPS) * g_ref[...]).astype(BF16)


def _merge(x2, y_attn, y_ssm, proj, wpa, wps, wo, g):
    m = x2.shape[0]
    tm = MERGE_TM
    ga_blk = OFF_GATE // D_MODEL

    def rows(width, blk=0):
        return pl.BlockSpec((tm, width), lambda i: (i, blk))

    def const(shape):
        return pl.BlockSpec(shape, lambda i: (0, 0))

    return pl.pallas_call(
        _merge_kernel,
        grid=(m // tm,),
        in_specs=[
            rows(D_MODEL), rows(V_COLS), rows(SSM_D_INNER),
            rows(D_MODEL, ga_blk), rows(D_MODEL, ga_blk + 1),
            const((V_COLS, D_MODEL)), const((SSM_D_INNER, D_MODEL)), const((D_MODEL, D_MODEL)),
            const((1, D_MODEL)),
        ],
        out_specs=[rows(D_MODEL), rows(D_MODEL)],
        out_shape=[jax.ShapeDtypeStruct((m, D_MODEL), F32),
                   jax.ShapeDtypeStruct((m, D_MODEL), BF16)],
        compiler_params=pltpu.CompilerParams(
            dimension_semantics=("parallel",), vmem_limit_bytes=VMEM_LIMIT),
        name="merge",
    )(x2, y_attn, y_ssm, proj, proj, wpa, wps, wo, g)


def _ffn_kernel(x1_ref, h_ref, hh_ref, wg_ref, wv_ref, cwg_ref, cbg_ref, cwv_ref, cbv_ref, wd_ref,
                o_ref, acc_scr, gscr, vscr, *, tm, tiles_per_seq):
    i = pl.program_id(0)
    f = pl.program_id(1)
    first = (i % tiles_per_seq) == 0
    h = h_ref[...]
    halo = hh_ref[...]
    halo = jnp.where(first, jnp.zeros_like(halo), halo)

    def branch(w_ref, cw_ref, cb_ref, scr):
        w = w_ref[...]
        scr[0:SUBLANES, :] = _dot(halo, w)
        scr[SUBLANES:SUBLANES + tm, :] = _dot(h, w)
        taps = cw_ref.shape[0]
        out = cb_ref[...] + cw_ref[taps - 1:taps, :] * scr[SUBLANES:SUBLANES + tm, :]
        for k in range(taps - 1):
            off = SUBLANES - (taps - 1) + k
            out = out + cw_ref[k:k + 1, :] * scr[off:off + tm, :]
        return out

    ug = branch(wg_ref, cwg_ref, cbg_ref, gscr)
    uv = branch(wv_ref, cwv_ref, cbv_ref, vscr)
    act = (_silu(ug) * uv).astype(BF16)
    part = _dot(act, wd_ref[...])

    @pl.when(f == 0)
    def _():
        acc_scr[...] = x1_ref[...] + part

    @pl.when(f != 0)
    def _():
        acc_scr[...] += part

    @pl.when(f == pl.num_programs(1) - 1)
    def _():
        o_ref[...] = acc_scr[...]


def _ffn(x1, h2, w_up, cw, cb, w_down, seq):
    m = x1.shape[0]
    tm = FFN_TM
    tf = FFN_TF
    nf = FFN_CHUNKS
    hb = tm // SUBLANES
    kern = functools.partial(_ffn_kernel, tm=tm, tiles_per_seq=seq // tm)
    return pl.pallas_call(
        kern,
        grid=(m // tm, nf),
        in_specs=[
            pl.BlockSpec((tm, D_MODEL), lambda i, f: (i, 0)),
            pl.BlockSpec((tm, D_MODEL), lambda i, f: (i, 0)),
            pl.BlockSpec((SUBLANES, D_MODEL), lambda i, f: (jnp.maximum(i * hb - 1, 0), 0)),
            pl.BlockSpec((D_MODEL, tf), lambda i, f: (0, f)),
            pl.BlockSpec((D_MODEL, tf), lambda i, f: (0, nf + f)),
            pl.BlockSpec((FFN_CONV, tf), lambda i, f: (0, f)),
            pl.BlockSpec((1, tf), lambda i, f: (0, f)),
            pl.BlockSpec((FFN_CONV, tf), lambda i, f: (0, nf + f)),
            pl.BlockSpec((1, tf), lambda i, f: (0, nf + f)),
            pl.BlockSpec((tf, D_MODEL), lambda i, f: (f, 0)),
        ],
        out_specs=pl.BlockSpec((tm, D_MODEL), lambda i, f: (i, 0)),
        out_shape=jax.ShapeDtypeStruct((m, D_MODEL), F32),
        scratch_shapes=[
            pltpu.VMEM((tm, D_MODEL), F32),
            pltpu.VMEM((tm + SUBLANES, tf), F32),
            pltpu.VMEM((tm + SUBLANES, tf), F32),
        ],
        compiler_params=pltpu.CompilerParams(
            dimension_semantics=("parallel", "arbitrary"), vmem_limit_bytes=VMEM_LIMIT),
        name="ffn",
    )(x1, h2, h2, w_up, w_up, cw, cb, cw, cb, w_down)


def _block_diag_ones(width, chunk):
    r = lax.broadcasted_iota(jnp.int32, (width, width), 0) // chunk
    c = lax.broadcasted_iota(jnp.int32, (width, width), 1) // chunk
    return (r == c).astype(BF16)


def _head_expand(rows, cols, chunk):
    r = lax.broadcasted_iota(jnp.int32, (rows, cols), 0)
    c = lax.broadcasted_iota(jnp.int32, (rows, cols), 1) // chunk
    return (r == c).astype(BF16)


def _layer(x, rel_bias, norm_mix_g, w_in, q_norm_g, k_norm_g, lq1, lk1, lq2, lk2, attn_subln_g,
           conv_ssm_w, conv_ssm_b, dt_bias, a_log, d_skip, ssm_norm_g, w_proj_attn, w_proj_ssm,
           w_out, norm_ffn_g, w_up, conv_ffn_w, conv_ffn_b, w_down):
    b, s, d = x.shape
    m = b * s
    x2 = x.reshape(m, d)

    o_q, o_k, o_v = 0, Q_COLS, 2 * Q_COLS
    o_z = o_v + V_COLS
    o_xs = o_z + SSM_D_INNER
    o_b = o_xs + SSM_D_INNER
    o_c = o_b + BC_COLS
    o_dt = o_c + BC_COLS
    o_gate = o_dt + SSM_HEADS

    def cols(start, width):
        return w_in[:, start:start + width]

    w_main = jnp.concatenate([
        cols(o_xs, SSM_D_INNER), cols(o_z, SSM_D_INNER), cols(o_gate, 2 * D_MODEL),
        cols(o_q, Q_COLS), cols(o_k, Q_COLS), cols(o_v, V_COLS),
        cols(o_b, BC_COLS), cols(o_c, BC_COLS)], axis=1).astype(BF16)
    pad_h = LANES - SSM_HEADS
    w_dt = jnp.pad(cols(o_dt, SSM_HEADS), ((0, 0), (0, pad_h))).astype(BF16)
    dt_b = jnp.pad(dt_bias.astype(F32), (0, pad_h)).reshape(1, LANES)
    reps = Q_COLS // ATTN_HEAD_DIM
    qk_gain = jnp.stack([
        jnp.tile(q_norm_g.astype(F32) * (ATTN_HEAD_DIM ** -0.5 * LOG2E), reps),
        jnp.tile(k_norm_g.astype(F32), reps)]).reshape(2, 1, Q_COLS)
    bd = _block_diag_ones(2 * LANES, ATTN_HEAD_DIM)

    proj, dt = _in_proj(x2, norm_mix_g.astype(F32).reshape(1, d), w_main, w_dt, dt_b, qk_gain, bd)
    proj3 = proj.reshape(b, s, PROJ_COLS)

    vec = lambda v: v.astype(F32).reshape(1, -1)
    bias, far, lam = _bias_build(rel_bias.astype(F32), vec(lq1), vec(lk1), vec(lq2), vec(lk2), ATTN_T)
    y_attn = _diff_attn(lam[0, :1], far[:, 0, 0], proj3, bias,
                        vec(attn_subln_g) * (1.0 - LAM_INIT))

    cw = conv_ssm_w.astype(F32)
    cbias = conv_ssm_b.astype(F32).reshape(1, -1)
    xs_sl = slice(0, SSM_D_INNER)
    b_sl = slice(SSM_D_INNER, SSM_D_INNER + BC_COLS)
    c_sl = slice(SSM_D_INNER + BC_COLS, SSM_D_INNER + 2 * BC_COLS)
    y_ssm = _ssd(
        proj3, dt.reshape(b, s, LANES),
        jnp.pad(a_log.astype(F32), (0, pad_h)).reshape(1, LANES),
        cw[:, xs_sl], cbias[:, xs_sl], cw[:, b_sl], cbias[:, b_sl], cw[:, c_sl], cbias[:, c_sl],
        jnp.repeat(d_skip.astype(F32), SSM_HEAD_DIM).reshape(1, SSM_D_INNER),
        vec(ssm_norm_g), _head_expand(LANES, SSM_D_INNER, SSM_HEAD_DIM))

    x1, h2 = _merge(x2, y_attn.reshape(m, V_COLS), y_ssm.reshape(m, SSM_D_INNER), proj,
                    w_proj_attn.astype(BF16), w_proj_ssm.astype(BF16), w_out.astype(BF16),
                    vec(norm_ffn_g))

    out = _ffn(x1, h2, w_up.astype(BF16), conv_ffn_w.astype(F32),
               conv_ffn_b.astype(F32).reshape(1, -1), w_down.astype(BF16), s)
    return out.reshape(b, s, d)


def kernel(x, rel_bias, norm_mix_g, w_in, q_norm_g, k_norm_g, lambda_q1, lambda_k1, lambda_q2,
           lambda_k2, attn_subln_g, conv_ssm_w, conv_ssm_b, dt_bias, a_log, d_skip, ssm_norm_g,
           w_proj_attn, w_proj_ssm, w_out, norm_ffn_g, w_up, conv_ffn_w, conv_ffn_b, w_down):
    depth = w_in.shape[0]
    assert depth == 1, "lambda_init and the stacked-parameter layout are specialised to depth 1"
    return _layer(x, rel_bias, norm_mix_g[0], w_in[0], q_norm_g[0], k_norm_g[0], lambda_q1[0],
                  lambda_k1[0], lambda_q2[0], lambda_k2[0], attn_subln_g[0], conv_ssm_w[0],
                  conv_ssm_b[0], dt_bias[0], a_log[0], d_skip[0], ssm_norm_g[0], w_proj_attn[0],
                  w_proj_ssm[0], w_out[0], norm_ffn_g[0], w_up[0], conv_ffn_w[0], conv_ffn_b[0],
                  w_down[0])
```

```python
import functools
import math

import jax
import jax.numpy as jnp
from jax import lax
from jax.experimental import pallas as pl
from jax.experimental.pallas import tpu as pltpu

F32 = jnp.float32
BF16 = jnp.bfloat16

D_MODEL = 1024
ATTN_HEADS = 8
ATTN_HEAD_DIM = 64
ATTN_V_DIM = 2 * ATTN_HEAD_DIM
NUM_BUCKETS = 32
MAX_DISTANCE = 128
SSM_D_INNER = 2048
SSM_HEAD_DIM = 64
SSM_HEADS = 32
SSM_GROUPS = 4
SSM_HEADS_PER_GROUP = SSM_HEADS // SSM_GROUPS
SSM_STATE = 128
SSM_CONV = 4
D_FF = 2816
FFN_CONV = 3
RMS_EPS = 1e-6
SUBLN_EPS = 1e-5
LAM_INIT = 0.8 - 0.6 * math.exp(0.0)

Q_COLS = ATTN_HEADS * 2 * ATTN_HEAD_DIM
V_COLS = ATTN_HEADS * ATTN_V_DIM
BC_COLS = SSM_GROUPS * SSM_STATE
GROUP_COLS = SSM_D_INNER // SSM_GROUPS

LANES = 128
SUBLANES = 8
BF16_SUBLANES = 16
MXU_WIDTH = 256
VMEM_LIMIT = 56 * 1024 * 1024

LOG2E = math.log2(math.e)
NEG_BIG = -1e30

OFF_XS = 0
OFF_Z = OFF_XS + SSM_D_INNER
OFF_GATE = OFF_Z + SSM_D_INNER
OFF_Q = OFF_GATE + 2 * D_MODEL
OFF_K = OFF_Q + Q_COLS
OFF_V = OFF_K + Q_COLS
OFF_B = OFF_V + V_COLS
OFF_C = OFF_B + BC_COLS
PROJ_COLS = OFF_C + BC_COLS

IN_TM = 1024
IN_TN = 1024
ATTN_T = 256
SSD_L = 128
MERGE_TM = 512
FFN_TM = 512
FFN_SPLITS = ((0, 512), (512, 512), (1024, 512), (1536, 512), (2048, 512), (2560, 256))
FFN_SCR = max(w for _, w in FFN_SPLITS)


def _split(v, parts):
    out = []
    for _ in range(parts - 1):
        hi = v.astype(BF16)
        out.append(hi)
        v = v - hi.astype(F32)
    out.append(v.astype(BF16))
    return out


def _dot(a, b):
    return jnp.dot(a, b, preferred_element_type=F32)


def _dot_nt(a, b):
    return lax.dot_general(a, b, (((1,), (1,)), ((), ())), preferred_element_type=F32)


def _dot_exact_rhs(parts, sel):
    out = _dot(parts[0], sel)
    for p in parts[1:]:
        out = out + _dot(p, sel)
    return out


def _dot_exact_lhs(sel, parts):
    out = _dot(sel, parts[0])
    for p in parts[1:]:
        out = out + _dot(sel, p)
    return out


def _silu(v):
    h = 0.5 * v
    return h + h * jnp.tanh(h)


def _in_proj_kernel(x_ref, g_ref, w_ref, wdt_ref, dtb_ref, qkg_ref, bd_ref, o_ref, dt_ref, h_scr,
                    *, q_tile, k_tile):
    j = pl.program_id(1)

    @pl.when(j == 0)
    def _():
        x = x_ref[...]
        ms = jnp.mean(x * x, axis=-1, keepdims=True)
        h = x * lax.rsqrt(ms + RMS_EPS) * g_ref[...]
        h_scr[...] = h.astype(BF16)
        t = _dot(h_scr[...], wdt_ref[...]) + dtb_ref[...]
        dt_ref[...] = jnp.maximum(t, 0.0) + jnp.log1p(jnp.exp(-jnp.abs(t)))

    acc = _dot(h_scr[...], w_ref[...])
    is_qk = jnp.logical_or(j == q_tile, j == k_tile)

    @pl.when(is_qk)
    def _():
        bd = bd_ref[...]
        width = bd.shape[0]
        for cblk in range(IN_TN // width):
            a = acc[:, cblk * width:(cblk + 1) * width]
            sq = a * a
            hi = sq.astype(BF16)
            lo = (sq - hi.astype(F32)).astype(BF16)
            ssum = _dot(hi, bd) + _dot(lo, bd)
            y = a * lax.rsqrt(ssum * (1.0 / ATTN_HEAD_DIM) + RMS_EPS)
            y = y * qkg_ref[0, :, cblk * width:(cblk + 1) * width]
            o_ref[:, cblk * width:(cblk + 1) * width] = y.astype(BF16)

    @pl.when(jnp.logical_not(is_qk))
    def _():
        o_ref[...] = acc.astype(BF16)


def _in_proj(x2, g, w_main, w_dt, dt_bias, qk_gain, bd):
    m = x2.shape[0]
    nj = PROJ_COLS // IN_TN
    q_tile, k_tile = OFF_Q // IN_TN, OFF_K // IN_TN
    kern = functools.partial(_in_proj_kernel, q_tile=q_tile, k_tile=k_tile)
    return pl.pallas_call(
        kern,
        grid=(m // IN_TM, nj),
        in_specs=[
            pl.BlockSpec((IN_TM, D_MODEL), lambda i, j: (i, 0)),
            pl.BlockSpec((1, D_MODEL), lambda i, j: (0, 0)),
            pl.BlockSpec((D_MODEL, IN_TN), lambda i, j: (0, j)),
            pl.BlockSpec((D_MODEL, LANES), lambda i, j: (0, 0)),
            pl.BlockSpec((1, LANES), lambda i, j: (0, 0)),
            pl.BlockSpec((1, 1, IN_TN), lambda i, j: (jnp.where(j == k_tile, 1, 0), 0, 0)),
            pl.BlockSpec(bd.shape, lambda i, j: (0, 0)),
        ],
        out_specs=[
            pl.BlockSpec((IN_TM, IN_TN), lambda i, j: (i, j)),
            pl.BlockSpec((IN_TM, LANES), lambda i, j: (i, 0)),
        ],
        out_shape=[
            jax.ShapeDtypeStruct((m, PROJ_COLS), BF16),
            jax.ShapeDtypeStruct((m, LANES), F32),
        ],
        scratch_shapes=[pltpu.VMEM((IN_TM, D_MODEL), BF16)],
        compiler_params=pltpu.CompilerParams(
            dimension_semantics=("parallel", "arbitrary"), vmem_limit_bytes=VMEM_LIMIT),
        name="in_proj",
    )(x2, g, w_main, w_dt, dt_bias, qk_gain, bd)


def _bias_kernel(rel_ref, lq1_ref, lk1_ref, lq2_ref, lk2_ref, bias_ref, far_ref, lam_ref, *, tile):
    h = pl.program_id(0)
    row = lax.broadcasted_iota(jnp.int32, (tile, tile), 0)
    col = lax.broadcasted_iota(jnp.int32, (tile, tile), 1)
    max_exact = NUM_BUCKETS // 2
    for t in range(2):
        d = row - col + t * tile
        n = jnp.maximum(d, 0)
        nf = jnp.maximum(n, 1).astype(F32)
        large = max_exact + (jnp.log(nf / max_exact) / math.log(MAX_DISTANCE / max_exact)
                             * (NUM_BUCKETS - max_exact)).astype(jnp.int32)
        large = jnp.minimum(large, NUM_BUCKETS - 1)
        bucket = jnp.where(n < max_exact, n, large)
        bias = jnp.zeros((tile, tile), F32)
        for bk in range(NUM_BUCKETS):
            bias = jnp.where(bucket == bk, rel_ref[bk, h], bias)
        bias = bias * LOG2E
        if t == 0:
            bias = jnp.where(d >= 0, bias, NEG_BIG)
        bias_ref[0, t] = bias
    far_ref[...] = jnp.full(far_ref.shape, rel_ref[NUM_BUCKETS - 1, h] * LOG2E, F32)
    s1 = jnp.sum(lq1_ref[...] * lk1_ref[...], axis=-1, keepdims=True)
    s2 = jnp.sum(lq2_ref[...] * lk2_ref[...], axis=-1, keepdims=True)
    lam = jnp.exp(s1) - jnp.exp(s2) + LAM_INIT
    lam_ref[...] = jnp.broadcast_to(lam, lam_ref.shape)


def _bias_build(rel_bias, lq1, lk1, lq2, lk2, tile):
    vec = pl.BlockSpec((1, ATTN_HEAD_DIM), lambda h: (0, 0))
    return pl.pallas_call(
        functools.partial(_bias_kernel, tile=tile),
        grid=(ATTN_HEADS,),
        in_specs=[pl.BlockSpec(memory_space=pltpu.SMEM), vec, vec, vec, vec],
        out_specs=[
            pl.BlockSpec((1, 2, tile, tile), lambda h: (h, 0, 0, 0)),
            pl.BlockSpec((1, SUBLANES, LANES), lambda h: (h, 0, 0)),
            pl.BlockSpec((SUBLANES, LANES), lambda h: (0, 0)),
        ],
        out_shape=[
            jax.ShapeDtypeStruct((ATTN_HEADS, 2, tile, tile), F32),
            jax.ShapeDtypeStruct((ATTN_HEADS, SUBLANES, LANES), F32),
            jax.ShapeDtypeStruct((SUBLANES, LANES), F32),
        ],
        compiler_params=pltpu.CompilerParams(dimension_semantics=("arbitrary",)),
        name="bias_build",
    )(rel_bias, lq1, lk1, lq2, lk2)


def _attn_kernel(lam_ref, far_ref, q_ref, k_ref, v_ref, bias_ref, g_ref, o_ref, *, tile, nq):
    h = pl.program_id(1)
    lam = lam_ref[0]
    far = far_ref[h]
    lane = lax.broadcasted_iota(jnp.int32, (tile, LANES), 1)
    lo_half = lane < ATTN_HEAD_DIM
    gain = g_ref[...]

    def scores(qi, which):
        q = q_ref[0, qi * tile:(qi + 1) * tile, :]
        zero = jnp.zeros_like(q)
        q_map = jnp.where(lo_half, q, zero) if which == 0 else jnp.where(lo_half, zero, q)
        return _dot_nt(q_map, k_ref[0, 0:(qi + 1) * tile, :])

    def softmax(qi, s):
        nk = (qi + 1) * tile
        n_far = nk - min(qi + 1, 2) * tile
        if qi == 0:
            near_bias = bias_ref[0, 0]
        else:
            near_bias = jnp.concatenate([bias_ref[0, 1], bias_ref[0, 0]], axis=1)
        s_near = s[:, n_far:] + near_bias
        m = jnp.max(s_near, axis=-1, keepdims=True)
        if n_far:
            s_far = s[:, :n_far]
            m = jnp.maximum(m, jnp.max(s_far, axis=-1, keepdims=True) + far)
            p = jnp.concatenate([jnp.exp2(s_far - (m - far)), jnp.exp2(s_near - m)], axis=1)
        else:
            p = jnp.exp2(s_near - m)
        return p.astype(BF16), jnp.sum(p, axis=-1, keepdims=True)

    def weighted_values(qi, p, l):
        return _dot(p, v_ref[0, 0:(qi + 1) * tile, :]) / l

    units = [(qi, which) for qi in range(nq) for which in range(2)]
    n_units = len(units)
    s_q, p_q, outs = {}, {}, {}
    for step in range(n_units + 2):
        if step < n_units:
            s_q[step] = scores(*units[step])
        if 0 <= step - 1 < n_units:
            p_q[step - 1] = softmax(units[step - 1][0], s_q.pop(step - 1))
        if 0 <= step - 2 < n_units:
            qi, which = units[step - 2]
            outs[which] = weighted_values(qi, *p_q.pop(step - 2))
            if which == 1:
                o = outs[0] - lam * outs[1]
                ms = jnp.mean(o * o, axis=-1, keepdims=True)
                o = o * lax.rsqrt(ms + SUBLN_EPS) * gain
                o_ref[0, qi * tile:(qi + 1) * tile, :] = o.astype(o_ref.dtype)


def _diff_attn(lam, far, proj3, bias, gain):
    b, s, _ = proj3.shape
    tile = bias.shape[-1]
    nq = s // tile
    qb, kb, vb = OFF_Q // LANES, OFF_K // LANES, OFF_V // LANES
    kern = functools.partial(_attn_kernel, tile=tile, nq=nq)
    return pl.pallas_call(
        kern,
        grid=(b, ATTN_HEADS),
        in_specs=[
            pl.BlockSpec(memory_space=pltpu.SMEM),
            pl.BlockSpec(memory_space=pltpu.SMEM),
            pl.BlockSpec((1, s, LANES), lambda i, h: (i, 0, qb + h)),
            pl.BlockSpec((1, s, LANES), lambda i, h: (i, 0, kb + h)),
            pl.BlockSpec((1, s, LANES), lambda i, h: (i, 0, vb + h)),
            pl.BlockSpec((1, 2, tile, tile), lambda i, h: (h, 0, 0, 0)),
            pl.BlockSpec((1, LANES), lambda i, h: (0, 0)),
        ],
        out_specs=pl.BlockSpec((1, s, LANES), lambda i, h: (i, 0, h)),
        out_shape=jax.ShapeDtypeStruct((b, s, V_COLS), BF16),
        compiler_params=pltpu.CompilerParams(
            dimension_semantics=("parallel", "parallel"), vmem_limit_bytes=VMEM_LIMIT),
        name="diff_attn",
    )(lam, far, proj3, proj3, proj3, bias, gain)


def _conv_silu(main_ref, halo_ref, w_ref, b_ref, shift, first, length, cols=slice(None)):
    halo = halo_ref[0, :, cols]
    x_ext = jnp.concatenate([jnp.where(first, jnp.zeros_like(halo), halo), main_ref[0, :, cols]],
                            axis=0)
    shifted = _dot(shift, x_ext)
    acc = b_ref[:, cols] + w_ref[0:1, cols] * shifted[0:length]
    for k in range(1, w_ref.shape[0]):
        acc = acc + w_ref[k:k + 1, cols] * shifted[k * length:(k + 1) * length]
    return _silu(acc)


def _ssd_kernel(xs_ref, xh_ref, bm_ref, bh_ref, cm_ref, ch_ref, z_ref, dt_ref, alog_ref,
                wx_ref, bx_ref, wb_ref, bb_ref, wc_ref, bc_ref, dskip_ref, ng_ref, e_ref,
                y_ref, state_scr, *, length):
    c = pl.program_id(1)
    first = c == 0

    @pl.when(first)
    def _():
        state_scr[...] = jnp.zeros_like(state_scr)

    r = lax.broadcasted_iota(jnp.int32, (SSM_CONV * length, BF16_SUBLANES + length), 0)
    j = lax.broadcasted_iota(jnp.int32, (SSM_CONV * length, BF16_SUBLANES + length), 1)
    tap = r // length
    shift = (j == BF16_SUBLANES + (r - tap * length) - (SSM_CONV - 1 - tap)).astype(BF16)
    bm = _conv_silu(bm_ref, bh_ref, wb_ref, bb_ref, shift, first, length).astype(BF16)
    cm = _conv_silu(cm_ref, ch_ref, wc_ref, bc_ref, shift, first, length).astype(BF16)

    def conv_x(g):
        cols = slice(g * GROUP_COLS, (g + 1) * GROUP_COLS)
        return _conv_silu(xs_ref, xh_ref, wx_ref, bx_ref, shift, first, length, cols)

    xs_all = [conv_x(g) for g in range(SSM_GROUPS)]

    dt = dt_ref[0]
    a = -jnp.exp(alog_ref[...])
    adt = dt * a
    row = lax.broadcasted_iota(jnp.int32, (length, length), 0)
    col = lax.broadcasted_iota(jnp.int32, (length, length), 1)
    causal = row >= col
    tri = causal.astype(BF16)
    cum = _dot_exact_lhs(tri, _split(adt, 3))
    cum_t = cum.T
    dt_t = dt.T
    total = cum[length - 1:length, :]
    sel = e_ref[...]
    ecum_x = _dot_exact_rhs(_split(jnp.exp(cum), 2), sel)
    edtw_x = _dot_exact_rhs(_split(dt * jnp.exp(total - cum), 2), sel)
    etot_x = ecum_x[length - 1:length, :]

    lane = lax.broadcasted_iota(jnp.int32, (length, LANES), 1)
    lo_half = lane < SSM_HEAD_DIM

    for g in range(SSM_GROUPS):
        gs = slice(g * SSM_STATE, (g + 1) * SSM_STATE)
        cs = slice(g * GROUP_COLS, (g + 1) * GROUP_COLS)
        xs_f = xs_all[g]
        b_g = bm[:, gs]
        c_g = cm[:, gs]
        xs_g = xs_f.astype(BF16)
        cb = jnp.where(causal, _dot_nt(c_g, b_g), 0.0)
        st = state_scr[g]
        y_g = _dot(c_g, st.astype(BF16)) * ecum_x[:, cs]
        xw = (xs_f * edtw_x[:, cs]).astype(BF16)
        b_t = b_g.astype(F32).T.astype(BF16)
        state_scr[g] = st * etot_x[:, cs] + _dot(b_t, xw)
        y_parts = []
        for pair in range(SSM_HEADS_PER_GROUP // 2):
            ms = []
            for hh in range(2):
                h = g * SSM_HEADS_PER_GROUP + pair * 2 + hh
                seg = cum[:, h:h + 1] - cum_t[h:h + 1, :]
                dec = jnp.exp(jnp.minimum(seg, 0.0))
                ms.append((cb * dec * dt_t[h:h + 1, :]).astype(BF16))
            x_pair = xs_g[:, pair * LANES:(pair + 1) * LANES]
            zero = jnp.zeros_like(x_pair)
            rhs = jnp.concatenate([jnp.where(lo_half, x_pair, zero),
                                   jnp.where(lo_half, zero, x_pair)], axis=0)
            y_parts.append(_dot(jnp.concatenate(ms, axis=1), rhs))
        y_g = y_g + jnp.concatenate(y_parts, axis=1)
        y_g = y_g + xs_f * dskip_ref[:, cs]
        y_g = y_g * _silu(z_ref[0, :, cs].astype(F32))
        msq = jnp.mean(y_g * y_g, axis=-1, keepdims=True)
        y_g = y_g * lax.rsqrt(msq + SUBLN_EPS) * ng_ref[:, cs]
        y_ref[0, :, cs] = y_g.astype(y_ref.dtype)


def _ssd(proj3, dt3, alog, wx, bx, wb, bb, wc, bc, dskip_x, ng, sel):
    b, s, _ = proj3.shape
    length = SSD_L
    nc = s // length
    hb = length // BF16_SUBLANES
    xs_blk, z_blk = OFF_XS // SSM_D_INNER, OFF_Z // SSM_D_INNER
    b_blk, c_blk = OFF_B // BC_COLS, OFF_C // BC_COLS

    def main(width, blk):
        return pl.BlockSpec((1, length, width), lambda i, c: (i, c, blk))

    def halo(width, blk):
        return pl.BlockSpec((1, BF16_SUBLANES, width),
                            lambda i, c: (i, jnp.maximum(c * hb - 1, 0), blk))

    def const(shape):
        return pl.BlockSpec(shape, lambda i, c: (0,) * len(shape))

    kern = functools.partial(_ssd_kernel, length=length)
    return pl.pallas_call(
        kern,
        grid=(b, nc),
        in_specs=[
            main(SSM_D_INNER, xs_blk), halo(SSM_D_INNER, xs_blk),
            main(BC_COLS, b_blk), halo(BC_COLS, b_blk),
            main(BC_COLS, c_blk), halo(BC_COLS, c_blk),
            main(SSM_D_INNER, z_blk),
            pl.BlockSpec((1, length, LANES), lambda i, c: (i, c, 0)),
            const((1, LANES)),
            const((SSM_CONV, SSM_D_INNER)), const((1, SSM_D_INNER)),
            const((SSM_CONV, BC_COLS)), const((1, BC_COLS)),
            const((SSM_CONV, BC_COLS)), const((1, BC_COLS)),
            const((1, SSM_D_INNER)), const((1, SSM_D_INNER)),
            const((LANES, SSM_D_INNER)),
        ],
        out_specs=pl.BlockSpec((1, length, SSM_D_INNER), lambda i, c: (i, c, 0)),
        out_shape=jax.ShapeDtypeStruct((b, s, SSM_D_INNER), BF16),
        scratch_shapes=[pltpu.VMEM((SSM_GROUPS, SSM_STATE, GROUP_COLS), F32)],
        compiler_params=pltpu.CompilerParams(
            dimension_semantics=("parallel", "arbitrary"), vmem_limit_bytes=VMEM_LIMIT),
        name="ssd",
    )(proj3, proj3, proj3, proj3, proj3, proj3, proj3, dt3, alog,
      wx, bx, wb, bb, wc, bc, dskip_x, ng, sel)


def _merge_kernel(x_ref, ya_ref, ys_ref, ga_ref, gs_ref, wpa_ref, wps_ref, wo_ref, g_ref,
                  x1_ref, h2_ref):
    pa = _dot(ya_ref[...], wpa_ref[...])
    ps = _dot(ys_ref[...], wps_ref[...])
    ga = 1.0 / (1.0 + jnp.exp(-ga_ref[...].astype(F32)))
    gs = 1.0 / (1.0 + jnp.exp(-gs_ref[...].astype(F32)))
    mixed = (ga * pa + gs * ps).astype(BF16)
    x1 = x_ref[...] + _dot(mixed, wo_ref[...])
    x1_ref[...] = x1
    ms = jnp.mean(x1 * x1, axis=-1, keepdims=True)
    h2_ref[...] = (x1 * lax.rsqrt(ms + RMS_EPS) * g_ref[...]).astype(BF16)


def _merge(x2, y_attn, y_ssm, proj, wpa, wps, wo, g):
    m = x2.shape[0]
    tm = MERGE_TM
    ga_blk = OFF_GATE // D_MODEL

    def rows(width, blk=0):
        return pl.BlockSpec((tm, width), lambda i: (i, blk))

    def const(shape):
        return pl.BlockSpec(shape, lambda i: (0, 0))

    return pl.pallas_call(
        _merge_kernel,
        grid=(m // tm,),
        in_specs=[
            rows(D_MODEL), rows(V_COLS), rows(SSM_D_INNER),
            rows(D_MODEL, ga_blk), rows(D_MODEL, ga_blk + 1),
            const((V_COLS, D_MODEL)), const((SSM_D_INNER, D_MODEL)), const((D_MODEL, D_MODEL)),
            const((1, D_MODEL)),
        ],
        out_specs=[rows(D_MODEL), rows(D_MODEL)],
        out_shape=[jax.ShapeDtypeStruct((m, D_MODEL), F32),
                   jax.ShapeDtypeStruct((m, D_MODEL), BF16)],
        compiler_params=pltpu.CompilerParams(
            dimension_semantics=("parallel",), vmem_limit_bytes=VMEM_LIMIT),
        name="merge",
    )(x2, y_attn, y_ssm, proj, proj, wpa, wps, wo, g)


def _ffn_kernel(x1_ref, h_ref, hh_ref, wup_ref, cw_ref, cb_ref, wd_ref, o_ref, gscr, vscr,
                *, tm, tiles_per_seq):
    i = pl.program_id(0)
    first = (i % tiles_per_seq) == 0
    h = h_ref[...]
    halo = hh_ref[...]
    halo = jnp.where(first, jnp.zeros_like(halo), halo)
    taps = cw_ref.shape[0]

    def up_project(idx):
        start, width = FFN_SPLITS[idx]
        for col0, scr in ((start, gscr), (D_FF + start, vscr)):
            w = wup_ref[:, col0:col0 + width]
            scr[idx % 2, 0:SUBLANES, 0:width] = _dot(halo, w)
            scr[idx % 2, SUBLANES:SUBLANES + tm, 0:width] = _dot(h, w)

    def conv(idx, col0, scr):
        width = FFN_SPLITS[idx][1]
        cols = slice(col0, col0 + width)
        out = cb_ref[:, cols] + cw_ref[taps - 1:taps, cols] * scr[idx % 2, SUBLANES:SUBLANES + tm, 0:width]
        for k in range(taps - 1):
            off = SUBLANES - (taps - 1) + k
            out = out + cw_ref[k:k + 1, cols] * scr[idx % 2, off:off + tm, 0:width]
        return out

    acc = x1_ref[...]
    up_project(0)
    for idx, (start, width) in enumerate(FFN_SPLITS):
        if idx + 1 < len(FFN_SPLITS):
            up_project(idx + 1)
        act = (_silu(conv(idx, start, gscr)) * conv(idx, D_FF + start, vscr)).astype(BF16)
        acc = acc + _dot(act, wd_ref[start:start + width, :])

    o_ref[...] = acc


def _ffn(x1, h2, w_up, cw, cb, w_down, seq):
    m = x1.shape[0]
    tm = FFN_TM
    hb = tm // SUBLANES
    kern = functools.partial(_ffn_kernel, tm=tm, tiles_per_seq=seq // tm)

    def resident(shape):
        return pl.BlockSpec(shape, lambda i: (0, 0), pipeline_mode=pl.Buffered(1))

    return pl.pallas_call(
        kern,
        grid=(m // tm,),
        in_specs=[
            pl.BlockSpec((tm, D_MODEL), lambda i: (i, 0)),
            pl.BlockSpec((tm, D_MODEL), lambda i: (i, 0)),
            pl.BlockSpec((SUBLANES, D_MODEL), lambda i: (jnp.maximum(i * hb - 1, 0), 0)),
            resident((D_MODEL, 2 * D_FF)),
            resident((FFN_CONV, 2 * D_FF)),
            resident((1, 2 * D_FF)),
            resident((D_FF, D_MODEL)),
        ],
        out_specs=pl.BlockSpec((tm, D_MODEL), lambda i: (i, 0)),
        out_shape=jax.ShapeDtypeStruct((m, D_MODEL), F32),
        scratch_shapes=[
            pltpu.VMEM((2, tm + SUBLANES, FFN_SCR), F32),
            pltpu.VMEM((2, tm + SUBLANES, FFN_SCR), F32),
        ],
        compiler_params=pltpu.CompilerParams(
            dimension_semantics=("parallel",), vmem_limit_bytes=VMEM_LIMIT),
        name="ffn",
    )(x1, h2, h2, w_up, cw, cb, w_down)


def _block_diag_ones(width, chunk):
    r = lax.broadcasted_iota(jnp.int32, (width, width), 0) // chunk
    c = lax.broadcasted_iota(jnp.int32, (width, width), 1) // chunk
    return (r == c).astype(BF16)


def _head_expand(rows, cols, chunk):
    r = lax.broadcasted_iota(jnp.int32, (rows, cols), 0)
    c = lax.broadcasted_iota(jnp.int32, (rows, cols), 1) // chunk
    return (r == c).astype(BF16)


def _layer(x, rel_bias, norm_mix_g, w_in, q_norm_g, k_norm_g, lq1, lk1, lq2, lk2, attn_subln_g,
           conv_ssm_w, conv_ssm_b, dt_bias, a_log, d_skip, ssm_norm_g, w_proj_attn, w_proj_ssm,
           w_out, norm_ffn_g, w_up, conv_ffn_w, conv_ffn_b, w_down):
    b, s, d = x.shape
    m = b * s
    x2 = x.reshape(m, d)

    o_q, o_k, o_v = 0, Q_COLS, 2 * Q_COLS
    o_z = o_v + V_COLS
    o_xs = o_z + SSM_D_INNER
    o_b = o_xs + SSM_D_INNER
    o_c = o_b + BC_COLS
    o_dt = o_c + BC_COLS
    o_gate = o_dt + SSM_HEADS

    def cols(start, width):
        return w_in[:, start:start + width]

    w_main = jnp.concatenate([
        cols(o_xs, SSM_D_INNER), cols(o_z, SSM_D_INNER), cols(o_gate, 2 * D_MODEL),
        cols(o_q, Q_COLS), cols(o_k, Q_COLS), cols(o_v, V_COLS),
        cols(o_b, BC_COLS), cols(o_c, BC_COLS)], axis=1).astype(BF16)
    pad_h = LANES - SSM_HEADS
    w_dt = jnp.pad(cols(o_dt, SSM_HEADS), ((0, 0), (0, pad_h))).astype(BF16)
    dt_b = jnp.pad(dt_bias.astype(F32), (0, pad_h)).reshape(1, LANES)
    reps = Q_COLS // ATTN_HEAD_DIM
    qk_gain = jnp.stack([
        jnp.tile(q_norm_g.astype(F32) * (ATTN_HEAD_DIM ** -0.5 * LOG2E), reps),
        jnp.tile(k_norm_g.astype(F32), reps)]).reshape(2, 1, Q_COLS)
    bd = _block_diag_ones(MXU_WIDTH, ATTN_HEAD_DIM)

    proj, dt = _in_proj(x2, norm_mix_g.astype(F32).reshape(1, d), w_main, w_dt, dt_b, qk_gain, bd)
    proj3 = proj.reshape(b, s, PROJ_COLS)

    vec = lambda v: v.astype(F32).reshape(1, -1)
    bias, far, lam = _bias_build(rel_bias.astype(F32), vec(lq1), vec(lk1), vec(lq2), vec(lk2), ATTN_T)
    y_attn = _diff_attn(lam[0, :1], far[:, 0, 0], proj3, bias,
                        vec(attn_subln_g) * (1.0 - LAM_INIT))

    cw = conv_ssm_w.astype(F32)
    cbias = conv_ssm_b.astype(F32).reshape(1, -1)
    xs_sl = slice(0, SSM_D_INNER)
    b_sl = slice(SSM_D_INNER, SSM_D_INNER + BC_COLS)
    c_sl = slice(SSM_D_INNER + BC_COLS, SSM_D_INNER + 2 * BC_COLS)
    y_ssm = _ssd(
        proj3, dt.reshape(b, s, LANES),
        jnp.pad(a_log.astype(F32), (0, pad_h)).reshape(1, LANES),
        cw[:, xs_sl], cbias[:, xs_sl], cw[:, b_sl], cbias[:, b_sl], cw[:, c_sl], cbias[:, c_sl],
        jnp.repeat(d_skip.astype(F32), SSM_HEAD_DIM).reshape(1, SSM_D_INNER),
        vec(ssm_norm_g), _head_expand(LANES, SSM_D_INNER, SSM_HEAD_DIM))

    x1, h2 = _merge(x2, y_attn.reshape(m, V_COLS), y_ssm.reshape(m, SSM_D_INNER), proj,
                    w_proj_attn.astype(BF16), w_proj_ssm.astype(BF16), w_out.astype(BF16),
                    vec(norm_ffn_g))

    out = _ffn(x1, h2, w_up.astype(BF16), conv_ffn_w.astype(F32),
               conv_ffn_b.astype(F32).reshape(1, -1), w_down.astype(BF16), s)
    return out.reshape(b, s, d)


def kernel(x, rel_bias, norm_mix_g, w_in, q_norm_g, k_norm_g, lambda_q1, lambda_k1, lambda_q2,
           lambda_k2, attn_subln_g, conv_ssm_w, conv_ssm_b, dt_bias, a_log, d_skip, ssm_norm_g,
           w_proj_attn, w_proj_ssm, w_out, norm_ffn_g, w_up, conv_ffn_w, conv_ffn_b, w_down):
    depth = w_in.shape[0]
    assert depth == 1, "lambda_init and the stacked-parameter layout are specialised to depth 1"
    return _layer(x, rel_bias, norm_mix_g[0], w_in[0], q_norm_g[0], k_norm_g[0], lambda_q1[0],
                  lambda_k1[0], lambda_q2[0], lambda_k2[0], attn_subln_g[0], conv_ssm_w[0],
                  conv_ssm_b[0], dt_bias[0], a_log[0], d_skip[0], ssm_norm_g[0], w_proj_attn[0],
                  w_proj_ssm[0], w_out[0], norm_ffn_g[0], w_up[0], conv_ffn_w[0], conv_ffn_b[0],
                  w_down[0])
```

```python
import functools
import math

import jax
import jax.numpy as jnp
from jax import lax
from jax.experimental import pallas as pl
from jax.experimental.pallas import tpu as pltpu

F32 = jnp.float32
BF16 = jnp.bfloat16

D_MODEL = 1024
ATTN_HEADS = 8
ATTN_HEAD_DIM = 64
ATTN_V_DIM = 2 * ATTN_HEAD_DIM
NUM_BUCKETS = 32
MAX_DISTANCE = 128
SSM_D_INNER = 2048
SSM_HEAD_DIM = 64
SSM_HEADS = 32
SSM_GROUPS = 4
SSM_HEADS_PER_GROUP = SSM_HEADS // SSM_GROUPS
SSM_STATE = 128
SSM_CONV = 4
D_FF = 2816
FFN_CONV = 3
RMS_EPS = 1e-6
SUBLN_EPS = 1e-5
LAM_INIT = 0.8 - 0.6 * math.exp(0.0)

Q_COLS = ATTN_HEADS * 2 * ATTN_HEAD_DIM
V_COLS = ATTN_HEADS * ATTN_V_DIM
BC_COLS = SSM_GROUPS * SSM_STATE
GROUP_COLS = SSM_D_INNER // SSM_GROUPS

LANES = 128
SUBLANES = 8
BF16_SUBLANES = 16
MXU_WIDTH = 256
VMEM_LIMIT = 56 * 1024 * 1024

LOG2E = math.log2(math.e)
NEG_BIG = -1e30

OFF_XS = 0
OFF_Z = OFF_XS + SSM_D_INNER
OFF_GATE = OFF_Z + SSM_D_INNER
OFF_Q = OFF_GATE + 2 * D_MODEL
OFF_K = OFF_Q + Q_COLS
OFF_V = OFF_K + Q_COLS
OFF_B = OFF_V + V_COLS
OFF_C = OFF_B + BC_COLS
PROJ_COLS = OFF_C + BC_COLS

IN_TM = 512
IN_TN = 1024
IN_HALVES = 2
ATTN_T = 256
SSD_L = 128
MERGE_TM = 512
FFN_TM = 512
FFN_SPLITS = ((0, 512), (512, 512), (1024, 512), (1536, 512), (2048, 512), (2560, 256))
FFN_SCR = max(w for _, w in FFN_SPLITS)


def _split(v, parts):
    out = []
    for _ in range(parts - 1):
        hi = v.astype(BF16)
        out.append(hi)
        v = v - hi.astype(F32)
    out.append(v.astype(BF16))
    return out


def _dot(a, b):
    return jnp.dot(a, b, preferred_element_type=F32)


def _dot_nt(a, b):
    return lax.dot_general(a, b, (((1,), (1,)), ((), ())), preferred_element_type=F32)


def _dot_exact_rhs(parts, sel):
    out = _dot(parts[0], sel)
    for p in parts[1:]:
        out = out + _dot(p, sel)
    return out


def _dot_exact_lhs(sel, parts):
    out = _dot(sel, parts[0])
    for p in parts[1:]:
        out = out + _dot(sel, p)
    return out


def _silu(v):
    h = 0.5 * v
    return h + h * jnp.tanh(h)


def _in_proj_kernel(x_ref, g_ref, w_ref, wdt_ref, dtb_ref, qkg_ref, bd_ref, o_ref, dt_ref, h_scr,
                    *, tile_kinds):
    half = pl.program_id(1)

    @pl.when(half == 0)
    def _():
        x = x_ref[...]
        ms = jnp.mean(x * x, axis=-1, keepdims=True)
        h_scr[...] = (x * lax.rsqrt(ms + RMS_EPS) * g_ref[...]).astype(BF16)
        t = _dot(h_scr[...], wdt_ref[...]) + dtb_ref[...]
        dt_ref[...] = jnp.maximum(t, 0.0) + jnp.log1p(jnp.exp(-jnp.abs(t)))

    h = h_scr[...]

    def tile_cols(t):
        return slice(t * IN_TN, (t + 1) * IN_TN)

    def store_plain(t, acc):
        o_ref[:, tile_cols(t)] = acc.astype(BF16)

    def store_head_normed(t, acc, gain_row):
        bd = bd_ref[...]
        width = bd.shape[0]
        for cblk in range(IN_TN // width):
            cols = slice(cblk * width, (cblk + 1) * width)
            a = acc[:, cols]
            ssum = _dot((a * a).astype(BF16), bd)
            y = a * lax.rsqrt(ssum * (1.0 / ATTN_HEAD_DIM) + RMS_EPS)
            y = y * qkg_ref[gain_row, :, cols]
            o_ref[:, t * IN_TN + cblk * width:t * IN_TN + (cblk + 1) * width] = y.astype(BF16)

    for hv, kinds in enumerate(tile_kinds):
        @pl.when(half == hv)
        def _(kinds=kinds):
            acc_next = _dot(h, w_ref[:, tile_cols(0)])
            for t, kind in enumerate(kinds):
                acc = acc_next
                if t + 1 < len(kinds):
                    acc_next = _dot(h, w_ref[:, tile_cols(t + 1)])
                if kind == "plain":
                    store_plain(t, acc)
                else:
                    store_head_normed(t, acc, 0 if kind == "q" else 1)


def _in_proj(x2, g, w_main, w_dt, dt_bias, qk_gain, bd):
    m = x2.shape[0]
    half_cols = PROJ_COLS // IN_HALVES
    tiles = half_cols // IN_TN
    kinds = {OFF_Q // IN_TN: "q", OFF_K // IN_TN: "k"}
    tile_kinds = tuple(tuple(kinds.get(hv * tiles + t, "plain") for t in range(tiles))
                       for hv in range(IN_HALVES))
    kern = functools.partial(_in_proj_kernel, tile_kinds=tile_kinds)
    return pl.pallas_call(
        kern,
        grid=(m // IN_TM, IN_HALVES),
        in_specs=[
            pl.BlockSpec((IN_TM, D_MODEL), lambda i, hv: (i, 0)),
            pl.BlockSpec((1, D_MODEL), lambda i, hv: (0, 0)),
            pl.BlockSpec((D_MODEL, half_cols), lambda i, hv: (0, hv)),
            pl.BlockSpec((D_MODEL, LANES), lambda i, hv: (0, 0)),
            pl.BlockSpec((1, LANES), lambda i, hv: (0, 0)),
            pl.BlockSpec(qk_gain.shape, lambda i, hv: (0, 0, 0)),
            pl.BlockSpec(bd.shape, lambda i, hv: (0, 0)),
        ],
        out_specs=[
            pl.BlockSpec((IN_TM, half_cols), lambda i, hv: (i, hv)),
            pl.BlockSpec((IN_TM, LANES), lambda i, hv: (i, 0)),
        ],
        out_shape=[
            jax.ShapeDtypeStruct((m, PROJ_COLS), BF16),
            jax.ShapeDtypeStruct((m, LANES), F32),
        ],
        scratch_shapes=[pltpu.VMEM((IN_TM, D_MODEL), BF16)],
        compiler_params=pltpu.CompilerParams(
            dimension_semantics=("parallel", "arbitrary"), vmem_limit_bytes=VMEM_LIMIT),
        name="in_proj",
    )(x2, g, w_main, w_dt, dt_bias, qk_gain, bd)


def _bias_kernel(rel_ref, lq1_ref, lk1_ref, lq2_ref, lk2_ref, bias_ref, far_ref, lam_ref, *, tile):
    h = pl.program_id(0)
    row = lax.broadcasted_iota(jnp.int32, (tile, tile), 0)
    col = lax.broadcasted_iota(jnp.int32, (tile, tile), 1)
    max_exact = NUM_BUCKETS // 2
    for t in range(2):
        d = row - col + t * tile
        n = jnp.maximum(d, 0)
        nf = jnp.maximum(n, 1).astype(F32)
        large = max_exact + (jnp.log(nf / max_exact) / math.log(MAX_DISTANCE / max_exact)
                             * (NUM_BUCKETS - max_exact)).astype(jnp.int32)
        large = jnp.minimum(large, NUM_BUCKETS - 1)
        bucket = jnp.where(n < max_exact, n, large)
        bias = jnp.zeros((tile, tile), F32)
        for bk in range(NUM_BUCKETS):
            bias = jnp.where(bucket == bk, rel_ref[bk, h], bias)
        bias = bias * LOG2E
        if t == 0:
            bias = jnp.where(d >= 0, bias, NEG_BIG)
        bias_ref[0, t] = bias
    far_ref[...] = jnp.full(far_ref.shape, rel_ref[NUM_BUCKETS - 1, h] * LOG2E, F32)
    s1 = jnp.sum(lq1_ref[...] * lk1_ref[...], axis=-1, keepdims=True)
    s2 = jnp.sum(lq2_ref[...] * lk2_ref[...], axis=-1, keepdims=True)
    lam = jnp.exp(s1) - jnp.exp(s2) + LAM_INIT
    lam_ref[...] = jnp.broadcast_to(lam, lam_ref.shape)


def _bias_build(rel_bias, lq1, lk1, lq2, lk2, tile):
    vec = pl.BlockSpec((1, ATTN_HEAD_DIM), lambda h: (0, 0))
    return pl.pallas_call(
        functools.partial(_bias_kernel, tile=tile),
        grid=(ATTN_HEADS,),
        in_specs=[pl.BlockSpec(memory_space=pltpu.SMEM), vec, vec, vec, vec],
        out_specs=[
            pl.BlockSpec((1, 2, tile, tile), lambda h: (h, 0, 0, 0)),
            pl.BlockSpec((1, SUBLANES, LANES), lambda h: (h, 0, 0)),
            pl.BlockSpec((SUBLANES, LANES), lambda h: (0, 0)),
        ],
        out_shape=[
            jax.ShapeDtypeStruct((ATTN_HEADS, 2, tile, tile), F32),
            jax.ShapeDtypeStruct((ATTN_HEADS, SUBLANES, LANES), F32),
            jax.ShapeDtypeStruct((SUBLANES, LANES), F32),
        ],
        compiler_params=pltpu.CompilerParams(dimension_semantics=("arbitrary",)),
        name="bias_build",
    )(rel_bias, lq1, lk1, lq2, lk2)


def _attn_kernel(lam_ref, far_ref, q_ref, k_ref, v_ref, bias_ref, g_ref, o_ref, *, tile, nq):
    h = pl.program_id(1)
    lam = lam_ref[0]
    far = far_ref[h]
    lane = lax.broadcasted_iota(jnp.int32, (tile, LANES), 1)
    lo_half = lane < ATTN_HEAD_DIM
    gain = g_ref[...]

    def scores(qi, which):
        q = q_ref[0, qi * tile:(qi + 1) * tile, :]
        zero = jnp.zeros_like(q)
        q_map = jnp.where(lo_half, q, zero) if which == 0 else jnp.where(lo_half, zero, q)
        return _dot_nt(q_map, k_ref[0, 0:(qi + 1) * tile, :])

    def softmax(qi, s):
        nk = (qi + 1) * tile
        n_far = nk - min(qi + 1, 2) * tile
        if qi == 0:
            near_bias = bias_ref[0, 0]
        else:
            near_bias = jnp.concatenate([bias_ref[0, 1], bias_ref[0, 0]], axis=1)
        s_near = s[:, n_far:] + near_bias
        m = jnp.max(s_near, axis=-1, keepdims=True)
        if n_far:
            s_far = s[:, :n_far]
            m = jnp.maximum(m, jnp.max(s_far, axis=-1, keepdims=True) + far)
            p = jnp.concatenate([jnp.exp2(s_far - (m - far)), jnp.exp2(s_near - m)], axis=1)
        else:
            p = jnp.exp2(s_near - m)
        return p.astype(BF16), jnp.sum(p, axis=-1, keepdims=True)

    def weighted_values(qi, p, l):
        return _dot(p, v_ref[0, 0:(qi + 1) * tile, :]) / l

    units = [(qi, which) for qi in range(nq) for which in range(2)]
    n_units = len(units)
    s_q, p_q, outs = {}, {}, {}
    for step in range(n_units + 2):
        if step < n_units:
            s_q[step] = scores(*units[step])
        if 0 <= step - 1 < n_units:
            p_q[step - 1] = softmax(units[step - 1][0], s_q.pop(step - 1))
        if 0 <= step - 2 < n_units:
            qi, which = units[step - 2]
            outs[which] = weighted_values(qi, *p_q.pop(step - 2))
            if which == 1:
                o = outs[0] - lam * outs[1]
                ms = jnp.mean(o * o, axis=-1, keepdims=True)
                o = o * lax.rsqrt(ms + SUBLN_EPS) * gain
                o_ref[0, qi * tile:(qi + 1) * tile, :] = o.astype(o_ref.dtype)


def _diff_attn(lam, far, proj3, bias, gain):
    b, s, _ = proj3.shape
    tile = bias.shape[-1]
    nq = s // tile
    qb, kb, vb = OFF_Q // LANES, OFF_K // LANES, OFF_V // LANES
    kern = functools.partial(_attn_kernel, tile=tile, nq=nq)
    return pl.pallas_call(
        kern,
        grid=(b, ATTN_HEADS),
        in_specs=[
            pl.BlockSpec(memory_space=pltpu.SMEM),
            pl.BlockSpec(memory_space=pltpu.SMEM),
            pl.BlockSpec((1, s, LANES), lambda i, h: (i, 0, qb + h)),
            pl.BlockSpec((1, s, LANES), lambda i, h: (i, 0, kb + h)),
            pl.BlockSpec((1, s, LANES), lambda i, h: (i, 0, vb + h)),
            pl.BlockSpec((1, 2, tile, tile), lambda i, h: (h, 0, 0, 0)),
            pl.BlockSpec((1, LANES), lambda i, h: (0, 0)),
        ],
        out_specs=pl.BlockSpec((1, s, LANES), lambda i, h: (i, 0, h)),
        out_shape=jax.ShapeDtypeStruct((b, s, V_COLS), BF16),
        compiler_params=pltpu.CompilerParams(
            dimension_semantics=("parallel", "parallel"), vmem_limit_bytes=VMEM_LIMIT),
        name="diff_attn",
    )(lam, far, proj3, proj3, proj3, bias, gain)


def _conv_silu(main_ref, halo_ref, w_ref, b_ref, shift, first, length, cols=slice(None)):
    halo = halo_ref[0, :, cols]
    x_ext = jnp.concatenate([jnp.where(first, jnp.zeros_like(halo), halo), main_ref[0, :, cols]],
                            axis=0)
    shifted = _dot(shift, x_ext)
    acc = b_ref[:, cols] + w_ref[0:1, cols] * shifted[0:length]
    for k in range(1, w_ref.shape[0]):
        acc = acc + w_ref[k:k + 1, cols] * shifted[k * length:(k + 1) * length]
    return _silu(acc)


def _ssd_kernel(xs_ref, xh_ref, bm_ref, bh_ref, cm_ref, ch_ref, z_ref, dt_ref, alog_ref,
                wx_ref, bx_ref, wb_ref, bb_ref, wc_ref, bc_ref, dskip_ref, ng_ref, e_ref,
                y_ref, state_scr, *, length):
    c = pl.program_id(1)
    first = c == 0

    @pl.when(first)
    def _():
        state_scr[...] = jnp.zeros_like(state_scr)

    r = lax.broadcasted_iota(jnp.int32, (SSM_CONV * length, BF16_SUBLANES + length), 0)
    j = lax.broadcasted_iota(jnp.int32, (SSM_CONV * length, BF16_SUBLANES + length), 1)
    tap = r // length
    shift = (j == BF16_SUBLANES + (r - tap * length) - (SSM_CONV - 1 - tap)).astype(BF16)
    bm = _conv_silu(bm_ref, bh_ref, wb_ref, bb_ref, shift, first, length).astype(BF16)
    cm = _conv_silu(cm_ref, ch_ref, wc_ref, bc_ref, shift, first, length).astype(BF16)

    def conv_x(g):
        cols = slice(g * GROUP_COLS, (g + 1) * GROUP_COLS)
        return _conv_silu(xs_ref, xh_ref, wx_ref, bx_ref, shift, first, length, cols)

    xs_all = [conv_x(g) for g in range(SSM_GROUPS)]

    dt = dt_ref[0]
    a = -jnp.exp(alog_ref[...])
    adt = dt * a
    row = lax.broadcasted_iota(jnp.int32, (length, length), 0)
    col = lax.broadcasted_iota(jnp.int32, (length, length), 1)
    causal = row >= col
    tri = causal.astype(BF16)
    cum = _dot_exact_lhs(tri, _split(adt, 3))
    cum_t = cum.T
    dt_t = dt.T
    total = cum[length - 1:length, :]
    sel = e_ref[...]
    ecum_x = _dot_exact_rhs(_split(jnp.exp(cum), 2), sel)
    edtw_x = _dot_exact_rhs(_split(dt * jnp.exp(total - cum), 2), sel)
    etot_x = ecum_x[length - 1:length, :]

    lane = lax.broadcasted_iota(jnp.int32, (length, LANES), 1)
    lo_half = lane < SSM_HEAD_DIM

    for g in range(SSM_GROUPS):
        gs = slice(g * SSM_STATE, (g + 1) * SSM_STATE)
        cs = slice(g * GROUP_COLS, (g + 1) * GROUP_COLS)
        xs_f = xs_all[g]
        b_g = bm[:, gs]
        c_g = cm[:, gs]
        xs_g = xs_f.astype(BF16)
        cb = jnp.where(causal, _dot_nt(c_g, b_g), 0.0)
        st = state_scr[g]
        y_g = _dot(c_g, st.astype(BF16)) * ecum_x[:, cs]
        xw = (xs_f * edtw_x[:, cs]).astype(BF16)
        b_t = b_g.astype(F32).T.astype(BF16)
        state_scr[g] = st * etot_x[:, cs] + _dot(b_t, xw)
        y_parts = []
        for pair in range(SSM_HEADS_PER_GROUP // 2):
            ms = []
            for hh in range(2):
                h = g * SSM_HEADS_PER_GROUP + pair * 2 + hh
                seg = cum[:, h:h + 1] - cum_t[h:h + 1, :]
                dec = jnp.exp(jnp.minimum(seg, 0.0))
                ms.append((cb * dec * dt_t[h:h + 1, :]).astype(BF16))
            x_pair = xs_g[:, pair * LANES:(pair + 1) * LANES]
            zero = jnp.zeros_like(x_pair)
            rhs = jnp.concatenate([jnp.where(lo_half, x_pair, zero),
                                   jnp.where(lo_half, zero, x_pair)], axis=0)
            y_parts.append(_dot(jnp.concatenate(ms, axis=1), rhs))
        y_g = y_g + jnp.concatenate(y_parts, axis=1)
        y_g = y_g + xs_f * dskip_ref[:, cs]
        y_g = y_g * _silu(z_ref[0, :, cs].astype(F32))
        msq = jnp.mean(y_g * y_g, axis=-1, keepdims=True)
        y_g = y_g * lax.rsqrt(msq + SUBLN_EPS) * ng_ref[:, cs]
        y_ref[0, :, cs] = y_g.astype(y_ref.dtype)


def _ssd(proj3, dt3, alog, wx, bx, wb, bb, wc, bc, dskip_x, ng, sel):
    b, s, _ = proj3.shape
    length = SSD_L
    nc = s // length
    hb = length // BF16_SUBLANES
    xs_blk, z_blk = OFF_XS // SSM_D_INNER, OFF_Z // SSM_D_INNER
    b_blk, c_blk = OFF_B // BC_COLS, OFF_C // BC_COLS

    def main(width, blk):
        return pl.BlockSpec((1, length, width), lambda i, c: (i, c, blk))

    def halo(width, blk):
        return pl.BlockSpec((1, BF16_SUBLANES, width),
                            lambda i, c: (i, jnp.maximum(c * hb - 1, 0), blk))

    def const(shape):
        return pl.BlockSpec(shape, lambda i, c: (0,) * len(shape))

    kern = functools.partial(_ssd_kernel, length=length)
    return pl.pallas_call(
        kern,
        grid=(b, nc),
        in_specs=[
            main(SSM_D_INNER, xs_blk), halo(SSM_D_INNER, xs_blk),
            main(BC_COLS, b_blk), halo(BC_COLS, b_blk),
            main(BC_COLS, c_blk), halo(BC_COLS, c_blk),
            main(SSM_D_INNER, z_blk),
            pl.BlockSpec((1, length, LANES), lambda i, c: (i, c, 0)),
            const((1, LANES)),
            const((SSM_CONV, SSM_D_INNER)), const((1, SSM_D_INNER)),
            const((SSM_CONV, BC_COLS)), const((1, BC_COLS)),
            const((SSM_CONV, BC_COLS)), const((1, BC_COLS)),
            const((1, SSM_D_INNER)), const((1, SSM_D_INNER)),
            const((LANES, SSM_D_INNER)),
        ],
        out_specs=pl.BlockSpec((1, length, SSM_D_INNER), lambda i, c: (i, c, 0)),
        out_shape=jax.ShapeDtypeStruct((b, s, SSM_D_INNER), BF16),
        scratch_shapes=[pltpu.VMEM((SSM_GROUPS, SSM_STATE, GROUP_COLS), F32)],
        compiler_params=pltpu.CompilerParams(
            dimension_semantics=("parallel", "arbitrary"), vmem_limit_bytes=VMEM_LIMIT),
        name="ssd",
    )(proj3, proj3, proj3, proj3, proj3, proj3, proj3, dt3, alog,
      wx, bx, wb, bb, wc, bc, dskip_x, ng, sel)


def _merge_kernel(x_ref, ya_ref, ys_ref, ga_ref, gs_ref, wpa_ref, wps_ref, wo_ref, g_ref,
                  x1_ref, h2_ref):
    pa = _dot(ya_ref[...], wpa_ref[...])
    ps = _dot(ys_ref[...], wps_ref[...])
    ga = 1.0 / (1.0 + jnp.exp(-ga_ref[...].astype(F32)))
    gs = 1.0 / (1.0 + jnp.exp(-gs_ref[...].astype(F32)))
    mixed = (ga * pa + gs * ps).astype(BF16)
    x1 = x_ref[...] + _dot(mixed, wo_ref[...])
    x1_ref[...] = x1
    ms = jnp.mean(x1 * x1, axis=-1, keepdims=True)
    h2_ref[...] = (x1 * lax.rsqrt(ms + RMS_EPS) * g_ref[...]).astype(BF16)


def _merge(x2, y_attn, y_ssm, proj, wpa, wps, wo, g):
    m = x2.shape[0]
    tm = MERGE_TM
    ga_blk = OFF_GATE // D_MODEL

    def rows(width, blk=0):
        return pl.BlockSpec((tm, width), lambda i: (i, blk))

    def const(shape):
        return pl.BlockSpec(shape, lambda i: (0, 0))

    return pl.pallas_call(
        _merge_kernel,
        grid=(m // tm,),
        in_specs=[
            rows(D_MODEL), rows(V_COLS), rows(SSM_D_INNER),
            rows(D_MODEL, ga_blk), rows(D_MODEL, ga_blk + 1),
            const((V_COLS, D_MODEL)), const((SSM_D_INNER, D_MODEL)), const((D_MODEL, D_MODEL)),
            const((1, D_MODEL)),
        ],
        out_specs=[rows(D_MODEL), rows(D_MODEL)],
        out_shape=[jax.ShapeDtypeStruct((m, D_MODEL), F32),
                   jax.ShapeDtypeStruct((m, D_MODEL), BF16)],
        compiler_params=pltpu.CompilerParams(
            dimension_semantics=("parallel",), vmem_limit_bytes=VMEM_LIMIT),
        name="merge",
    )(x2, y_attn, y_ssm, proj, proj, wpa, wps, wo, g)


def _ffn_kernel(x1_ref, h_ref, hh_ref, wup_ref, cw_ref, cb_ref, wd_ref, o_ref, gscr, vscr,
                *, tm, tiles_per_seq):
    i = pl.program_id(0)
    first = (i % tiles_per_seq) == 0
    h = h_ref[...]
    halo = hh_ref[...]
    halo = jnp.where(first, jnp.zeros_like(halo), halo)
    taps = cw_ref.shape[0]

    def up_project(idx):
        start, width = FFN_SPLITS[idx]
        for col0, scr in ((start, gscr), (D_FF + start, vscr)):
            w = wup_ref[:, col0:col0 + width]
            scr[idx % 2, 0:SUBLANES, 0:width] = _dot(halo, w)
            scr[idx % 2, SUBLANES:SUBLANES + tm, 0:width] = _dot(h, w)

    def conv(idx, col0, scr):
        width = FFN_SPLITS[idx][1]
        cols = slice(col0, col0 + width)
        out = cb_ref[:, cols] + cw_ref[taps - 1:taps, cols] * scr[idx % 2, SUBLANES:SUBLANES + tm, 0:width]
        for k in range(taps - 1):
            off = SUBLANES - (taps - 1) + k
            out = out + cw_ref[k:k + 1, cols] * scr[idx % 2, off:off + tm, 0:width]
        return out

    acc = x1_ref[...]
    up_project(0)
    for idx, (start, width) in enumerate(FFN_SPLITS):
        if idx + 1 < len(FFN_SPLITS):
            up_project(idx + 1)
        act = (_silu(conv(idx, start, gscr)) * conv(idx, D_FF + start, vscr)).astype(BF16)
        acc = acc + _dot(act, wd_ref[start:start + width, :])

    o_ref[...] = acc


def _ffn(x1, h2, w_up, cw, cb, w_down, seq):
    m = x1.shape[0]
    tm = FFN_TM
    hb = tm // SUBLANES
    kern = functools.partial(_ffn_kernel, tm=tm, tiles_per_seq=seq // tm)

    def resident(shape):
        return pl.BlockSpec(shape, lambda i: (0, 0), pipeline_mode=pl.Buffered(1))

    return pl.pallas_call(
        kern,
        grid=(m // tm,),
        in_specs=[
            pl.BlockSpec((tm, D_MODEL), lambda i: (i, 0)),
            pl.BlockSpec((tm, D_MODEL), lambda i: (i, 0)),
            pl.BlockSpec((SUBLANES, D_MODEL), lambda i: (jnp.maximum(i * hb - 1, 0), 0)),
            resident((D_MODEL, 2 * D_FF)),
            resident((FFN_CONV, 2 * D_FF)),
            resident((1, 2 * D_FF)),
            resident((D_FF, D_MODEL)),
        ],
        out_specs=pl.BlockSpec((tm, D_MODEL), lambda i: (i, 0)),
        out_shape=jax.ShapeDtypeStruct((m, D_MODEL), F32),
        scratch_shapes=[
            pltpu.VMEM((2, tm + SUBLANES, FFN_SCR), F32),
            pltpu.VMEM((2, tm + SUBLANES, FFN_SCR), F32),
        ],
        compiler_params=pltpu.CompilerParams(
            dimension_semantics=("parallel",), vmem_limit_bytes=VMEM_LIMIT),
        name="ffn",
    )(x1, h2, h2, w_up, cw, cb, w_down)


def _block_diag_ones(width, chunk):
    r = lax.broadcasted_iota(jnp.int32, (width, width), 0) // chunk
    c = lax.broadcasted_iota(jnp.int32, (width, width), 1) // chunk
    return (r == c).astype(BF16)


def _head_expand(rows, cols, chunk):
    r = lax.broadcasted_iota(jnp.int32, (rows, cols), 0)
    c = lax.broadcasted_iota(jnp.int32, (rows, cols), 1) // chunk
    return (r == c).astype(BF16)


def _layer(x, rel_bias, norm_mix_g, w_in, q_norm_g, k_norm_g, lq1, lk1, lq2, lk2, attn_subln_g,
           conv_ssm_w, conv_ssm_b, dt_bias, a_log, d_skip, ssm_norm_g, w_proj_attn, w_proj_ssm,
           w_out, norm_ffn_g, w_up, conv_ffn_w, conv_ffn_b, w_down):
    b, s, d = x.shape
    m = b * s
    x2 = x.reshape(m, d)

    o_q, o_k, o_v = 0, Q_COLS, 2 * Q_COLS
    o_z = o_v + V_COLS
    o_xs = o_z + SSM_D_INNER
    o_b = o_xs + SSM_D_INNER
    o_c = o_b + BC_COLS
    o_dt = o_c + BC_COLS
    o_gate = o_dt + SSM_HEADS

    w_in = w_in.astype(BF16)

    def cols(start, width):
        return w_in[:, start:start + width]

    w_main = jnp.concatenate([
        cols(o_xs, SSM_D_INNER), cols(o_z, SSM_D_INNER), cols(o_gate, 2 * D_MODEL),
        cols(o_q, Q_COLS), cols(o_k, Q_COLS), cols(o_v, V_COLS),
        cols(o_b, BC_COLS), cols(o_c, BC_COLS)], axis=1)
    pad_h = LANES - SSM_HEADS
    w_dt = jnp.pad(cols(o_dt, SSM_HEADS), ((0, 0), (0, pad_h))).astype(BF16)
    dt_b = jnp.pad(dt_bias.astype(F32), (0, pad_h)).reshape(1, LANES)
    reps = Q_COLS // ATTN_HEAD_DIM
    qk_gain = jnp.stack([
        jnp.tile(q_norm_g.astype(F32) * (ATTN_HEAD_DIM ** -0.5 * LOG2E), reps),
        jnp.tile(k_norm_g.astype(F32), reps)]).reshape(2, 1, Q_COLS)
    bd = _block_diag_ones(MXU_WIDTH, ATTN_HEAD_DIM)

    proj, dt = _in_proj(x2, norm_mix_g.astype(F32).reshape(1, d), w_main, w_dt, dt_b, qk_gain, bd)
    proj3 = proj.reshape(b, s, PROJ_COLS)

    vec = lambda v: v.astype(F32).reshape(1, -1)
    bias, far, lam = _bias_build(rel_bias.astype(F32), vec(lq1), vec(lk1), vec(lq2), vec(lk2), ATTN_T)
    y_attn = _diff_attn(lam[0, :1], far[:, 0, 0], proj3, bias,
                        vec(attn_subln_g) * (1.0 - LAM_INIT))

    cw = conv_ssm_w.astype(F32)
    cbias = conv_ssm_b.astype(F32).reshape(1, -1)
    xs_sl = slice(0, SSM_D_INNER)
    b_sl = slice(SSM_D_INNER, SSM_D_INNER + BC_COLS)
    c_sl = slice(SSM_D_INNER + BC_COLS, SSM_D_INNER + 2 * BC_COLS)
    y_ssm = _ssd(
        proj3, dt.reshape(b, s, LANES),
        jnp.pad(a_log.astype(F32), (0, pad_h)).reshape(1, LANES),
        cw[:, xs_sl], cbias[:, xs_sl], cw[:, b_sl], cbias[:, b_sl], cw[:, c_sl], cbias[:, c_sl],
        jnp.repeat(d_skip.astype(F32), SSM_HEAD_DIM).reshape(1, SSM_D_INNER),
        vec(ssm_norm_g), _head_expand(LANES, SSM_D_INNER, SSM_HEAD_DIM))

    x1, h2 = _merge(x2, y_attn.reshape(m, V_COLS), y_ssm.reshape(m, SSM_D_INNER), proj,
                    w_proj_attn.astype(BF16), w_proj_ssm.astype(BF16), w_out.astype(BF16),
                    vec(norm_ffn_g))

    out = _ffn(x1, h2, w_up.astype(BF16), conv_ffn_w.astype(F32),
               conv_ffn_b.astype(F32).reshape(1, -1), w_down.astype(BF16), s)
    return out.reshape(b, s, d)


def kernel(x, rel_bias, norm_mix_g, w_in, q_norm_g, k_norm_g, lambda_q1, lambda_k1, lambda_q2,
           lambda_k2, attn_subln_g, conv_ssm_w, conv_ssm_b, dt_bias, a_log, d_skip, ssm_norm_g,
           w_proj_attn, w_proj_ssm, w_out, norm_ffn_g, w_up, conv_ffn_w, conv_ffn_b, w_down):
    depth = w_in.shape[0]
    assert depth == 1, "lambda_init and the stacked-parameter layout are specialised to depth 1"
    return _layer(x, rel_bias, norm_mix_g[0], w_in[0], q_norm_g[0], k_norm_g[0], lambda_q1[0],
                  lambda_k1[0], lambda_q2[0], lambda_k2[0], attn_subln_g[0], conv_ssm_w[0],
                  conv_ssm_b[0], dt_bias[0], a_log[0], d_skip[0], ssm_norm_g[0], w_proj_attn[0],
                  w_proj_ssm[0], w_out[0], norm_ffn_g[0], w_up[0], conv_ffn_w[0], conv_ffn_b[0],
                  w_down[0])
```

```python
import functools
import math

import jax
import jax.numpy as jnp
from jax import lax
from jax.experimental import pallas as pl
from jax.experimental.pallas import tpu as pltpu

F32 = jnp.float32
BF16 = jnp.bfloat16

D_MODEL = 1024
ATTN_HEADS = 8
ATTN_HEAD_DIM = 64
ATTN_V_DIM = 2 * ATTN_HEAD_DIM
NUM_BUCKETS = 32
MAX_DISTANCE = 128
SSM_D_INNER = 2048
SSM_HEAD_DIM = 64
SSM_HEADS = 32
SSM_GROUPS = 4
SSM_HEADS_PER_GROUP = SSM_HEADS // SSM_GROUPS
SSM_STATE = 128
SSM_CONV = 4
D_FF = 2816
FFN_CONV = 3
RMS_EPS = 1e-6
SUBLN_EPS = 1e-5
LAM_INIT = 0.8 - 0.6 * math.exp(0.0)

Q_COLS = ATTN_HEADS * 2 * ATTN_HEAD_DIM
V_COLS = ATTN_HEADS * ATTN_V_DIM
BC_COLS = SSM_GROUPS * SSM_STATE
GROUP_COLS = SSM_D_INNER // SSM_GROUPS

LANES = 128
SUBLANES = 8
BF16_SUBLANES = 16
MXU_WIDTH = 256
VMEM_LIMIT = 56 * 1024 * 1024

LOG2E = math.log2(math.e)
NEG_BIG = -1e30

OFF_XS = 0
OFF_Z = OFF_XS + SSM_D_INNER
OFF_GATE = OFF_Z + SSM_D_INNER
OFF_Q = OFF_GATE + 2 * D_MODEL
OFF_K = OFF_Q + Q_COLS
OFF_V = OFF_K + Q_COLS
OFF_B = OFF_V + V_COLS
OFF_C = OFF_B + BC_COLS
PROJ_COLS = OFF_C + BC_COLS

IN_TM = 512
IN_TN = 1024
IN_HALVES = 2
ATTN_T = 128
SSD_L = 128
MERGE_TM = 512
FFN_TM = 512
FFN_SPLITS = ((0, 512), (512, 512), (1024, 512), (1536, 512), (2048, 512), (2560, 256))
FFN_SCR = max(w for _, w in FFN_SPLITS)


def _split(v, parts):
    out = []
    for _ in range(parts - 1):
        hi = v.astype(BF16)
        out.append(hi)
        v = v - hi.astype(F32)
    out.append(v.astype(BF16))
    return out


def _dot(a, b):
    return jnp.dot(a, b, preferred_element_type=F32)


def _dot_nt(a, b):
    return lax.dot_general(a, b, (((1,), (1,)), ((), ())), preferred_element_type=F32)


def _dot_exact_rhs(parts, sel):
    out = _dot(parts[0], sel)
    for p in parts[1:]:
        out = out + _dot(p, sel)
    return out


def _dot_exact_lhs(sel, parts):
    out = _dot(sel, parts[0])
    for p in parts[1:]:
        out = out + _dot(sel, p)
    return out


def _silu(v):
    h = 0.5 * v
    return h + h * jnp.tanh(h)


def _in_proj_kernel(x_ref, g_ref, w_ref, wdt_ref, dtb_ref, qkg_ref, bd_ref, o_ref, dt_ref, h_scr,
                    *, tile_kinds):
    half = pl.program_id(1)

    @pl.when(half == 0)
    def _():
        x = x_ref[...]
        ms = jnp.mean(x * x, axis=-1, keepdims=True)
        h_scr[...] = (x * lax.rsqrt(ms + RMS_EPS) * g_ref[...]).astype(BF16)
        t = _dot(h_scr[...], wdt_ref[...]) + dtb_ref[...]
        dt_ref[...] = jnp.maximum(t, 0.0) + jnp.log1p(jnp.exp(-jnp.abs(t)))

    h = h_scr[...]

    def tile_cols(t):
        return slice(t * IN_TN, (t + 1) * IN_TN)

    def store_plain(t, acc):
        o_ref[:, tile_cols(t)] = acc.astype(BF16)

    def store_head_normed(t, acc, gain_row):
        bd = bd_ref[...]
        width = bd.shape[0]
        for cblk in range(IN_TN // width):
            cols = slice(cblk * width, (cblk + 1) * width)
            a = acc[:, cols]
            ssum = _dot((a * a).astype(BF16), bd)
            y = a * lax.rsqrt(ssum * (1.0 / ATTN_HEAD_DIM) + RMS_EPS)
            y = y * qkg_ref[gain_row, :, cols]
            o_ref[:, t * IN_TN + cblk * width:t * IN_TN + (cblk + 1) * width] = y.astype(BF16)

    for hv, kinds in enumerate(tile_kinds):
        @pl.when(half == hv)
        def _(kinds=kinds):
            acc_next = _dot(h, w_ref[:, tile_cols(0)])
            for t, kind in enumerate(kinds):
                acc = acc_next
                if t + 1 < len(kinds):
                    acc_next = _dot(h, w_ref[:, tile_cols(t + 1)])
                if kind == "plain":
                    store_plain(t, acc)
                else:
                    store_head_normed(t, acc, 0 if kind == "q" else 1)


def _in_proj(x2, g, w_main, w_dt, dt_bias, qk_gain, bd):
    m = x2.shape[0]
    half_cols = PROJ_COLS // IN_HALVES
    tiles = half_cols // IN_TN
    kinds = {OFF_Q // IN_TN: "q", OFF_K // IN_TN: "k"}
    tile_kinds = tuple(tuple(kinds.get(hv * tiles + t, "plain") for t in range(tiles))
                       for hv in range(IN_HALVES))
    kern = functools.partial(_in_proj_kernel, tile_kinds=tile_kinds)
    return pl.pallas_call(
        kern,
        grid=(m // IN_TM, IN_HALVES),
        in_specs=[
            pl.BlockSpec((IN_TM, D_MODEL), lambda i, hv: (i, 0)),
            pl.BlockSpec((1, D_MODEL), lambda i, hv: (0, 0)),
            pl.BlockSpec((D_MODEL, half_cols), lambda i, hv: (0, hv)),
            pl.BlockSpec((D_MODEL, LANES), lambda i, hv: (0, 0)),
            pl.BlockSpec((1, LANES), lambda i, hv: (0, 0)),
            pl.BlockSpec(qk_gain.shape, lambda i, hv: (0, 0, 0)),
            pl.BlockSpec(bd.shape, lambda i, hv: (0, 0)),
        ],
        out_specs=[
            pl.BlockSpec((IN_TM, half_cols), lambda i, hv: (i, hv)),
            pl.BlockSpec((IN_TM, LANES), lambda i, hv: (i, 0)),
        ],
        out_shape=[
            jax.ShapeDtypeStruct((m, PROJ_COLS), BF16),
            jax.ShapeDtypeStruct((m, LANES), F32),
        ],
        scratch_shapes=[pltpu.VMEM((IN_TM, D_MODEL), BF16)],
        compiler_params=pltpu.CompilerParams(
            dimension_semantics=("parallel", "arbitrary"), vmem_limit_bytes=VMEM_LIMIT),
        name="in_proj",
    )(x2, g, w_main, w_dt, dt_bias, qk_gain, bd)


def _bias_kernel(rel_ref, lq1_ref, lk1_ref, lq2_ref, lk2_ref, bias_ref, far_ref, lam_ref, *, tile):
    h = pl.program_id(0)
    row = lax.broadcasted_iota(jnp.int32, (tile, tile), 0)
    col = lax.broadcasted_iota(jnp.int32, (tile, tile), 1)
    max_exact = NUM_BUCKETS // 2
    for t in range(2):
        d = row - col + t * tile
        n = jnp.maximum(d, 0)
        nf = jnp.maximum(n, 1).astype(F32)
        large = max_exact + (jnp.log(nf / max_exact) / math.log(MAX_DISTANCE / max_exact)
                             * (NUM_BUCKETS - max_exact)).astype(jnp.int32)
        large = jnp.minimum(large, NUM_BUCKETS - 1)
        bucket = jnp.where(n < max_exact, n, large)
        bias = jnp.zeros((tile, tile), F32)
        for bk in range(NUM_BUCKETS):
            bias = jnp.where(bucket == bk, rel_ref[bk, h], bias)
        bias = bias * LOG2E
        if t == 0:
            bias = jnp.where(d >= 0, bias, NEG_BIG)
        bias_ref[0, t] = bias
    far_ref[...] = jnp.full(far_ref.shape, rel_ref[NUM_BUCKETS - 1, h] * LOG2E, F32)
    s1 = jnp.sum(lq1_ref[...] * lk1_ref[...], axis=-1, keepdims=True)
    s2 = jnp.sum(lq2_ref[...] * lk2_ref[...], axis=-1, keepdims=True)
    lam = jnp.exp(s1) - jnp.exp(s2) + LAM_INIT
    lam_ref[...] = jnp.broadcast_to(lam, lam_ref.shape)


def _bias_build(rel_bias, lq1, lk1, lq2, lk2, tile):
    vec = pl.BlockSpec((1, ATTN_HEAD_DIM), lambda h: (0, 0))
    return pl.pallas_call(
        functools.partial(_bias_kernel, tile=tile),
        grid=(ATTN_HEADS,),
        in_specs=[pl.BlockSpec(memory_space=pltpu.SMEM), vec, vec, vec, vec],
        out_specs=[
            pl.BlockSpec((1, 2, tile, tile), lambda h: (h, 0, 0, 0)),
            pl.BlockSpec((1, SUBLANES, LANES), lambda h: (h, 0, 0)),
            pl.BlockSpec((SUBLANES, LANES), lambda h: (0, 0)),
        ],
        out_shape=[
            jax.ShapeDtypeStruct((ATTN_HEADS, 2, tile, tile), F32),
            jax.ShapeDtypeStruct((ATTN_HEADS, SUBLANES, LANES), F32),
            jax.ShapeDtypeStruct((SUBLANES, LANES), F32),
        ],
        compiler_params=pltpu.CompilerParams(dimension_semantics=("arbitrary",)),
        name="bias_build",
    )(rel_bias, lq1, lk1, lq2, lk2)


def _attn_kernel(lam_ref, far_ref, q_ref, k_ref, v_ref, bias_ref, g_ref, o_ref, *, tile, nq):
    h = pl.program_id(1)
    lam = lam_ref[0]
    far = far_ref[h]
    lane = lax.broadcasted_iota(jnp.int32, (tile, LANES), 1)
    lo_half = lane < ATTN_HEAD_DIM
    gain = g_ref[...]

    def scores(qi, which):
        q = q_ref[0, qi * tile:(qi + 1) * tile, :]
        zero = jnp.zeros_like(q)
        q_map = jnp.where(lo_half, q, zero) if which == 0 else jnp.where(lo_half, zero, q)
        return _dot_nt(q_map, k_ref[0, 0:(qi + 1) * tile, :])

    def softmax(qi, s):
        nk = (qi + 1) * tile
        n_far = nk - min(qi + 1, 2) * tile
        if qi == 0:
            near_bias = bias_ref[0, 0]
        else:
            near_bias = jnp.concatenate([bias_ref[0, 1], bias_ref[0, 0]], axis=1)
        s_near = s[:, n_far:] + near_bias
        m = jnp.max(s_near, axis=-1, keepdims=True)
        if n_far:
            s_far = s[:, :n_far]
            m = jnp.maximum(m, jnp.max(s_far, axis=-1, keepdims=True) + far)
            p = jnp.concatenate([jnp.exp2(s_far - (m - far)), jnp.exp2(s_near - m)], axis=1)
        else:
            p = jnp.exp2(s_near - m)
        return p.astype(BF16), jnp.sum(p, axis=-1, keepdims=True)

    def weighted_values(qi, p, l):
        return _dot(p, v_ref[0, 0:(qi + 1) * tile, :]) / l

    units = [(qi, which) for qi in reversed(range(nq)) for which in range(2)]
    n_units = len(units)
    s_q, p_q, outs = {}, {}, {}
    lag_s, lag_v = 2, 4
    for step in range(n_units + lag_v):
        if step < n_units:
            s_q[step] = scores(*units[step])
        if 0 <= step - lag_s < n_units:
            p_q[step - lag_s] = softmax(units[step - lag_s][0], s_q.pop(step - lag_s))
        if 0 <= step - lag_v < n_units:
            qi, which = units[step - lag_v]
            outs[which] = weighted_values(qi, *p_q.pop(step - lag_v))
            if which == 1:
                o = outs[0] - lam * outs[1]
                ms = jnp.mean(o * o, axis=-1, keepdims=True)
                o = o * lax.rsqrt(ms + SUBLN_EPS) * gain
                o_ref[0, qi * tile:(qi + 1) * tile, :] = o.astype(o_ref.dtype)


def _diff_attn(lam, far, proj3, bias, gain):
    b, s, _ = proj3.shape
    tile = bias.shape[-1]
    nq = s // tile
    qb, kb, vb = OFF_Q // LANES, OFF_K // LANES, OFF_V // LANES
    kern = functools.partial(_attn_kernel, tile=tile, nq=nq)
    return pl.pallas_call(
        kern,
        grid=(b, ATTN_HEADS),
        in_specs=[
            pl.BlockSpec(memory_space=pltpu.SMEM),
            pl.BlockSpec(memory_space=pltpu.SMEM),
            pl.BlockSpec((1, s, LANES), lambda i, h: (i, 0, qb + h)),
            pl.BlockSpec((1, s, LANES), lambda i, h: (i, 0, kb + h)),
            pl.BlockSpec((1, s, LANES), lambda i, h: (i, 0, vb + h)),
            pl.BlockSpec((1, 2, tile, tile), lambda i, h: (h, 0, 0, 0)),
            pl.BlockSpec((1, LANES), lambda i, h: (0, 0)),
        ],
        out_specs=pl.BlockSpec((1, s, LANES), lambda i, h: (i, 0, h)),
        out_shape=jax.ShapeDtypeStruct((b, s, V_COLS), BF16),
        compiler_params=pltpu.CompilerParams(
            dimension_semantics=("parallel", "parallel"), vmem_limit_bytes=VMEM_LIMIT),
        name="diff_attn",
    )(lam, far, proj3, proj3, proj3, bias, gain)


def _shift_matrices(taps, length, halo_rows):
    def build(rows, cols, offset):
        r = lax.broadcasted_iota(jnp.int32, (taps * rows, cols), 0)
        j = lax.broadcasted_iota(jnp.int32, (taps * rows, cols), 1)
        tap = r // rows
        return (j == offset + (r - tap * rows) - (taps - 1 - tap)).astype(BF16)

    return build(length, length, 0), build(halo_rows, halo_rows, halo_rows)


def _conv_silu(main_ref, halo_ref, w_ref, b_ref, shifts, first, length, cols=slice(None)):
    shift_main, shift_head = shifts
    hrows = halo_ref.shape[1]
    halo = halo_ref[0, :, cols]
    halo = jnp.where(first, jnp.zeros_like(halo), halo)
    shifted = _dot(shift_main, main_ref[0, :, cols])
    head = _dot(shift_head, halo)
    acc = b_ref[:, cols]
    for k in range(w_ref.shape[0]):
        tap = shifted[k * length:(k + 1) * length]
        tap = jnp.concatenate([tap[0:hrows] + head[k * hrows:(k + 1) * hrows], tap[hrows:]], axis=0)
        acc = acc + w_ref[k:k + 1, cols] * tap
    return _silu(acc)


def _ssd_kernel(xs_ref, xh_ref, bm_ref, bh_ref, cm_ref, ch_ref, z_ref, dt_ref, alog_ref,
                wx_ref, bx_ref, wb_ref, bb_ref, wc_ref, bc_ref, dskip_ref, ng_ref, e_ref,
                y_ref, state_scr, *, length):
    c = pl.program_id(1)
    first = c == 0

    @pl.when(first)
    def _():
        state_scr[...] = jnp.zeros_like(state_scr)

    shift = _shift_matrices(SSM_CONV, length, BF16_SUBLANES)
    bm = _conv_silu(bm_ref, bh_ref, wb_ref, bb_ref, shift, first, length).astype(BF16)
    cm = _conv_silu(cm_ref, ch_ref, wc_ref, bc_ref, shift, first, length).astype(BF16)

    def conv_x(g):
        cols = slice(g * GROUP_COLS, (g + 1) * GROUP_COLS)
        return _conv_silu(xs_ref, xh_ref, wx_ref, bx_ref, shift, first, length, cols)

    xs_all = [conv_x(g) for g in range(SSM_GROUPS)]

    dt = dt_ref[0]
    a = -jnp.exp(alog_ref[...])
    adt = dt * a
    row = lax.broadcasted_iota(jnp.int32, (length, length), 0)
    col = lax.broadcasted_iota(jnp.int32, (length, length), 1)
    causal = row >= col
    tri = causal.astype(BF16)
    cum = _dot_exact_lhs(tri, _split(adt, 3))
    cum2 = cum * LOG2E
    src_t = cum2.T - jnp.log2(dt.T)
    total = cum[length - 1:length, :]
    sel2 = e_ref[...]

    def expand(v):
        return _dot(jnp.concatenate(_split(v, 2), axis=1), sel2)

    ecum_x = expand(jnp.exp(cum))
    edtw_x = expand(dt * jnp.exp(total - cum))
    etot_x = ecum_x[length - 1:length, :]

    lane = lax.broadcasted_iota(jnp.int32, (length, LANES), 1)
    lo_half = lane < SSM_HEAD_DIM

    for g in range(SSM_GROUPS):
        gs = slice(g * SSM_STATE, (g + 1) * SSM_STATE)
        cs = slice(g * GROUP_COLS, (g + 1) * GROUP_COLS)
        xs_f = xs_all[g]
        b_g = bm[:, gs]
        c_g = cm[:, gs]
        xs_g = xs_f.astype(BF16)
        cb = _dot_nt(c_g, b_g)
        st = state_scr[g]
        y_g = _dot(c_g, st.astype(BF16)) * ecum_x[:, cs]
        xw = (xs_f * edtw_x[:, cs]).astype(BF16)
        b_t = b_g.astype(F32).T.astype(BF16)
        state_scr[g] = st * etot_x[:, cs] + _dot(b_t, xw)
        y_parts = []
        for pair in range(SSM_HEADS_PER_GROUP // 2):
            ms = []
            for hh in range(2):
                h = g * SSM_HEADS_PER_GROUP + pair * 2 + hh
                seg = jnp.where(causal, cum2[:, h:h + 1] - src_t[h:h + 1, :], NEG_BIG)
                ms.append((cb * jnp.exp2(seg)).astype(BF16))
            x_pair = xs_g[:, pair * LANES:(pair + 1) * LANES]
            zero = jnp.zeros_like(x_pair)
            rhs = jnp.concatenate([jnp.where(lo_half, x_pair, zero),
                                   jnp.where(lo_half, zero, x_pair)], axis=0)
            y_parts.append(_dot(jnp.concatenate(ms, axis=1), rhs))
        y_g = y_g + jnp.concatenate(y_parts, axis=1)
        y_g = y_g + xs_f * dskip_ref[:, cs]
        y_g = y_g * _silu(z_ref[0, :, cs].astype(F32))
        msq = jnp.mean(y_g * y_g, axis=-1, keepdims=True)
        y_g = y_g * lax.rsqrt(msq + SUBLN_EPS) * ng_ref[:, cs]
        y_ref[0, :, cs] = y_g.astype(y_ref.dtype)


def _ssd(proj3, dt3, alog, wx, bx, wb, bb, wc, bc, dskip_x, ng, sel):
    b, s, _ = proj3.shape
    length = SSD_L
    nc = s // length
    hb = length // BF16_SUBLANES
    xs_blk, z_blk = OFF_XS // SSM_D_INNER, OFF_Z // SSM_D_INNER
    b_blk, c_blk = OFF_B // BC_COLS, OFF_C // BC_COLS

    def main(width, blk):
        return pl.BlockSpec((1, length, width), lambda i, c: (i, c, blk))

    def halo(width, blk):
        return pl.BlockSpec((1, BF16_SUBLANES, width),
                            lambda i, c: (i, jnp.maximum(c * hb - 1, 0), blk))

    def const(shape):
        return pl.BlockSpec(shape, lambda i, c: (0,) * len(shape))

    kern = functools.partial(_ssd_kernel, length=length)
    return pl.pallas_call(
        kern,
        grid=(b, nc),
        in_specs=[
            main(SSM_D_INNER, xs_blk), halo(SSM_D_INNER, xs_blk),
            main(BC_COLS, b_blk), halo(BC_COLS, b_blk),
            main(BC_COLS, c_blk), halo(BC_COLS, c_blk),
            main(SSM_D_INNER, z_blk),
            pl.BlockSpec((1, length, LANES), lambda i, c: (i, c, 0)),
            const((1, LANES)),
            const((SSM_CONV, SSM_D_INNER)), const((1, SSM_D_INNER)),
            const((SSM_CONV, BC_COLS)), const((1, BC_COLS)),
            const((SSM_CONV, BC_COLS)), const((1, BC_COLS)),
            const((1, SSM_D_INNER)), const((1, SSM_D_INNER)),
            const((2 * LANES, SSM_D_INNER)),
        ],
        out_specs=pl.BlockSpec((1, length, SSM_D_INNER), lambda i, c: (i, c, 0)),
        out_shape=jax.ShapeDtypeStruct((b, s, SSM_D_INNER), BF16),
        scratch_shapes=[pltpu.VMEM((SSM_GROUPS, SSM_STATE, GROUP_COLS), F32)],
        compiler_params=pltpu.CompilerParams(
            dimension_semantics=("parallel", "arbitrary"), vmem_limit_bytes=VMEM_LIMIT),
        name="ssd",
    )(proj3, proj3, proj3, proj3, proj3, proj3, proj3, dt3, alog,
      wx, bx, wb, bb, wc, bc, dskip_x, ng, sel)


def _merge_kernel(x_ref, ya_ref, ys_ref, ga_ref, gs_ref, wpa_ref, wps_ref, wo_ref, g_ref,
                  x1_ref, h2_ref):
    pa = _dot(ya_ref[...], wpa_ref[...])
    ps = _dot(ys_ref[...], wps_ref[...])
    ga = 1.0 / (1.0 + jnp.exp(-ga_ref[...].astype(F32)))
    gs = 1.0 / (1.0 + jnp.exp(-gs_ref[...].astype(F32)))
    mixed = (ga * pa + gs * ps).astype(BF16)
    x1 = x_ref[...] + _dot(mixed, wo_ref[...])
    x1_ref[...] = x1
    ms = jnp.mean(x1 * x1, axis=-1, keepdims=True)
    h2_ref[...] = (x1 * lax.rsqrt(ms + RMS_EPS) * g_ref[...]).astype(BF16)


def _merge(x2, y_attn, y_ssm, proj, wpa, wps, wo, g):
    m = x2.shape[0]
    tm = MERGE_TM
    ga_blk = OFF_GATE // D_MODEL

    def rows(width, blk=0):
        return pl.BlockSpec((tm, width), lambda i: (i, blk))

    def const(shape):
        return pl.BlockSpec(shape, lambda i: (0, 0))

    return pl.pallas_call(
        _merge_kernel,
        grid=(m // tm,),
        in_specs=[
            rows(D_MODEL), rows(V_COLS), rows(SSM_D_INNER),
            rows(D_MODEL, ga_blk), rows(D_MODEL, ga_blk + 1),
            const((V_COLS, D_MODEL)), const((SSM_D_INNER, D_MODEL)), const((D_MODEL, D_MODEL)),
            const((1, D_MODEL)),
        ],
        out_specs=[rows(D_MODEL), rows(D_MODEL)],
        out_shape=[jax.ShapeDtypeStruct((m, D_MODEL), F32),
                   jax.ShapeDtypeStruct((m, D_MODEL), BF16)],
        compiler_params=pltpu.CompilerParams(
            dimension_semantics=("parallel",), vmem_limit_bytes=VMEM_LIMIT),
        name="merge",
    )(x2, y_attn, y_ssm, proj, proj, wpa, wps, wo, g)


def _ffn_kernel(x1_ref, h_ref, hh_ref, wup_ref, cw_ref, cb_ref, wd_ref, o_ref, gscr, vscr,
                *, tm, tiles_per_seq):
    i = pl.program_id(0)
    first = (i % tiles_per_seq) == 0
    h = h_ref[...]
    halo = hh_ref[...]
    halo = jnp.where(first, jnp.zeros_like(halo), halo)
    taps = cw_ref.shape[0]

    def up_project(idx):
        start, width = FFN_SPLITS[idx]
        for col0, scr in ((start, gscr), (D_FF + start, vscr)):
            w = wup_ref[:, col0:col0 + width]
            scr[idx % 2, 0:SUBLANES, 0:width] = _dot(halo, w)
            scr[idx % 2, SUBLANES:SUBLANES + tm, 0:width] = _dot(h, w)

    def conv(idx, col0, scr):
        width = FFN_SPLITS[idx][1]
        cols = slice(col0, col0 + width)
        out = cb_ref[:, cols] + cw_ref[taps - 1:taps, cols] * scr[idx % 2, SUBLANES:SUBLANES + tm, 0:width]
        for k in range(taps - 1):
            off = SUBLANES - (taps - 1) + k
            out = out + cw_ref[k:k + 1, cols] * scr[idx % 2, off:off + tm, 0:width]
        return out

    acc = x1_ref[...]
    up_project(0)
    for idx, (start, width) in enumerate(FFN_SPLITS):
        if idx + 1 < len(FFN_SPLITS):
            up_project(idx + 1)
        act = (_silu(conv(idx, start, gscr)) * conv(idx, D_FF + start, vscr)).astype(BF16)
        acc = acc + _dot(act, wd_ref[start:start + width, :])

    o_ref[...] = acc


def _ffn(x1, h2, w_up, cw, cb, w_down, seq):
    m = x1.shape[0]
    tm = FFN_TM
    hb = tm // SUBLANES
    kern = functools.partial(_ffn_kernel, tm=tm, tiles_per_seq=seq // tm)

    def resident(shape):
        return pl.BlockSpec(shape, lambda i: (0, 0), pipeline_mode=pl.Buffered(1))

    return pl.pallas_call(
        kern,
        grid=(m // tm,),
        in_specs=[
            pl.BlockSpec((tm, D_MODEL), lambda i: (i, 0)),
            pl.BlockSpec((tm, D_MODEL), lambda i: (i, 0)),
            pl.BlockSpec((SUBLANES, D_MODEL), lambda i: (jnp.maximum(i * hb - 1, 0), 0)),
            resident((D_MODEL, 2 * D_FF)),
            resident((FFN_CONV, 2 * D_FF)),
            resident((1, 2 * D_FF)),
            resident((D_FF, D_MODEL)),
        ],
        out_specs=pl.BlockSpec((tm, D_MODEL), lambda i: (i, 0)),
        out_shape=jax.ShapeDtypeStruct((m, D_MODEL), F32),
        scratch_shapes=[
            pltpu.VMEM((2, tm + SUBLANES, FFN_SCR), F32),
            pltpu.VMEM((2, tm + SUBLANES, FFN_SCR), F32),
        ],
        compiler_params=pltpu.CompilerParams(
            dimension_semantics=("parallel",), vmem_limit_bytes=VMEM_LIMIT),
        name="ffn",
    )(x1, h2, h2, w_up, cw, cb, w_down)


def _block_diag_ones(width, chunk):
    r = lax.broadcasted_iota(jnp.int32, (width, width), 0) // chunk
    c = lax.broadcasted_iota(jnp.int32, (width, width), 1) // chunk
    return (r == c).astype(BF16)


def _head_expand(rows, cols, chunk):
    r = lax.broadcasted_iota(jnp.int32, (rows, cols), 0)
    c = lax.broadcasted_iota(jnp.int32, (rows, cols), 1) // chunk
    return (r == c).astype(BF16)


def _layer(x, rel_bias, norm_mix_g, w_in, q_norm_g, k_norm_g, lq1, lk1, lq2, lk2, attn_subln_g,
           conv_ssm_w, conv_ssm_b, dt_bias, a_log, d_skip, ssm_norm_g, w_proj_attn, w_proj_ssm,
           w_out, norm_ffn_g, w_up, conv_ffn_w, conv_ffn_b, w_down):
    b, s, d = x.shape
    m = b * s
    x2 = x.reshape(m, d)

    o_q, o_k, o_v = 0, Q_COLS, 2 * Q_COLS
    o_z = o_v + V_COLS
    o_xs = o_z + SSM_D_INNER
    o_b = o_xs + SSM_D_INNER
    o_c = o_b + BC_COLS
    o_dt = o_c + BC_COLS
    o_gate = o_dt + SSM_HEADS

    w_in = w_in.astype(BF16)

    def cols(start, width):
        return w_in[:, start:start + width]

    w_main = jnp.concatenate([
        cols(o_xs, SSM_D_INNER), cols(o_z, SSM_D_INNER), cols(o_gate, 2 * D_MODEL),
        cols(o_q, Q_COLS), cols(o_k, Q_COLS), cols(o_v, V_COLS),
        cols(o_b, BC_COLS), cols(o_c, BC_COLS)], axis=1)
    pad_h = LANES - SSM_HEADS
    w_dt = jnp.pad(cols(o_dt, SSM_HEADS), ((0, 0), (0, pad_h))).astype(BF16)
    dt_b = jnp.pad(dt_bias.astype(F32), (0, pad_h)).reshape(1, LANES)
    reps = Q_COLS // ATTN_HEAD_DIM
    qk_gain = jnp.stack([
        jnp.tile(q_norm_g.astype(F32) * (ATTN_HEAD_DIM ** -0.5 * LOG2E), reps),
        jnp.tile(k_norm_g.astype(F32), reps)]).reshape(2, 1, Q_COLS)
    bd = _block_diag_ones(MXU_WIDTH, ATTN_HEAD_DIM)

    proj, dt = _in_proj(x2, norm_mix_g.astype(F32).reshape(1, d), w_main, w_dt, dt_b, qk_gain, bd)
    proj3 = proj.reshape(b, s, PROJ_COLS)

    vec = lambda v: v.astype(F32).reshape(1, -1)
    bias, far, lam = _bias_build(rel_bias.astype(F32), vec(lq1), vec(lk1), vec(lq2), vec(lk2), ATTN_T)
    y_attn = _diff_attn(lam[0, :1], far[:, 0, 0], proj3, bias,
                        vec(attn_subln_g) * (1.0 - LAM_INIT))

    cw = conv_ssm_w.astype(F32)
    cbias = conv_ssm_b.astype(F32).reshape(1, -1)
    xs_sl = slice(0, SSM_D_INNER)
    b_sl = slice(SSM_D_INNER, SSM_D_INNER + BC_COLS)
    c_sl = slice(SSM_D_INNER + BC_COLS, SSM_D_INNER + 2 * BC_COLS)
    y_ssm = _ssd(
        proj3, dt.reshape(b, s, LANES),
        jnp.pad(a_log.astype(F32), (0, pad_h)).reshape(1, LANES),
        cw[:, xs_sl], cbias[:, xs_sl], cw[:, b_sl], cbias[:, b_sl], cw[:, c_sl], cbias[:, c_sl],
        jnp.repeat(d_skip.astype(F32), SSM_HEAD_DIM).reshape(1, SSM_D_INNER),
        vec(ssm_norm_g), jnp.tile(_head_expand(LANES, SSM_D_INNER, SSM_HEAD_DIM), (2, 1)))

    x1, h2 = _merge(x2, y_attn.reshape(m, V_COLS), y_ssm.reshape(m, SSM_D_INNER), proj,
                    w_proj_attn.astype(BF16), w_proj_ssm.astype(BF16), w_out.astype(BF16),
                    vec(norm_ffn_g))

    out = _ffn(x1, h2, w_up.astype(BF16), conv_ffn_w.astype(F32),
               conv_ffn_b.astype(F32).reshape(1, -1), w_down.astype(BF16), s)
    return out.reshape(b, s, d)


def kernel(x, rel_bias, norm_mix_g, w_in, q_norm_g, k_norm_g, lambda_q1, lambda_k1, lambda_q2,
           lambda_k2, attn_subln_g, conv_ssm_w, conv_ssm_b, dt_bias, a_log, d_skip, ssm_norm_g,
           w_proj_attn, w_proj_ssm, w_out, norm_ffn_g, w_up, conv_ffn_w, conv_ffn_b, w_down):
    depth = w_in.shape[0]
    assert depth == 1, "lambda_init and the stacked-parameter layout are specialised to depth 1"
    return _layer(x, rel_bias, norm_mix_g[0], w_in[0], q_norm_g[0], k_norm_g[0], lambda_q1[0],
                  lambda_k1[0], lambda_q2[0], lambda_k2[0], attn_subln_g[0], conv_ssm_w[0],
                  conv_ssm_b[0], dt_bias[0], a_log[0], d_skip[0], ssm_norm_g[0], w_proj_attn[0],
                  w_proj_ssm[0], w_out[0], norm_ffn_g[0], w_up[0], conv_ffn_w[0], conv_ffn_b[0],
                  w_down[0])
```

```python
import functools
import math

import jax
import jax.numpy as jnp
from jax import lax
from jax.experimental import pallas as pl
from jax.experimental.pallas import tpu as pltpu

F32 = jnp.float32
BF16 = jnp.bfloat16

D_MODEL = 1024
ATTN_HEADS = 8
ATTN_HEAD_DIM = 64
ATTN_V_DIM = 2 * ATTN_HEAD_DIM
NUM_BUCKETS = 32
MAX_DISTANCE = 128
SSM_D_INNER = 2048
SSM_HEAD_DIM = 64
SSM_HEADS = 32
SSM_GROUPS = 4
SSM_HEADS_PER_GROUP = SSM_HEADS // SSM_GROUPS
SSM_STATE = 128
SSM_CONV = 4
D_FF = 2816
FFN_CONV = 3
RMS_EPS = 1e-6
SUBLN_EPS = 1e-5
LAM_INIT = 0.8 - 0.6 * math.exp(0.0)

Q_COLS = ATTN_HEADS * 2 * ATTN_HEAD_DIM
V_COLS = ATTN_HEADS * ATTN_V_DIM
BC_COLS = SSM_GROUPS * SSM_STATE
GROUP_COLS = SSM_D_INNER // SSM_GROUPS

LANES = 128
SUBLANES = 8
BF16_SUBLANES = 16
MXU_WIDTH = 256
VMEM_LIMIT = 56 * 1024 * 1024

LOG2E = math.log2(math.e)
NEG_BIG = -1e30

OFF_XS = 0
OFF_Z = OFF_XS + SSM_D_INNER
OFF_GATE = OFF_Z + SSM_D_INNER
OFF_Q = OFF_GATE + 2 * D_MODEL
OFF_K = OFF_Q + Q_COLS
OFF_V = OFF_K + Q_COLS
OFF_B = OFF_V + V_COLS
OFF_C = OFF_B + BC_COLS
PROJ_COLS = OFF_C + BC_COLS

IN_TM = 512
IN_TN = 1024
IN_HALVES = 2
ATTN_T = 128
ATTN_LAGS = (1, 3)
SSD_L = 128
MERGE_TM = 512
FFN_TM = 512
FFN_SPLITS = ((0, 512), (512, 512), (1024, 512), (1536, 512), (2048, 512), (2560, 256))
FFN_SCR = max(w for _, w in FFN_SPLITS)
FFN_SLOTS = 3


def _split(v, parts):
    out = []
    for _ in range(parts - 1):
        hi = v.astype(BF16)
        out.append(hi)
        v = v - hi.astype(F32)
    out.append(v.astype(BF16))
    return out


def _dot(a, b):
    return jnp.dot(a, b, preferred_element_type=F32)


def _dot_nt(a, b):
    return lax.dot_general(a, b, (((1,), (1,)), ((), ())), preferred_element_type=F32)


def _dot_exact_rhs(parts, sel):
    out = _dot(parts[0], sel)
    for p in parts[1:]:
        out = out + _dot(p, sel)
    return out


def _dot_exact_lhs(sel, parts):
    out = _dot(sel, parts[0])
    for p in parts[1:]:
        out = out + _dot(sel, p)
    return out


def _silu(v):
    h = 0.5 * v
    return h + h * jnp.tanh(h)


def _in_proj_kernel(x_ref, g_ref, w_ref, wdt_ref, dtb_ref, qkg_ref, bd_ref, o_ref, dt_ref, h_scr,
                    *, tile_kinds):
    half = pl.program_id(1)

    @pl.when(half == 0)
    def _():
        x = x_ref[...]
        ms = jnp.mean(x * x, axis=-1, keepdims=True)
        h_scr[...] = (x * lax.rsqrt(ms + RMS_EPS) * g_ref[...]).astype(BF16)
        t = _dot(h_scr[...], wdt_ref[...]) + dtb_ref[...]
        dt_ref[...] = jnp.maximum(t, 0.0) + jnp.log1p(jnp.exp(-jnp.abs(t)))

    h = h_scr[...]

    def tile_cols(t):
        return slice(t * IN_TN, (t + 1) * IN_TN)

    def store_plain(t, acc):
        o_ref[:, tile_cols(t)] = acc.astype(BF16)

    def store_head_normed(t, acc, gain_row):
        bd = bd_ref[...]
        width = bd.shape[0]
        for cblk in range(IN_TN // width):
            cols = slice(cblk * width, (cblk + 1) * width)
            a = acc[:, cols]
            ssum = _dot((a * a).astype(BF16), bd)
            y = a * lax.rsqrt(ssum * (1.0 / ATTN_HEAD_DIM) + RMS_EPS)
            y = y * qkg_ref[gain_row, :, cols]
            o_ref[:, t * IN_TN + cblk * width:t * IN_TN + (cblk + 1) * width] = y.astype(BF16)

    for hv, kinds in enumerate(tile_kinds):
        @pl.when(half == hv)
        def _(kinds=kinds):
            acc_next = _dot(h, w_ref[:, tile_cols(0)])
            for t, kind in enumerate(kinds):
                acc = acc_next
                if t + 1 < len(kinds):
                    acc_next = _dot(h, w_ref[:, tile_cols(t + 1)])
                if kind == "plain":
                    store_plain(t, acc)
                else:
                    store_head_normed(t, acc, 0 if kind == "q" else 1)


def _in_proj(x2, g, w_main, w_dt, dt_bias, qk_gain, bd):
    m = x2.shape[0]
    half_cols = PROJ_COLS // IN_HALVES
    tiles = half_cols // IN_TN
    kinds = {OFF_Q // IN_TN: "q", OFF_K // IN_TN: "k"}
    tile_kinds = tuple(tuple(kinds.get(hv * tiles + t, "plain") for t in range(tiles))
                       for hv in range(IN_HALVES))
    kern = functools.partial(_in_proj_kernel, tile_kinds=tile_kinds)
    return pl.pallas_call(
        kern,
        grid=(m // IN_TM, IN_HALVES),
        in_specs=[
            pl.BlockSpec((IN_TM, D_MODEL), lambda i, hv: (i, 0)),
            pl.BlockSpec((1, D_MODEL), lambda i, hv: (0, 0)),
            pl.BlockSpec((D_MODEL, half_cols), lambda i, hv: (0, hv)),
            pl.BlockSpec((D_MODEL, LANES), lambda i, hv: (0, 0)),
            pl.BlockSpec((1, LANES), lambda i, hv: (0, 0)),
            pl.BlockSpec(qk_gain.shape, lambda i, hv: (0, 0, 0)),
            pl.BlockSpec(bd.shape, lambda i, hv: (0, 0)),
        ],
        out_specs=[
            pl.BlockSpec((IN_TM, half_cols), lambda i, hv: (i, hv)),
            pl.BlockSpec((IN_TM, LANES), lambda i, hv: (i, 0)),
        ],
        out_shape=[
            jax.ShapeDtypeStruct((m, PROJ_COLS), BF16),
            jax.ShapeDtypeStruct((m, LANES), F32),
        ],
        scratch_shapes=[pltpu.VMEM((IN_TM, D_MODEL), BF16)],
        compiler_params=pltpu.CompilerParams(
            dimension_semantics=("parallel", "arbitrary"), vmem_limit_bytes=VMEM_LIMIT),
        name="in_proj",
    )(x2, g, w_main, w_dt, dt_bias, qk_gain, bd)


def _bias_kernel(rel_ref, lq1_ref, lk1_ref, lq2_ref, lk2_ref, bias_ref, far_ref, lam_ref, *, tile):
    h = pl.program_id(0)
    row = lax.broadcasted_iota(jnp.int32, (tile, tile), 0)
    col = lax.broadcasted_iota(jnp.int32, (tile, tile), 1)
    max_exact = NUM_BUCKETS // 2
    for t in range(2):
        d = row - col + t * tile
        n = jnp.maximum(d, 0)
        nf = jnp.maximum(n, 1).astype(F32)
        large = max_exact + (jnp.log(nf / max_exact) / math.log(MAX_DISTANCE / max_exact)
                             * (NUM_BUCKETS - max_exact)).astype(jnp.int32)
        large = jnp.minimum(large, NUM_BUCKETS - 1)
        bucket = jnp.where(n < max_exact, n, large)
        bias = jnp.zeros((tile, tile), F32)
        for bk in range(NUM_BUCKETS):
            bias = jnp.where(bucket == bk, rel_ref[bk, h], bias)
        bias = bias * LOG2E
        if t == 0:
            bias = jnp.where(d >= 0, bias, NEG_BIG)
        bias_ref[0, t] = bias
    far_ref[...] = jnp.full(far_ref.shape, rel_ref[NUM_BUCKETS - 1, h] * LOG2E, F32)
    s1 = jnp.sum(lq1_ref[...] * lk1_ref[...], axis=-1, keepdims=True)
    s2 = jnp.sum(lq2_ref[...] * lk2_ref[...], axis=-1, keepdims=True)
    lam = jnp.exp(s1) - jnp.exp(s2) + LAM_INIT
    lam_ref[...] = jnp.broadcast_to(lam, lam_ref.shape)


def _bias_build(rel_bias, lq1, lk1, lq2, lk2, tile):
    vec = pl.BlockSpec((1, ATTN_HEAD_DIM), lambda h: (0, 0))
    return pl.pallas_call(
        functools.partial(_bias_kernel, tile=tile),
        grid=(ATTN_HEADS,),
        in_specs=[pl.BlockSpec(memory_space=pltpu.SMEM), vec, vec, vec, vec],
        out_specs=[
            pl.BlockSpec((1, 2, tile, tile), lambda h: (h, 0, 0, 0)),
            pl.BlockSpec((1, SUBLANES, LANES), lambda h: (h, 0, 0)),
            pl.BlockSpec((SUBLANES, LANES), lambda h: (0, 0)),
        ],
        out_shape=[
            jax.ShapeDtypeStruct((ATTN_HEADS, 2, tile, tile), F32),
            jax.ShapeDtypeStruct((ATTN_HEADS, SUBLANES, LANES), F32),
            jax.ShapeDtypeStruct((SUBLANES, LANES), F32),
        ],
        compiler_params=pltpu.CompilerParams(dimension_semantics=("arbitrary",)),
        name="bias_build",
    )(rel_bias, lq1, lk1, lq2, lk2)


def _attn_kernel(lam_ref, far_ref, q_ref, k_ref, v_ref, bias_ref, g_ref, o_ref, vext_scr,
                 *, tile, nq):
    h = pl.program_id(1)
    lam = lam_ref[0]
    far = far_ref[h]
    lane = lax.broadcasted_iota(jnp.int32, (tile, LANES), 1)
    lo_half = lane < ATTN_HEAD_DIM
    gain = g_ref[...]

    vals = v_ref[0]
    vext_scr[...] = jnp.concatenate([vals, jnp.ones_like(vals)], axis=1)

    def scores(qi):
        q = q_ref[0, qi * tile:(qi + 1) * tile, :]
        zero = jnp.zeros_like(q)
        q_both = jnp.concatenate([jnp.where(lo_half, q, zero), jnp.where(lo_half, zero, q)], axis=0)
        return _dot_nt(q_both, k_ref[0, 0:(qi + 1) * tile, :])

    def softmax(qi, s):
        nk = (qi + 1) * tile
        n_far = nk - min(qi + 1, 2) * tile
        if qi == 0:
            near_bias = bias_ref[0, 0]
        else:
            near_bias = jnp.concatenate([bias_ref[0, 1], bias_ref[0, 0]], axis=1)
        s_near = s[:, n_far:] + jnp.concatenate([near_bias, near_bias], axis=0)
        m = jnp.max(s_near, axis=-1, keepdims=True)
        if n_far:
            s_far = s[:, :n_far]
            m = jnp.maximum(m, jnp.max(s_far, axis=-1, keepdims=True) + far)
            p = jnp.concatenate([jnp.exp2(s_far - (m - far)), jnp.exp2(s_near - m)], axis=1)
        else:
            p = jnp.exp2(s_near - m)
        return p.astype(BF16)

    def weighted_values(qi, p):
        pv = _dot(p, vext_scr[0:(qi + 1) * tile, :])
        return pv[:, :LANES] / pv[:, LANES:]

    units = list(reversed(range(nq)))
    n_units = len(units)
    s_q, p_q = {}, {}
    lag_s, lag_v = ATTN_LAGS
    for step in range(n_units + lag_v):
        if step < n_units:
            s_q[step] = scores(units[step])
        if 0 <= step - lag_s < n_units:
            p_q[step - lag_s] = softmax(units[step - lag_s], s_q.pop(step - lag_s))
        if 0 <= step - lag_v < n_units:
            qi = units[step - lag_v]
            both = weighted_values(qi, p_q.pop(step - lag_v))
            o = both[:tile] - lam * both[tile:]
            ms = jnp.mean(o * o, axis=-1, keepdims=True)
            o = o * lax.rsqrt(ms + SUBLN_EPS) * gain
            o_ref[0, qi * tile:(qi + 1) * tile, :] = o.astype(o_ref.dtype)


def _diff_attn(lam, far, proj3, bias, gain):
    b, s, _ = proj3.shape
    tile = bias.shape[-1]
    nq = s // tile
    qb, kb, vb = OFF_Q // LANES, OFF_K // LANES, OFF_V // LANES
    kern = functools.partial(_attn_kernel, tile=tile, nq=nq)
    return pl.pallas_call(
        kern,
        grid=(b, ATTN_HEADS),
        in_specs=[
            pl.BlockSpec(memory_space=pltpu.SMEM),
            pl.BlockSpec(memory_space=pltpu.SMEM),
            pl.BlockSpec((1, s, LANES), lambda i, h: (i, 0, qb + h)),
            pl.BlockSpec((1, s, LANES), lambda i, h: (i, 0, kb + h)),
            pl.BlockSpec((1, s, LANES), lambda i, h: (i, 0, vb + h)),
            pl.BlockSpec((1, 2, tile, tile), lambda i, h: (h, 0, 0, 0)),
            pl.BlockSpec((1, LANES), lambda i, h: (0, 0)),
        ],
        out_specs=pl.BlockSpec((1, s, LANES), lambda i, h: (i, 0, h)),
        out_shape=jax.ShapeDtypeStruct((b, s, V_COLS), BF16),
        scratch_shapes=[pltpu.VMEM((s, 2 * LANES), BF16)],
        compiler_params=pltpu.CompilerParams(
            dimension_semantics=("parallel", "parallel"), vmem_limit_bytes=VMEM_LIMIT),
        name="diff_attn",
    )(lam, far, proj3, proj3, proj3, bias, gain)


def _shift_matrices(taps, length, halo_rows):
    def build(rows, cols, offset):
        r = lax.broadcasted_iota(jnp.int32, (taps * rows, cols), 0)
        j = lax.broadcasted_iota(jnp.int32, (taps * rows, cols), 1)
        tap = r // rows
        return (j == offset + (r - tap * rows) - (taps - 1 - tap)).astype(BF16)

    return build(length, length, 0), build(halo_rows, halo_rows, halo_rows)


def _conv_silu(main_ref, halo_ref, w_ref, b_ref, shifts, first, length, cols=slice(None)):
    shift_main, shift_head = shifts
    hrows = halo_ref.shape[1]
    halo = halo_ref[0, :, cols]
    halo = jnp.where(first, jnp.zeros_like(halo), halo)
    shifted = _dot(shift_main, main_ref[0, :, cols])
    head = _dot(shift_head, halo)
    acc = b_ref[:, cols]
    for k in range(w_ref.shape[0]):
        tap = shifted[k * length:(k + 1) * length]
        tap = jnp.concatenate([tap[0:hrows] + head[k * hrows:(k + 1) * hrows], tap[hrows:]], axis=0)
        acc = acc + w_ref[k:k + 1, cols] * tap
    return _silu(acc)


def _ssd_kernel(xs_ref, xh_ref, bm_ref, bh_ref, cm_ref, ch_ref, z_ref, dt_ref, alog_ref,
                wx_ref, bx_ref, wb_ref, bb_ref, wc_ref, bc_ref, dskip_ref, ng_ref, e_ref,
                y_ref, state_scr, *, length):
    c = pl.program_id(1)
    first = c == 0

    @pl.when(first)
    def _():
        state_scr[...] = jnp.zeros_like(state_scr)

    shift = _shift_matrices(SSM_CONV, length, BF16_SUBLANES)
    bm = _conv_silu(bm_ref, bh_ref, wb_ref, bb_ref, shift, first, length).astype(BF16)
    cm = _conv_silu(cm_ref, ch_ref, wc_ref, bc_ref, shift, first, length).astype(BF16)

    def conv_x(g):
        cols = slice(g * GROUP_COLS, (g + 1) * GROUP_COLS)
        return _conv_silu(xs_ref, xh_ref, wx_ref, bx_ref, shift, first, length, cols)

    xs_all = [conv_x(g) for g in range(SSM_GROUPS)]

    dt = dt_ref[0]
    a = -jnp.exp(alog_ref[...])
    adt = dt * a
    row = lax.broadcasted_iota(jnp.int32, (length, length), 0)
    col = lax.broadcasted_iota(jnp.int32, (length, length), 1)
    causal = row >= col
    tri = causal.astype(BF16)
    cum = _dot_exact_lhs(tri, _split(adt, 3))
    cum2 = cum * LOG2E
    src_t = cum2.T - jnp.log2(dt.T)
    total = cum[length - 1:length, :]
    sel2 = e_ref[...]

    def expand(v):
        return _dot(jnp.concatenate(_split(v, 2), axis=1), sel2)

    ecum_x = expand(jnp.exp(cum))
    edtw_x = expand(dt * jnp.exp(total - cum))
    etot_x = ecum_x[length - 1:length, :]

    lane = lax.broadcasted_iota(jnp.int32, (length, LANES), 1)
    lo_half = lane < SSM_HEAD_DIM

    for g in range(SSM_GROUPS):
        gs = slice(g * SSM_STATE, (g + 1) * SSM_STATE)
        cs = slice(g * GROUP_COLS, (g + 1) * GROUP_COLS)
        xs_f = xs_all[g]
        b_g = bm[:, gs]
        c_g = cm[:, gs]
        xs_g = xs_f.astype(BF16)
        cb = _dot_nt(c_g, b_g)
        st = state_scr[g]
        y_g = _dot(c_g, st.astype(BF16)) * ecum_x[:, cs]
        xw = (xs_f * edtw_x[:, cs]).astype(BF16)
        b_t = b_g.astype(F32).T.astype(BF16)
        state_scr[g] = st * etot_x[:, cs] + _dot(b_t, xw)
        y_parts = []
        for pair in range(SSM_HEADS_PER_GROUP // 2):
            ms = []
            for hh in range(2):
                h = g * SSM_HEADS_PER_GROUP + pair * 2 + hh
                seg = jnp.where(causal, cum2[:, h:h + 1] - src_t[h:h + 1, :], NEG_BIG)
                ms.append((cb * jnp.exp2(seg)).astype(BF16))
            x_pair = xs_g[:, pair * LANES:(pair + 1) * LANES]
            zero = jnp.zeros_like(x_pair)
            rhs = jnp.concatenate([jnp.where(lo_half, x_pair, zero),
                                   jnp.where(lo_half, zero, x_pair)], axis=0)
            y_parts.append(_dot(jnp.concatenate(ms, axis=1), rhs))
        y_g = y_g + jnp.concatenate(y_parts, axis=1)
        y_g = y_g + xs_f * dskip_ref[:, cs]
        y_g = y_g * _silu(z_ref[0, :, cs].astype(F32))
        msq = jnp.mean(y_g * y_g, axis=-1, keepdims=True)
        y_g = y_g * lax.rsqrt(msq + SUBLN_EPS) * ng_ref[:, cs]
        y_ref[0, :, cs] = y_g.astype(y_ref.dtype)


def _ssd(proj3, dt3, alog, wx, bx, wb, bb, wc, bc, dskip_x, ng, sel):
    b, s, _ = proj3.shape
    length = SSD_L
    nc = s // length
    hb = length // BF16_SUBLANES
    xs_blk, z_blk = OFF_XS // SSM_D_INNER, OFF_Z // SSM_D_INNER
    b_blk, c_blk = OFF_B // BC_COLS, OFF_C // BC_COLS

    def main(width, blk):
        return pl.BlockSpec((1, length, width), lambda i, c: (i, c, blk))

    def halo(width, blk):
        return pl.BlockSpec((1, BF16_SUBLANES, width),
                            lambda i, c: (i, jnp.maximum(c * hb - 1, 0), blk))

    def const(shape):
        return pl.BlockSpec(shape, lambda i, c: (0,) * len(shape))

    kern = functools.partial(_ssd_kernel, length=length)
    return pl.pallas_call(
        kern,
        grid=(b, nc),
        in_specs=[
            main(SSM_D_INNER, xs_blk), halo(SSM_D_INNER, xs_blk),
            main(BC_COLS, b_blk), halo(BC_COLS, b_blk),
            main(BC_COLS, c_blk), halo(BC_COLS, c_blk),
            main(SSM_D_INNER, z_blk),
            pl.BlockSpec((1, length, LANES), lambda i, c: (i, c, 0)),
            const((1, LANES)),
            const((SSM_CONV, SSM_D_INNER)), const((1, SSM_D_INNER)),
            const((SSM_CONV, BC_COLS)), const((1, BC_COLS)),
            const((SSM_CONV, BC_COLS)), const((1, BC_COLS)),
            const((1, SSM_D_INNER)), const((1, SSM_D_INNER)),
            const((2 * LANES, SSM_D_INNER)),
        ],
        out_specs=pl.BlockSpec((1, length, SSM_D_INNER), lambda i, c: (i, c, 0)),
        out_shape=jax.ShapeDtypeStruct((b, s, SSM_D_INNER), BF16),
        scratch_shapes=[pltpu.VMEM((SSM_GROUPS, SSM_STATE, GROUP_COLS), F32)],
        compiler_params=pltpu.CompilerParams(
            dimension_semantics=("parallel", "arbitrary"), vmem_limit_bytes=VMEM_LIMIT),
        name="ssd",
    )(proj3, proj3, proj3, proj3, proj3, proj3, proj3, dt3, alog,
      wx, bx, wb, bb, wc, bc, dskip_x, ng, sel)


def _merge_kernel(x_ref, ya_ref, ys_ref, ga_ref, gs_ref, wpa_ref, wps_ref, wo_ref, g_ref,
                  x1_ref, h2_ref):
    pa = _dot(ya_ref[...], wpa_ref[...])
    ps = _dot(ys_ref[...], wps_ref[...])
    ga = 1.0 / (1.0 + jnp.exp(-ga_ref[...].astype(F32)))
    gs = 1.0 / (1.0 + jnp.exp(-gs_ref[...].astype(F32)))
    mixed = (ga * pa + gs * ps).astype(BF16)
    x1 = x_ref[...] + _dot(mixed, wo_ref[...])
    x1_ref[...] = x1
    ms = jnp.mean(x1 * x1, axis=-1, keepdims=True)
    h2_ref[...] = (x1 * lax.rsqrt(ms + RMS_EPS) * g_ref[...]).astype(BF16)


def _merge(x2, y_attn, y_ssm, proj, wpa, wps, wo, g):
    m = x2.shape[0]
    tm = MERGE_TM
    ga_blk = OFF_GATE // D_MODEL

    def rows(width, blk=0):
        return pl.BlockSpec((tm, width), lambda i: (i, blk))

    def const(shape):
        return pl.BlockSpec(shape, lambda i: (0, 0))

    return pl.pallas_call(
        _merge_kernel,
        grid=(m // tm,),
        in_specs=[
            rows(D_MODEL), rows(V_COLS), rows(SSM_D_INNER),
            rows(D_MODEL, ga_blk), rows(D_MODEL, ga_blk + 1),
            const((V_COLS, D_MODEL)), const((SSM_D_INNER, D_MODEL)), const((D_MODEL, D_MODEL)),
            const((1, D_MODEL)),
        ],
        out_specs=[rows(D_MODEL), rows(D_MODEL)],
        out_shape=[jax.ShapeDtypeStruct((m, D_MODEL), F32),
                   jax.ShapeDtypeStruct((m, D_MODEL), BF16)],
        compiler_params=pltpu.CompilerParams(
            dimension_semantics=("parallel",), vmem_limit_bytes=VMEM_LIMIT),
        name="merge",
    )(x2, y_attn, y_ssm, proj, proj, wpa, wps, wo, g)


def _ffn_kernel(x1_ref, h_ref, hh_ref, wup_ref, cw_ref, cb_ref, wd_ref, o_ref, gscr, vscr,
                *, tm, tiles_per_seq):
    i = pl.program_id(0)
    first = (i % tiles_per_seq) == 0
    halo = hh_ref[...]
    halo = jnp.where(first, jnp.zeros_like(halo), halo)
    hrows = halo.shape[0]
    h_ext = jnp.concatenate([halo, h_ref[...]], axis=0)
    taps = cw_ref.shape[0]

    slots = gscr.shape[0]

    def up_project(idx):
        start, width = FFN_SPLITS[idx]
        for col0, scr in ((start, gscr), (D_FF + start, vscr)):
            scr[idx % slots, :, 0:width] = _dot(h_ext, wup_ref[:, col0:col0 + width])

    def conv(idx, col0, scr):
        width = FFN_SPLITS[idx][1]
        cols = slice(col0, col0 + width)
        slot = idx % slots
        out = cb_ref[:, cols] + cw_ref[taps - 1:taps, cols] * scr[slot, hrows:hrows + tm, 0:width]
        for k in range(taps - 1):
            off = hrows - (taps - 1) + k
            out = out + cw_ref[k:k + 1, cols] * scr[slot, off:off + tm, 0:width]
        return out

    def gated(idx):
        start = FFN_SPLITS[idx][0]
        return (_silu(conv(idx, start, gscr)) * conv(idx, D_FF + start, vscr)).astype(BF16)

    n_chunks = len(FFN_SPLITS)
    ahead = slots - 1
    acc = x1_ref[...]
    acts = {}
    for idx in range(min(ahead, n_chunks)):
        up_project(idx)
    for idx in range(n_chunks + 1):
        if idx + ahead < n_chunks:
            up_project(idx + ahead)
        if idx < n_chunks:
            acts[idx] = gated(idx)
        if idx >= 1:
            start, width = FFN_SPLITS[idx - 1]
            acc = acc + _dot(acts.pop(idx - 1), wd_ref[start:start + width, :])

    o_ref[...] = acc


def _ffn(x1, h2, w_up, cw, cb, w_down, seq):
    m = x1.shape[0]
    tm = FFN_TM
    hb = tm // BF16_SUBLANES
    kern = functools.partial(_ffn_kernel, tm=tm, tiles_per_seq=seq // tm)

    def resident(shape):
        return pl.BlockSpec(shape, lambda i: (0, 0), pipeline_mode=pl.Buffered(1))

    return pl.pallas_call(
        kern,
        grid=(m // tm,),
        in_specs=[
            pl.BlockSpec((tm, D_MODEL), lambda i: (i, 0)),
            pl.BlockSpec((tm, D_MODEL), lambda i: (i, 0)),
            pl.BlockSpec((BF16_SUBLANES, D_MODEL), lambda i: (jnp.maximum(i * hb - 1, 0), 0)),
            resident((D_MODEL, 2 * D_FF)),
            resident((FFN_CONV, 2 * D_FF)),
            resident((1, 2 * D_FF)),
            resident((D_FF, D_MODEL)),
        ],
        out_specs=pl.BlockSpec((tm, D_MODEL), lambda i: (i, 0)),
        out_shape=jax.ShapeDtypeStruct((m, D_MODEL), F32),
        scratch_shapes=[
            pltpu.VMEM((FFN_SLOTS, tm + BF16_SUBLANES, FFN_SCR), F32),
            pltpu.VMEM((FFN_SLOTS, tm + BF16_SUBLANES, FFN_SCR), F32),
        ],
        compiler_params=pltpu.CompilerParams(
            dimension_semantics=("parallel",), vmem_limit_bytes=VMEM_LIMIT),
        name="ffn",
    )(x1, h2, h2, w_up, cw, cb, w_down)


def _block_diag_ones(width, chunk):
    r = lax.broadcasted_iota(jnp.int32, (width, width), 0) // chunk
    c = lax.broadcasted_iota(jnp.int32, (width, width), 1) // chunk
    return (r == c).astype(BF16)


def _head_expand(rows, cols, chunk):
    r = lax.broadcasted_iota(jnp.int32, (rows, cols), 0)
    c = lax.broadcasted_iota(jnp.int32, (rows, cols), 1) // chunk
    return (r == c).astype(BF16)


def _layer(x, rel_bias, norm_mix_g, w_in, q_norm_g, k_norm_g, lq1, lk1, lq2, lk2, attn_subln_g,
           conv_ssm_w, conv_ssm_b, dt_bias, a_log, d_skip, ssm_norm_g, w_proj_attn, w_proj_ssm,
           w_out, norm_ffn_g, w_up, conv_ffn_w, conv_ffn_b, w_down):
    b, s, d = x.shape
    m = b * s
    x2 = x.reshape(m, d)

    o_q, o_k, o_v = 0, Q_COLS, 2 * Q_COLS
    o_z = o_v + V_COLS
    o_xs = o_z + SSM_D_INNER
    o_b = o_xs + SSM_D_INNER
    o_c = o_b + BC_COLS
    o_dt = o_c + BC_COLS
    o_gate = o_dt + SSM_HEADS

    w_in = w_in.astype(BF16)

    def cols(start, width):
        return w_in[:, start:start + width]

    w_main = jnp.concatenate([
        cols(o_xs, SSM_D_INNER), cols(o_z, SSM_D_INNER), cols(o_gate, 2 * D_MODEL),
        cols(o_q, Q_COLS), cols(o_k, Q_COLS), cols(o_v, V_COLS),
        cols(o_b, BC_COLS), cols(o_c, BC_COLS)], axis=1)
    pad_h = LANES - SSM_HEADS
    w_dt = jnp.pad(cols(o_dt, SSM_HEADS), ((0, 0), (0, pad_h))).astype(BF16)
    dt_b = jnp.pad(dt_bias.astype(F32), (0, pad_h)).reshape(1, LANES)
    reps = Q_COLS // ATTN_HEAD_DIM
    qk_gain = jnp.stack([
        jnp.tile(q_norm_g.astype(F32) * (ATTN_HEAD_DIM ** -0.5 * LOG2E), reps),
        jnp.tile(k_norm_g.astype(F32), reps)]).reshape(2, 1, Q_COLS)
    bd = _block_diag_ones(MXU_WIDTH, ATTN_HEAD_DIM)

    proj, dt = _in_proj(x2, norm_mix_g.astype(F32).reshape(1, d), w_main, w_dt, dt_b, qk_gain, bd)
    proj3 = proj.reshape(b, s, PROJ_COLS)

    vec = lambda v: v.astype(F32).reshape(1, -1)
    bias, far, lam = _bias_build(rel_bias.astype(F32), vec(lq1), vec(lk1), vec(lq2), vec(lk2), ATTN_T)
    y_attn = _diff_attn(lam[0, :1], far[:, 0, 0], proj3, bias,
                        vec(attn_subln_g) * (1.0 - LAM_INIT))

    cw = conv_ssm_w.astype(F32)
    cbias = conv_ssm_b.astype(F32).reshape(1, -1)
    xs_sl = slice(0, SSM_D_INNER)
    b_sl = slice(SSM_D_INNER, SSM_D_INNER + BC_COLS)
    c_sl = slice(SSM_D_INNER + BC_COLS, SSM_D_INNER + 2 * BC_COLS)
    y_ssm = _ssd(
        proj3, dt.reshape(b, s, LANES),
        jnp.pad(a_log.astype(F32), (0, pad_h)).reshape(1, LANES),
        cw[:, xs_sl], cbias[:, xs_sl], cw[:, b_sl], cbias[:, b_sl], cw[:, c_sl], cbias[:, c_sl],
        jnp.repeat(d_skip.astype(F32), SSM_HEAD_DIM).reshape(1, SSM_D_INNER),
        vec(ssm_norm_g), jnp.tile(_head_expand(LANES, SSM_D_INNER, SSM_HEAD_DIM), (2, 1)))

    x1, h2 = _merge(x2, y_attn.reshape(m, V_COLS), y_ssm.reshape(m, SSM_D_INNER), proj,
                    w_proj_attn.astype(BF16), w_proj_ssm.astype(BF16), w_out.astype(BF16),
                    vec(norm_ffn_g))

    out = _ffn(x1, h2, w_up.astype(BF16), conv_ffn_w.astype(F32),
               conv_ffn_b.astype(F32).reshape(1, -1), w_down.astype(BF16), s)
    return out.reshape(b, s, d)


def kernel(x, rel_bias, norm_mix_g, w_in, q_norm_g, k_norm_g, lambda_q1, lambda_k1, lambda_q2,
           lambda_k2, attn_subln_g, conv_ssm_w, conv_ssm_b, dt_bias, a_log, d_skip, ssm_norm_g,
           w_proj_attn, w_proj_ssm, w_out, norm_ffn_g, w_up, conv_ffn_w, conv_ffn_b, w_down):
    depth = w_in.shape[0]
    assert depth == 1, "lambda_init and the stacked-parameter layout are specialised to depth 1"
    return _layer(x, rel_bias, norm_mix_g[0], w_in[0], q_norm_g[0], k_norm_g[0], lambda_q1[0],
                  lambda_k1[0], lambda_q2[0], lambda_k2[0], attn_subln_g[0], conv_ssm_w[0],
                  conv_ssm_b[0], dt_bias[0], a_log[0], d_skip[0], ssm_norm_g[0], w_proj_attn[0],
                  w_proj_ssm[0], w_out[0], norm_ffn_g[0], w_up[0], conv_ffn_w[0], conv_ffn_b[0],
                  w_down[0])
```

```python
import functools
import math

import jax
import jax.numpy as jnp
from jax import lax
from jax.experimental import pallas as pl
from jax.experimental.pallas import tpu as pltpu

F32 = jnp.float32
BF16 = jnp.bfloat16

D_MODEL = 1024
ATTN_HEADS = 8
ATTN_HEAD_DIM = 64
ATTN_V_DIM = 2 * ATTN_HEAD_DIM
NUM_BUCKETS = 32
MAX_DISTANCE = 128
SSM_D_INNER = 2048
SSM_HEAD_DIM = 64
SSM_HEADS = 32
SSM_GROUPS = 4
SSM_HEADS_PER_GROUP = SSM_HEADS // SSM_GROUPS
SSM_STATE = 128
SSM_CONV = 4
D_FF = 2816
FFN_CONV = 3
RMS_EPS = 1e-6
SUBLN_EPS = 1e-5
LAM_INIT = 0.8 - 0.6 * math.exp(0.0)

Q_COLS = ATTN_HEADS * 2 * ATTN_HEAD_DIM
V_COLS = ATTN_HEADS * ATTN_V_DIM
BC_COLS = SSM_GROUPS * SSM_STATE
GROUP_COLS = SSM_D_INNER // SSM_GROUPS

LANES = 128
SUBLANES = 8
BF16_SUBLANES = 16
MXU_WIDTH = 256
VMEM_LIMIT = 56 * 1024 * 1024

LOG2E = math.log2(math.e)
NEG_BIG = -1e30

W_SEGMENTS = (("q", Q_COLS), ("k", Q_COLS), ("v", V_COLS), ("z", SSM_D_INNER), ("xs", SSM_D_INNER),
              ("b", BC_COLS), ("c", BC_COLS), ("gate", 2 * D_MODEL))
OFF_Z = 0
OFF_Q = OFF_Z + SSM_D_INNER
OFF_K = OFF_Q + Q_COLS
OFF_V = OFF_K + Q_COLS
OFF_B = OFF_V + V_COLS
OFF_C = OFF_B + BC_COLS
OFF_XS = OFF_C + BC_COLS
OFF_GATE = OFF_XS + SSM_D_INNER
PROJ_COLS = OFF_GATE + 2 * D_MODEL
OUT_OFFSETS = {"q": OFF_Q, "k": OFF_K, "v": OFF_V, "z": OFF_Z, "xs": OFF_XS, "b": OFF_B,
               "c": OFF_C, "gate": OFF_GATE}

IN_TM = 512
IN_TN = 512
IN_HALVES = 2
ATTN_T = 128
ATTN_LAGS = (1, 3)
SSD_L = 128
MERGE_TM = 512
FFN_TM = 512
FFN_SPLITS = ((0, 512), (512, 512), (1024, 512), (1536, 512), (2048, 512), (2560, 256))
FFN_SCR = max(w for _, w in FFN_SPLITS)
FFN_SLOTS = 3


def _split(v, parts):
    out = []
    for _ in range(parts - 1):
        hi = v.astype(BF16)
        out.append(hi)
        v = v - hi.astype(F32)
    out.append(v.astype(BF16))
    return out


def _dot(a, b):
    return jnp.dot(a, b, preferred_element_type=F32)


def _dot_nt(a, b):
    return lax.dot_general(a, b, (((1,), (1,)), ((), ())), preferred_element_type=F32)


def _dot_exact_rhs(parts, sel):
    out = _dot(parts[0], sel)
    for p in parts[1:]:
        out = out + _dot(p, sel)
    return out


def _dot_exact_lhs(sel, parts):
    out = _dot(sel, parts[0])
    for p in parts[1:]:
        out = out + _dot(sel, p)
    return out


def _silu(v):
    h = 0.5 * v
    return h + h * jnp.tanh(h)


def _in_proj_kernel(x0_ref, xn_ref, g_ref, w_ref, wdt_ref, dtb_ref, qkg_ref, o_ref, dt_ref,
                    h_scr, *, tile_plan):
    i = pl.program_id(0)
    half = pl.program_id(1)
    slot = i % 2

    def normed(x):
        ms = jnp.mean(x * x, axis=-1, keepdims=True)
        return (x * lax.rsqrt(ms + RMS_EPS) * g_ref[...]).astype(BF16)

    @pl.when(jnp.logical_and(i == 0, half == 0))
    def _():
        h_scr[0] = normed(x0_ref[...])

    h = h_scr[slot]

    def tile_cols(t):
        return slice(t * IN_TN, (t + 1) * IN_TN)

    def store_plain(out0, acc):
        o_ref[:, out0:out0 + IN_TN] = acc.astype(BF16)

    def store_head_normed(out0, acc, gain_row, gain_col0):
        lo_half = lax.broadcasted_iota(jnp.int32, (IN_TM, LANES), 1) < ATTN_HEAD_DIM
        for cblk in range(IN_TN // LANES):
            a = acc[:, cblk * LANES:(cblk + 1) * LANES]
            sq = a * a
            s_lo = jnp.sum(jnp.where(lo_half, sq, 0.0), axis=-1, keepdims=True)
            s_hi = jnp.sum(jnp.where(lo_half, 0.0, sq), axis=-1, keepdims=True)
            ssum = jnp.where(lo_half, s_lo, s_hi)
            y = a * lax.rsqrt(ssum * (1.0 / ATTN_HEAD_DIM) + RMS_EPS)
            y = y * qkg_ref[gain_row, :, gain_col0 + cblk * LANES:gain_col0 + (cblk + 1) * LANES]
            o_ref[:, out0 + cblk * LANES:out0 + (cblk + 1) * LANES] = y.astype(BF16)

    for hv, plan in enumerate(tile_plan):
        @pl.when(half == hv)
        def _(plan=plan, hv=hv):
            acc_next = _dot(h, w_ref[:, tile_cols(0)])
            if hv == 0:
                t = _dot(h, wdt_ref[...]) + dtb_ref[...]
                dt_ref[...] = jnp.maximum(t, 0.0) + jnp.log1p(jnp.exp(-jnp.abs(t)))
            if hv == len(tile_plan) - 1:
                h_scr[1 - slot] = normed(xn_ref[...])
            for t, (out0, head_norm) in enumerate(plan):
                acc = acc_next
                if t + 1 < len(plan):
                    acc_next = _dot(h, w_ref[:, tile_cols(t + 1)])
                if head_norm is None:
                    store_plain(out0, acc)
                else:
                    store_head_normed(out0, acc, *head_norm)


def _in_proj_plan():
    half_cols = PROJ_COLS // IN_HALVES
    gain_rows = {"q": 0, "k": 1}
    plan = [[] for _ in range(IN_HALVES)]
    w_off = 0
    for name, width in W_SEGMENTS:
        for col in range(0, width, IN_TN):
            hv, out_col = (w_off + col) // half_cols, OUT_OFFSETS[name] + col
            assert out_col // half_cols == hv and width % IN_TN == 0
            head_norm = (gain_rows[name], col) if name in gain_rows else None
            plan[hv].append((out_col - hv * half_cols, head_norm))
        w_off += width
    return tuple(tuple(p) for p in plan)


def _in_proj(x2, g, w_main, w_dt, dt_bias, qk_gain):
    m = x2.shape[0]
    n_tiles = m // IN_TM
    half_cols = PROJ_COLS // IN_HALVES
    kern = functools.partial(_in_proj_kernel, tile_plan=_in_proj_plan())
    return pl.pallas_call(
        kern,
        grid=(m // IN_TM, IN_HALVES),
        in_specs=[
            pl.BlockSpec((IN_TM, D_MODEL), lambda i, hv: (0, 0)),
            pl.BlockSpec((IN_TM, D_MODEL), lambda i, hv: (jnp.minimum(i + 1, n_tiles - 1), 0)),
            pl.BlockSpec((1, D_MODEL), lambda i, hv: (0, 0)),
            pl.BlockSpec((D_MODEL, half_cols), lambda i, hv: (0, hv)),
            pl.BlockSpec((D_MODEL, LANES), lambda i, hv: (0, 0)),
            pl.BlockSpec((1, LANES), lambda i, hv: (0, 0)),
            pl.BlockSpec(qk_gain.shape, lambda i, hv: (0, 0, 0)),
        ],
        out_specs=[
            pl.BlockSpec((IN_TM, half_cols), lambda i, hv: (i, hv)),
            pl.BlockSpec((IN_TM, LANES), lambda i, hv: (i, 0)),
        ],
        out_shape=[
            jax.ShapeDtypeStruct((m, PROJ_COLS), BF16),
            jax.ShapeDtypeStruct((m, LANES), F32),
        ],
        scratch_shapes=[pltpu.VMEM((2, IN_TM, D_MODEL), BF16)],
        compiler_params=pltpu.CompilerParams(
            dimension_semantics=("arbitrary", "arbitrary"), vmem_limit_bytes=VMEM_LIMIT),
        name="in_proj",
    )(x2, x2, g, w_main, w_dt, dt_bias, qk_gain)


def _bias_kernel(rel_ref, lq1_ref, lk1_ref, lq2_ref, lk2_ref, bias_ref, far_ref, lam_ref, *, tile):
    h = pl.program_id(0)
    row = lax.broadcasted_iota(jnp.int32, (tile, tile), 0)
    col = lax.broadcasted_iota(jnp.int32, (tile, tile), 1)
    max_exact = NUM_BUCKETS // 2
    for t in range(2):
        d = row - col + t * tile
        n = jnp.maximum(d, 0)
        nf = jnp.maximum(n, 1).astype(F32)
        large = max_exact + (jnp.log(nf / max_exact) / math.log(MAX_DISTANCE / max_exact)
                             * (NUM_BUCKETS - max_exact)).astype(jnp.int32)
        large = jnp.minimum(large, NUM_BUCKETS - 1)
        bucket = jnp.where(n < max_exact, n, large)
        bias = jnp.zeros((tile, tile), F32)
        for bk in range(NUM_BUCKETS):
            bias = jnp.where(bucket == bk, rel_ref[bk, h], bias)
        bias = bias * LOG2E
        if t == 0:
            bias = jnp.where(d >= 0, bias, NEG_BIG)
        bias_ref[0, t] = bias
    far_ref[...] = jnp.full(far_ref.shape, rel_ref[NUM_BUCKETS - 1, h] * LOG2E, F32)
    s1 = jnp.sum(lq1_ref[...] * lk1_ref[...], axis=-1, keepdims=True)
    s2 = jnp.sum(lq2_ref[...] * lk2_ref[...], axis=-1, keepdims=True)
    lam = jnp.exp(s1) - jnp.exp(s2) + LAM_INIT
    lam_ref[...] = jnp.broadcast_to(lam, lam_ref.shape)


def _bias_build(rel_bias, lq1, lk1, lq2, lk2, tile):
    vec = pl.BlockSpec((1, ATTN_HEAD_DIM), lambda h: (0, 0))
    return pl.pallas_call(
        functools.partial(_bias_kernel, tile=tile),
        grid=(ATTN_HEADS,),
        in_specs=[pl.BlockSpec(memory_space=pltpu.SMEM), vec, vec, vec, vec],
        out_specs=[
            pl.BlockSpec((1, 2, tile, tile), lambda h: (h, 0, 0, 0)),
            pl.BlockSpec((1, SUBLANES, LANES), lambda h: (h, 0, 0)),
            pl.BlockSpec((SUBLANES, LANES), lambda h: (0, 0)),
        ],
        out_shape=[
            jax.ShapeDtypeStruct((ATTN_HEADS, 2, tile, tile), F32),
            jax.ShapeDtypeStruct((ATTN_HEADS, SUBLANES, LANES), F32),
            jax.ShapeDtypeStruct((SUBLANES, LANES), F32),
        ],
        compiler_params=pltpu.CompilerParams(dimension_semantics=("arbitrary",)),
        name="bias_build",
    )(rel_bias, lq1, lk1, lq2, lk2)


def _attn_kernel(lam_ref, far_ref, q_ref, k_ref, v_ref, bias_ref, g_ref, o_ref, vext_scr,
                 *, tile, nq):
    h = pl.program_id(1)
    lam = lam_ref[0]
    far = far_ref[h]
    lane = lax.broadcasted_iota(jnp.int32, (tile, LANES), 1)
    lo_half = lane < ATTN_HEAD_DIM
    gain = g_ref[...]

    vals = v_ref[0]
    vext_scr[...] = jnp.concatenate([vals, jnp.ones_like(vals)], axis=1)

    def scores(qi):
        q = q_ref[0, qi * tile:(qi + 1) * tile, :]
        zero = jnp.zeros_like(q)
        q_both = jnp.concatenate([jnp.where(lo_half, q, zero), jnp.where(lo_half, zero, q)], axis=0)
        return _dot_nt(q_both, k_ref[0, 0:(qi + 1) * tile, :])

    def softmax(qi, s):
        nk = (qi + 1) * tile
        n_far = nk - min(qi + 1, 2) * tile
        if qi == 0:
            near_bias = bias_ref[0, 0]
        else:
            near_bias = jnp.concatenate([bias_ref[0, 1], bias_ref[0, 0]], axis=1)
        s_near = s[:, n_far:] + jnp.concatenate([near_bias, near_bias], axis=0)
        m = jnp.max(s_near, axis=-1, keepdims=True)
        if n_far:
            s_far = s[:, :n_far]
            m = jnp.maximum(m, jnp.max(s_far, axis=-1, keepdims=True) + far)
            p = jnp.concatenate([jnp.exp2(s_far - (m - far)), jnp.exp2(s_near - m)], axis=1)
        else:
            p = jnp.exp2(s_near - m)
        return p.astype(BF16)

    def weighted_values(qi, p):
        pv = _dot(p, vext_scr[0:(qi + 1) * tile, :])
        return pv[:, :LANES] / pv[:, LANES:]

    units = list(reversed(range(nq)))
    n_units = len(units)
    s_q, p_q = {}, {}
    lag_s, lag_v = ATTN_LAGS
    for step in range(n_units + lag_v):
        if step < n_units:
            s_q[step] = scores(units[step])
        if 0 <= step - lag_s < n_units:
            p_q[step - lag_s] = softmax(units[step - lag_s], s_q.pop(step - lag_s))
        if 0 <= step - lag_v < n_units:
            qi = units[step - lag_v]
            both = weighted_values(qi, p_q.pop(step - lag_v))
            o = both[:tile] - lam * both[tile:]
            ms = jnp.mean(o * o, axis=-1, keepdims=True)
            o = o * lax.rsqrt(ms + SUBLN_EPS) * gain
            o_ref[0, qi * tile:(qi + 1) * tile, :] = o.astype(o_ref.dtype)


def _diff_attn(lam, far, proj3, bias, gain):
    b, s, _ = proj3.shape
    tile = bias.shape[-1]
    nq = s // tile
    qb, kb, vb = OFF_Q // LANES, OFF_K // LANES, OFF_V // LANES
    kern = functools.partial(_attn_kernel, tile=tile, nq=nq)
    return pl.pallas_call(
        kern,
        grid=(b, ATTN_HEADS),
        in_specs=[
            pl.BlockSpec(memory_space=pltpu.SMEM),
            pl.BlockSpec(memory_space=pltpu.SMEM),
            pl.BlockSpec((1, s, LANES), lambda i, h: (i, 0, qb + h)),
            pl.BlockSpec((1, s, LANES), lambda i, h: (i, 0, kb + h)),
            pl.BlockSpec((1, s, LANES), lambda i, h: (i, 0, vb + h)),
            pl.BlockSpec((1, 2, tile, tile), lambda i, h: (h, 0, 0, 0)),
            pl.BlockSpec((1, LANES), lambda i, h: (0, 0)),
        ],
        out_specs=pl.BlockSpec((1, s, LANES), lambda i, h: (i, 0, h)),
        out_shape=jax.ShapeDtypeStruct((b, s, V_COLS), BF16),
        scratch_shapes=[pltpu.VMEM((s, 2 * LANES), BF16)],
        compiler_params=pltpu.CompilerParams(
            dimension_semantics=("parallel", "parallel"), vmem_limit_bytes=VMEM_LIMIT),
        name="diff_attn",
    )(lam, far, proj3, proj3, proj3, bias, gain)


def _shift_matrices(taps, length, halo_rows):
    def build(rows, cols, offset):
        r = lax.broadcasted_iota(jnp.int32, (taps * rows, cols), 0)
        j = lax.broadcasted_iota(jnp.int32, (taps * rows, cols), 1)
        tap = r // rows
        return (j == offset + (r - tap * rows) - (taps - 1 - tap)).astype(BF16)

    return build(length, length, 0), build(halo_rows, halo_rows, halo_rows)


def _conv_silu(main_ref, halo_ref, w_ref, b_ref, shifts, first, length, cols=slice(None)):
    shift_main, shift_head = shifts
    hrows = halo_ref.shape[1]
    halo = halo_ref[0, :, cols]
    halo = jnp.where(first, jnp.zeros_like(halo), halo)
    shifted = _dot(shift_main, main_ref[0, :, cols])
    head = _dot(shift_head, halo)
    acc = b_ref[:, cols]
    for k in range(w_ref.shape[0]):
        tap = shifted[k * length:(k + 1) * length]
        tap = jnp.concatenate([tap[0:hrows] + head[k * hrows:(k + 1) * hrows], tap[hrows:]], axis=0)
        acc = acc + w_ref[k:k + 1, cols] * tap
    return _silu(acc)


def _ssd_kernel(xs_ref, xh_ref, bm_ref, bh_ref, cm_ref, ch_ref, z_ref, dt_ref, alog_ref,
                wx_ref, bx_ref, wb_ref, bb_ref, wc_ref, bc_ref, dskip_ref, ng_ref, e_ref,
                y_ref, state_scr, *, length):
    c = pl.program_id(1)
    first = c == 0

    @pl.when(first)
    def _():
        state_scr[...] = jnp.zeros_like(state_scr)

    shift = _shift_matrices(SSM_CONV, length, BF16_SUBLANES)
    bm = _conv_silu(bm_ref, bh_ref, wb_ref, bb_ref, shift, first, length).astype(BF16)
    cm = _conv_silu(cm_ref, ch_ref, wc_ref, bc_ref, shift, first, length).astype(BF16)

    def conv_x(g):
        cols = slice(g * GROUP_COLS, (g + 1) * GROUP_COLS)
        return _conv_silu(xs_ref, xh_ref, wx_ref, bx_ref, shift, first, length, cols)

    xs_all = [conv_x(g) for g in range(SSM_GROUPS)]

    dt = dt_ref[0]
    a = -jnp.exp(alog_ref[...])
    adt = dt * a
    row = lax.broadcasted_iota(jnp.int32, (length, length), 0)
    col = lax.broadcasted_iota(jnp.int32, (length, length), 1)
    causal = row >= col
    tri = causal.astype(BF16)
    cum = _dot_exact_lhs(tri, _split(adt, 3))
    cum2 = cum * LOG2E
    src_t = cum2.T - jnp.log2(dt.T)
    total = cum[length - 1:length, :]
    sel2 = e_ref[...]

    def expand(v):
        return _dot(jnp.concatenate(_split(v, 2), axis=1), sel2)

    ecum_x = expand(jnp.exp(cum))
    edtw_x = expand(dt * jnp.exp(total - cum))
    etot_x = ecum_x[length - 1:length, :]

    lane = lax.broadcasted_iota(jnp.int32, (length, LANES), 1)
    lo_half = lane < SSM_HEAD_DIM

    for g in range(SSM_GROUPS):
        gs = slice(g * SSM_STATE, (g + 1) * SSM_STATE)
        cs = slice(g * GROUP_COLS, (g + 1) * GROUP_COLS)
        xs_f = xs_all[g]
        b_g = bm[:, gs]
        c_g = cm[:, gs]
        xs_g = xs_f.astype(BF16)
        cb = _dot_nt(c_g, b_g)
        st = state_scr[g]
        y_g = _dot(c_g, st.astype(BF16)) * ecum_x[:, cs]
        xw = (xs_f * edtw_x[:, cs]).astype(BF16)
        b_t = b_g.astype(F32).T.astype(BF16)
        state_scr[g] = st * etot_x[:, cs] + _dot(b_t, xw)
        y_parts = []
        for pair in range(SSM_HEADS_PER_GROUP // 2):
            ms = []
            for hh in range(2):
                h = g * SSM_HEADS_PER_GROUP + pair * 2 + hh
                seg = jnp.where(causal, cum2[:, h:h + 1] - src_t[h:h + 1, :], NEG_BIG)
                ms.append((cb * jnp.exp2(seg)).astype(BF16))
            x_pair = xs_g[:, pair * LANES:(pair + 1) * LANES]
            zero = jnp.zeros_like(x_pair)
            rhs = jnp.concatenate([jnp.where(lo_half, x_pair, zero),
                                   jnp.where(lo_half, zero, x_pair)], axis=0)
            y_parts.append(_dot(jnp.concatenate(ms, axis=1), rhs))
        y_g = y_g + jnp.concatenate(y_parts, axis=1)
        y_g = y_g + xs_f * dskip_ref[:, cs]
        y_g = y_g * _silu(z_ref[0, :, cs].astype(F32))
        msq = jnp.mean(y_g * y_g, axis=-1, keepdims=True)
        y_g = y_g * lax.rsqrt(msq + SUBLN_EPS) * ng_ref[:, cs]
        y_ref[0, :, cs] = y_g.astype(y_ref.dtype)


def _ssd(proj3, dt3, alog, wx, bx, wb, bb, wc, bc, dskip_x, ng, sel):
    b, s, _ = proj3.shape
    length = SSD_L
    nc = s // length
    hb = length // BF16_SUBLANES
    xs_blk, z_blk = OFF_XS // SSM_D_INNER, OFF_Z // SSM_D_INNER
    b_blk, c_blk = OFF_B // BC_COLS, OFF_C // BC_COLS

    def main(width, blk):
        return pl.BlockSpec((1, length, width), lambda i, c: (i, c, blk))

    def halo(width, blk):
        return pl.BlockSpec((1, BF16_SUBLANES, width),
                            lambda i, c: (i, jnp.maximum(c * hb - 1, 0), blk))

    def const(shape):
        return pl.BlockSpec(shape, lambda i, c: (0,) * len(shape))

    kern = functools.partial(_ssd_kernel, length=length)
    return pl.pallas_call(
        kern,
        grid=(b, nc),
        in_specs=[
            main(SSM_D_INNER, xs_blk), halo(SSM_D_INNER, xs_blk),
            main(BC_COLS, b_blk), halo(BC_COLS, b_blk),
            main(BC_COLS, c_blk), halo(BC_COLS, c_blk),
            main(SSM_D_INNER, z_blk),
            pl.BlockSpec((1, length, LANES), lambda i, c: (i, c, 0)),
            const((1, LANES)),
            const((SSM_CONV, SSM_D_INNER)), const((1, SSM_D_INNER)),
            const((SSM_CONV, BC_COLS)), const((1, BC_COLS)),
            const((SSM_CONV, BC_COLS)), const((1, BC_COLS)),
            const((1, SSM_D_INNER)), const((1, SSM_D_INNER)),
            const((2 * LANES, SSM_D_INNER)),
        ],
        out_specs=pl.BlockSpec((1, length, SSM_D_INNER), lambda i, c: (i, c, 0)),
        out_shape=jax.ShapeDtypeStruct((b, s, SSM_D_INNER), BF16),
        scratch_shapes=[pltpu.VMEM((SSM_GROUPS, SSM_STATE, GROUP_COLS), F32)],
        compiler_params=pltpu.CompilerParams(
            dimension_semantics=("parallel", "arbitrary"), vmem_limit_bytes=VMEM_LIMIT),
        name="ssd",
    )(proj3, proj3, proj3, proj3, proj3, proj3, proj3, dt3, alog,
      wx, bx, wb, bb, wc, bc, dskip_x, ng, sel)


def _merge_kernel(x_ref, ya_ref, ys_ref, ga_ref, gs_ref, wpa_ref, wps_ref, wo_ref, g_ref,
                  x1_ref, h2_ref):
    pa = _dot(ya_ref[...], wpa_ref[...])
    ps = _dot(ys_ref[...], wps_ref[...])
    ga = 1.0 / (1.0 + jnp.exp(-ga_ref[...].astype(F32)))
    gs = 1.0 / (1.0 + jnp.exp(-gs_ref[...].astype(F32)))
    mixed = (ga * pa + gs * ps).astype(BF16)
    x1 = x_ref[...] + _dot(mixed, wo_ref[...])
    x1_ref[...] = x1
    ms = jnp.mean(x1 * x1, axis=-1, keepdims=True)
    h2_ref[...] = (x1 * lax.rsqrt(ms + RMS_EPS) * g_ref[...]).astype(BF16)


def _merge(x2, y_attn, y_ssm, proj, wpa, wps, wo, g):
    m = x2.shape[0]
    tm = MERGE_TM
    ga_blk = OFF_GATE // D_MODEL

    def rows(width, blk=0):
        return pl.BlockSpec((tm, width), lambda i: (i, blk))

    def const(shape):
        return pl.BlockSpec(shape, lambda i: (0, 0))

    return pl.pallas_call(
        _merge_kernel,
        grid=(m // tm,),
        in_specs=[
            rows(D_MODEL), rows(V_COLS), rows(SSM_D_INNER),
            rows(D_MODEL, ga_blk), rows(D_MODEL, ga_blk + 1),
            const((V_COLS, D_MODEL)), const((SSM_D_INNER, D_MODEL)), const((D_MODEL, D_MODEL)),
            const((1, D_MODEL)),
        ],
        out_specs=[rows(D_MODEL), rows(D_MODEL)],
        out_shape=[jax.ShapeDtypeStruct((m, D_MODEL), F32),
                   jax.ShapeDtypeStruct((m, D_MODEL), BF16)],
        compiler_params=pltpu.CompilerParams(
            dimension_semantics=("parallel",), vmem_limit_bytes=VMEM_LIMIT),
        name="merge",
    )(x2, y_attn, y_ssm, proj, proj, wpa, wps, wo, g)


def _ffn_kernel(x1_ref, h_ref, hh_ref, wup_ref, cw_ref, cb_ref, wd_ref, o_ref, gscr, vscr,
                *, tm, tiles_per_seq):
    i = pl.program_id(0)
    first = (i % tiles_per_seq) == 0
    halo = hh_ref[...]
    halo = jnp.where(first, jnp.zeros_like(halo), halo)
    hrows = halo.shape[0]
    h_ext = jnp.concatenate([halo, h_ref[...]], axis=0)
    taps = cw_ref.shape[0]

    slots = gscr.shape[0]

    def up_project(idx):
        start, width = FFN_SPLITS[idx]
        for col0, scr in ((start, gscr), (D_FF + start, vscr)):
            scr[idx % slots, :, 0:width] = _dot(h_ext, wup_ref[:, col0:col0 + width])

    def conv(idx, col0, scr):
        width = FFN_SPLITS[idx][1]
        cols = slice(col0, col0 + width)
        slot = idx % slots
        out = cb_ref[:, cols] + cw_ref[taps - 1:taps, cols] * scr[slot, hrows:hrows + tm, 0:width]
        for k in range(taps - 1):
            off = hrows - (taps - 1) + k
            out = out + cw_ref[k:k + 1, cols] * scr[slot, off:off + tm, 0:width]
        return out

    def gated(idx):
        start = FFN_SPLITS[idx][0]
        return (_silu(conv(idx, start, gscr)) * conv(idx, D_FF + start, vscr)).astype(BF16)

    n_chunks = len(FFN_SPLITS)
    ahead = slots - 1
    acc = x1_ref[...]
    acts = {}
    for idx in range(min(ahead, n_chunks)):
        up_project(idx)
    for idx in range(n_chunks + 1):
        if idx + ahead < n_chunks:
            up_project(idx + ahead)
        if idx < n_chunks:
            acts[idx] = gated(idx)
        if idx >= 1:
            start, width = FFN_SPLITS[idx - 1]
            acc = acc + _dot(acts.pop(idx - 1), wd_ref[start:start + width, :])

    o_ref[...] = acc


def _ffn(x1, h2, w_up, cw, cb, w_down, seq):
    m = x1.shape[0]
    tm = FFN_TM
    hb = tm // BF16_SUBLANES
    kern = functools.partial(_ffn_kernel, tm=tm, tiles_per_seq=seq // tm)

    def resident(shape):
        return pl.BlockSpec(shape, lambda i: (0, 0), pipeline_mode=pl.Buffered(1))

    return pl.pallas_call(
        kern,
        grid=(m // tm,),
        in_specs=[
            pl.BlockSpec((tm, D_MODEL), lambda i: (i, 0)),
            pl.BlockSpec((tm, D_MODEL), lambda i: (i, 0)),
            pl.BlockSpec((BF16_SUBLANES, D_MODEL), lambda i: (jnp.maximum(i * hb - 1, 0), 0)),
            resident((D_MODEL, 2 * D_FF)),
            resident((FFN_CONV, 2 * D_FF)),
            resident((1, 2 * D_FF)),
            resident((D_FF, D_MODEL)),
        ],
        out_specs=pl.BlockSpec((tm, D_MODEL), lambda i: (i, 0)),
        out_shape=jax.ShapeDtypeStruct((m, D_MODEL), F32),
        scratch_shapes=[
            pltpu.VMEM((FFN_SLOTS, tm + BF16_SUBLANES, FFN_SCR), F32),
            pltpu.VMEM((FFN_SLOTS, tm + BF16_SUBLANES, FFN_SCR), F32),
        ],
        compiler_params=pltpu.CompilerParams(
            dimension_semantics=("parallel",), vmem_limit_bytes=VMEM_LIMIT),
        name="ffn",
    )(x1, h2, h2, w_up, cw, cb, w_down)


def _head_expand(rows, cols, chunk):
    r = lax.broadcasted_iota(jnp.int32, (rows, cols), 0)
    c = lax.broadcasted_iota(jnp.int32, (rows, cols), 1) // chunk
    return (r == c).astype(BF16)


def _layer(x, rel_bias, norm_mix_g, w_in, q_norm_g, k_norm_g, lq1, lk1, lq2, lk2, attn_subln_g,
           conv_ssm_w, conv_ssm_b, dt_bias, a_log, d_skip, ssm_norm_g, w_proj_attn, w_proj_ssm,
           w_out, norm_ffn_g, w_up, conv_ffn_w, conv_ffn_b, w_down):
    b, s, d = x.shape
    m = b * s
    x2 = x.reshape(m, d)

    o_dt = sum(width for name, width in W_SEGMENTS if name != "gate")
    w_main = jnp.concatenate([w_in[:, :o_dt].astype(BF16),
                              w_in[:, o_dt + SSM_HEADS:].astype(BF16)], axis=1)
    pad_h = LANES - SSM_HEADS
    w_dt = jnp.pad(w_in[:, o_dt:o_dt + SSM_HEADS], ((0, 0), (0, pad_h))).astype(BF16)
    dt_b = jnp.pad(dt_bias.astype(F32), (0, pad_h)).reshape(1, LANES)
    reps = Q_COLS // ATTN_HEAD_DIM
    qk_gain = jnp.stack([
        jnp.tile(q_norm_g.astype(F32) * (ATTN_HEAD_DIM ** -0.5 * LOG2E), reps),
        jnp.tile(k_norm_g.astype(F32), reps)]).reshape(2, 1, Q_COLS)

    proj, dt = _in_proj(x2, norm_mix_g.astype(F32).reshape(1, d), w_main, w_dt, dt_b, qk_gain)
    proj3 = proj.reshape(b, s, PROJ_COLS)

    vec = lambda v: v.astype(F32).reshape(1, -1)
    bias, far, lam = _bias_build(rel_bias.astype(F32), vec(lq1), vec(lk1), vec(lq2), vec(lk2), ATTN_T)
    y_attn = _diff_attn(lam[0, :1], far[:, 0, 0], proj3, bias,
                        vec(attn_subln_g) * (1.0 - LAM_INIT))

    cw = conv_ssm_w.astype(F32)
    cbias = conv_ssm_b.astype(F32).reshape(1, -1)
    xs_sl = slice(0, SSM_D_INNER)
    b_sl = slice(SSM_D_INNER, SSM_D_INNER + BC_COLS)
    c_sl = slice(SSM_D_INNER + BC_COLS, SSM_D_INNER + 2 * BC_COLS)
    y_ssm = _ssd(
        proj3, dt.reshape(b, s, LANES),
        jnp.pad(a_log.astype(F32), (0, pad_h)).reshape(1, LANES),
        cw[:, xs_sl], cbias[:, xs_sl], cw[:, b_sl], cbias[:, b_sl], cw[:, c_sl], cbias[:, c_sl],
        jnp.repeat(d_skip.astype(F32), SSM_HEAD_DIM).reshape(1, SSM_D_INNER),
        vec(ssm_norm_g), jnp.tile(_head_expand(LANES, SSM_D_INNER, SSM_HEAD_DIM), (2, 1)))

    x1, h2 = _merge(x2, y_attn.reshape(m, V_COLS), y_ssm.reshape(m, SSM_D_INNER), proj,
                    w_proj_attn.astype(BF16), w_proj_ssm.astype(BF16), w_out.astype(BF16),
                    vec(norm_ffn_g))

    out = _ffn(x1, h2, w_up.astype(BF16), conv_ffn_w.astype(F32),
               conv_ffn_b.astype(F32).reshape(1, -1), w_down.astype(BF16), s)
    return out.reshape(b, s, d)


def kernel(x, rel_bias, norm_mix_g, w_in, q_norm_g, k_norm_g, lambda_q1, lambda_k1, lambda_q2,
           lambda_k2, attn_subln_g, conv_ssm_w, conv_ssm_b, dt_bias, a_log, d_skip, ssm_norm_g,
           w_proj_attn, w_proj_ssm, w_out, norm_ffn_g, w_up, conv_ffn_w, conv_ffn_b, w_down):
    depth = w_in.shape[0]
    assert depth == 1, "lambda_init and the stacked-parameter layout are specialised to depth 1"
    return _layer(x, rel_bias, norm_mix_g[0], w_in[0], q_norm_g[0], k_norm_g[0], lambda_q1[0],
                  lambda_k1[0], lambda_q2[0], lambda_k2[0], attn_subln_g[0], conv_ssm_w[0],
                  conv_ssm_b[0], dt_bias[0], a_log[0], d_skip[0], ssm_norm_g[0], w_proj_attn[0],
                  w_proj_ssm[0], w_out[0], norm_ffn_g[0], w_up[0], conv_ffn_w[0], conv_ffn_b[0],
                  w_down[0])
```

```python
import functools
import math

import jax
import jax.numpy as jnp
from jax import lax
from jax.experimental import pallas as pl
from jax.experimental.pallas import tpu as pltpu

F32 = jnp.float32
BF16 = jnp.bfloat16

D_MODEL = 1024
ATTN_HEADS = 8
ATTN_HEAD_DIM = 64
ATTN_V_DIM = 2 * ATTN_HEAD_DIM
NUM_BUCKETS = 32
MAX_DISTANCE = 128
SSM_D_INNER = 2048
SSM_HEAD_DIM = 64
SSM_HEADS = 32
SSM_GROUPS = 4
SSM_HEADS_PER_GROUP = SSM_HEADS // SSM_GROUPS
SSM_STATE = 128
SSM_CONV = 4
D_FF = 2816
FFN_CONV = 3
RMS_EPS = 1e-6
SUBLN_EPS = 1e-5
LAM_INIT = 0.8 - 0.6 * math.exp(0.0)

Q_COLS = ATTN_HEADS * 2 * ATTN_HEAD_DIM
V_COLS = ATTN_HEADS * ATTN_V_DIM
BC_COLS = SSM_GROUPS * SSM_STATE
GROUP_COLS = SSM_D_INNER // SSM_GROUPS

LANES = 128
SUBLANES = 8
BF16_SUBLANES = 16
MXU_WIDTH = 256
VMEM_LIMIT = 56 * 1024 * 1024

LOG2E = math.log2(math.e)
NEG_BIG = -1e30

W_SEGMENTS = (("q", Q_COLS), ("k", Q_COLS), ("v", V_COLS), ("z", SSM_D_INNER), ("xs", SSM_D_INNER),
              ("b", BC_COLS), ("c", BC_COLS), ("gate", 2 * D_MODEL))
OFF_Z = 0
OFF_Q = OFF_Z + SSM_D_INNER
OFF_K = OFF_Q + Q_COLS
OFF_V = OFF_K + Q_COLS
OFF_B = OFF_V + V_COLS
OFF_C = OFF_B + BC_COLS
OFF_XS = OFF_C + BC_COLS
OFF_GATE = OFF_XS + SSM_D_INNER
PROJ_COLS = OFF_GATE + 2 * D_MODEL
OUT_OFFSETS = {"q": OFF_Q, "k": OFF_K, "v": OFF_V, "z": OFF_Z, "xs": OFF_XS, "b": OFF_B,
               "c": OFF_C, "gate": OFF_GATE}

IN_TM = 512
IN_TN = 512
IN_HALVES = 2
ATTN_T = 128
ATTN_LAGS = (1, 3)
SSD_L = 128
SSD_SUB = 4
MERGE_TM = 512
FFN_TM = 512
FFN_SPLITS = ((0, 512), (512, 512), (1024, 512), (1536, 512), (2048, 512), (2560, 256))
FFN_SCR = max(w for _, w in FFN_SPLITS)
FFN_SLOTS = 3


def _split(v, parts):
    out = []
    for _ in range(parts - 1):
        hi = v.astype(BF16)
        out.append(hi)
        v = v - hi.astype(F32)
    out.append(v.astype(BF16))
    return out


def _dot(a, b):
    return jnp.dot(a, b, preferred_element_type=F32)


def _dot_nt(a, b):
    return lax.dot_general(a, b, (((1,), (1,)), ((), ())), preferred_element_type=F32)


def _dot_exact_rhs(parts, sel):
    out = _dot(parts[0], sel)
    for p in parts[1:]:
        out = out + _dot(p, sel)
    return out


def _dot_exact_lhs(sel, parts):
    out = _dot(sel, parts[0])
    for p in parts[1:]:
        out = out + _dot(sel, p)
    return out


def _silu(v):
    h = 0.5 * v
    return h + h * jnp.tanh(h)


def _in_proj_kernel(x0_ref, xn_ref, g_ref, w_ref, wdt_ref, dtb_ref, qkg_ref, o_ref, dt_ref,
                    h_scr, *, tile_plan):
    i = pl.program_id(0)
    half = pl.program_id(1)
    slot = i % 2

    def normed(x):
        ms = jnp.mean(x * x, axis=-1, keepdims=True)
        return (x * lax.rsqrt(ms + RMS_EPS) * g_ref[...]).astype(BF16)

    @pl.when(jnp.logical_and(i == 0, half == 0))
    def _():
        h_scr[0] = normed(x0_ref[...])

    h = h_scr[slot]

    def tile_cols(t):
        return slice(t * IN_TN, (t + 1) * IN_TN)

    def store_plain(out0, acc):
        o_ref[:, out0:out0 + IN_TN] = acc.astype(BF16)

    def store_head_normed(out0, acc, gain_row, gain_col0):
        lo_half = lax.broadcasted_iota(jnp.int32, (IN_TM, LANES), 1) < ATTN_HEAD_DIM
        for cblk in range(IN_TN // LANES):
            a = acc[:, cblk * LANES:(cblk + 1) * LANES]
            sq = a * a
            s_lo = jnp.sum(jnp.where(lo_half, sq, 0.0), axis=-1, keepdims=True)
            s_hi = jnp.sum(jnp.where(lo_half, 0.0, sq), axis=-1, keepdims=True)
            ssum = jnp.where(lo_half, s_lo, s_hi)
            y = a * lax.rsqrt(ssum * (1.0 / ATTN_HEAD_DIM) + RMS_EPS)
            y = y * qkg_ref[gain_row, :, gain_col0 + cblk * LANES:gain_col0 + (cblk + 1) * LANES]
            o_ref[:, out0 + cblk * LANES:out0 + (cblk + 1) * LANES] = y.astype(BF16)

    for hv, plan in enumerate(tile_plan):
        @pl.when(half == hv)
        def _(plan=plan, hv=hv):
            acc_next = _dot(h, w_ref[:, tile_cols(0)])
            if hv == 0:
                t = _dot(h, wdt_ref[...]) + dtb_ref[...]
                dt_ref[...] = jnp.maximum(t, 0.0) + jnp.log1p(jnp.exp(-jnp.abs(t)))
            if hv == len(tile_plan) - 1:
                h_scr[1 - slot] = normed(xn_ref[...])
            for t, (out0, head_norm) in enumerate(plan):
                acc = acc_next
                if t + 1 < len(plan):
                    acc_next = _dot(h, w_ref[:, tile_cols(t + 1)])
                if head_norm is None:
                    store_plain(out0, acc)
                else:
                    store_head_normed(out0, acc, *head_norm)


def _in_proj_plan():
    half_cols = PROJ_COLS // IN_HALVES
    gain_rows = {"q": 0, "k": 1}
    plan = [[] for _ in range(IN_HALVES)]
    w_off = 0
    for name, width in W_SEGMENTS:
        for col in range(0, width, IN_TN):
            hv, out_col = (w_off + col) // half_cols, OUT_OFFSETS[name] + col
            assert out_col // half_cols == hv and width % IN_TN == 0
            head_norm = (gain_rows[name], col) if name in gain_rows else None
            plan[hv].append((out_col - hv * half_cols, head_norm))
        w_off += width
    return tuple(tuple(p) for p in plan)


def _in_proj(x2, g, w_main, w_dt, dt_bias, qk_gain):
    m = x2.shape[0]
    n_tiles = m // IN_TM
    half_cols = PROJ_COLS // IN_HALVES
    kern = functools.partial(_in_proj_kernel, tile_plan=_in_proj_plan())
    return pl.pallas_call(
        kern,
        grid=(m // IN_TM, IN_HALVES),
        in_specs=[
            pl.BlockSpec((IN_TM, D_MODEL), lambda i, hv: (0, 0)),
            pl.BlockSpec((IN_TM, D_MODEL), lambda i, hv: (jnp.minimum(i + 1, n_tiles - 1), 0)),
            pl.BlockSpec((1, D_MODEL), lambda i, hv: (0, 0)),
            pl.BlockSpec((D_MODEL, half_cols), lambda i, hv: (0, hv)),
            pl.BlockSpec((D_MODEL, LANES), lambda i, hv: (0, 0)),
            pl.BlockSpec((1, LANES), lambda i, hv: (0, 0)),
            pl.BlockSpec(qk_gain.shape, lambda i, hv: (0, 0, 0)),
        ],
        out_specs=[
            pl.BlockSpec((IN_TM, half_cols), lambda i, hv: (i, hv)),
            pl.BlockSpec((IN_TM, LANES), lambda i, hv: (i, 0)),
        ],
        out_shape=[
            jax.ShapeDtypeStruct((m, PROJ_COLS), BF16),
            jax.ShapeDtypeStruct((m, LANES), F32),
        ],
        scratch_shapes=[pltpu.VMEM((2, IN_TM, D_MODEL), BF16)],
        compiler_params=pltpu.CompilerParams(
            dimension_semantics=("arbitrary", "arbitrary"), vmem_limit_bytes=VMEM_LIMIT),
        name="in_proj",
    )(x2, x2, g, w_main, w_dt, dt_bias, qk_gain)


def _bias_kernel(rel_ref, lq1_ref, lk1_ref, lq2_ref, lk2_ref, bias_ref, far_ref, lam_ref, *, tile):
    h = pl.program_id(0)
    row = lax.broadcasted_iota(jnp.int32, (tile, tile), 0)
    col = lax.broadcasted_iota(jnp.int32, (tile, tile), 1)
    max_exact = NUM_BUCKETS // 2
    for t in range(2):
        d = row - col + t * tile
        n = jnp.maximum(d, 0)
        nf = jnp.maximum(n, 1).astype(F32)
        large = max_exact + (jnp.log(nf / max_exact) / math.log(MAX_DISTANCE / max_exact)
                             * (NUM_BUCKETS - max_exact)).astype(jnp.int32)
        large = jnp.minimum(large, NUM_BUCKETS - 1)
        bucket = jnp.where(n < max_exact, n, large)
        bias = jnp.zeros((tile, tile), F32)
        for bk in range(NUM_BUCKETS):
            bias = jnp.where(bucket == bk, rel_ref[bk, h], bias)
        bias = bias * LOG2E
        if t == 0:
            bias = jnp.where(d >= 0, bias, NEG_BIG)
        bias_ref[0, t] = bias
    far_ref[...] = jnp.full(far_ref.shape, rel_ref[NUM_BUCKETS - 1, h] * LOG2E, F32)
    s1 = jnp.sum(lq1_ref[...] * lk1_ref[...], axis=-1, keepdims=True)
    s2 = jnp.sum(lq2_ref[...] * lk2_ref[...], axis=-1, keepdims=True)
    lam = jnp.exp(s1) - jnp.exp(s2) + LAM_INIT
    lam_ref[...] = jnp.broadcast_to(lam, lam_ref.shape)


def _bias_build(rel_bias, lq1, lk1, lq2, lk2, tile):
    vec = pl.BlockSpec((1, ATTN_HEAD_DIM), lambda h: (0, 0))
    return pl.pallas_call(
        functools.partial(_bias_kernel, tile=tile),
        grid=(ATTN_HEADS,),
        in_specs=[pl.BlockSpec(memory_space=pltpu.SMEM), vec, vec, vec, vec],
        out_specs=[
            pl.BlockSpec((1, 2, tile, tile), lambda h: (h, 0, 0, 0)),
            pl.BlockSpec((1, SUBLANES, LANES), lambda h: (h, 0, 0)),
            pl.BlockSpec((SUBLANES, LANES), lambda h: (0, 0)),
        ],
        out_shape=[
            jax.ShapeDtypeStruct((ATTN_HEADS, 2, tile, tile), F32),
            jax.ShapeDtypeStruct((ATTN_HEADS, SUBLANES, LANES), F32),
            jax.ShapeDtypeStruct((SUBLANES, LANES), F32),
        ],
        compiler_params=pltpu.CompilerParams(dimension_semantics=("arbitrary",)),
        name="bias_build",
    )(rel_bias, lq1, lk1, lq2, lk2)


def _attn_kernel(lam_ref, far_ref, q_ref, k_ref, v_ref, bias_ref, g_ref, o_ref, vext_scr,
                 *, tile, nq):
    h = pl.program_id(1)
    lam = lam_ref[0]
    far = far_ref[h]
    lane = lax.broadcasted_iota(jnp.int32, (tile, LANES), 1)
    lo_half = lane < ATTN_HEAD_DIM
    gain = g_ref[...]

    vals = v_ref[0]
    vext_scr[...] = jnp.concatenate([vals, jnp.ones_like(vals)], axis=1)

    def scores(qi):
        q = q_ref[0, qi * tile:(qi + 1) * tile, :]
        zero = jnp.zeros_like(q)
        q_both = jnp.concatenate([jnp.where(lo_half, q, zero), jnp.where(lo_half, zero, q)], axis=0)
        return _dot_nt(q_both, k_ref[0, 0:(qi + 1) * tile, :])

    def softmax(qi, s):
        nk = (qi + 1) * tile
        n_far = nk - min(qi + 1, 2) * tile
        if qi == 0:
            near_bias = bias_ref[0, 0]
        else:
            near_bias = jnp.concatenate([bias_ref[0, 1], bias_ref[0, 0]], axis=1)
        s_near = s[:, n_far:] + jnp.concatenate([near_bias, near_bias], axis=0)
        m = jnp.max(s_near, axis=-1, keepdims=True)
        if n_far:
            s_far = s[:, :n_far]
            m = jnp.maximum(m, jnp.max(s_far, axis=-1, keepdims=True) + far)
            p = jnp.concatenate([jnp.exp2(s_far - (m - far)), jnp.exp2(s_near - m)], axis=1)
        else:
            p = jnp.exp2(s_near - m)
        return p.astype(BF16)

    def weighted_values(qi, p):
        pv = _dot(p, vext_scr[0:(qi + 1) * tile, :])
        return pv[:, :LANES] / pv[:, LANES:]

    units = list(reversed(range(nq)))
    n_units = len(units)
    s_q, p_q = {}, {}
    lag_s, lag_v = ATTN_LAGS
    for step in range(n_units + lag_v):
        if step < n_units:
            s_q[step] = scores(units[step])
        if 0 <= step - lag_s < n_units:
            p_q[step - lag_s] = softmax(units[step - lag_s], s_q.pop(step - lag_s))
        if 0 <= step - lag_v < n_units:
            qi = units[step - lag_v]
            both = weighted_values(qi, p_q.pop(step - lag_v))
            o = both[:tile] - lam * both[tile:]
            ms = jnp.mean(o * o, axis=-1, keepdims=True)
            o = o * lax.rsqrt(ms + SUBLN_EPS) * gain
            o_ref[0, qi * tile:(qi + 1) * tile, :] = o.astype(o_ref.dtype)


def _diff_attn(lam, far, proj3, bias, gain):
    b, s, _ = proj3.shape
    tile = bias.shape[-1]
    nq = s // tile
    qb, kb, vb = OFF_Q // LANES, OFF_K // LANES, OFF_V // LANES
    kern = functools.partial(_attn_kernel, tile=tile, nq=nq)
    return pl.pallas_call(
        kern,
        grid=(b, ATTN_HEADS),
        in_specs=[
            pl.BlockSpec(memory_space=pltpu.SMEM),
            pl.BlockSpec(memory_space=pltpu.SMEM),
            pl.BlockSpec((1, s, LANES), lambda i, h: (i, 0, qb + h)),
            pl.BlockSpec((1, s, LANES), lambda i, h: (i, 0, kb + h)),
            pl.BlockSpec((1, s, LANES), lambda i, h: (i, 0, vb + h)),
            pl.BlockSpec((1, 2, tile, tile), lambda i, h: (h, 0, 0, 0)),
            pl.BlockSpec((1, LANES), lambda i, h: (0, 0)),
        ],
        out_specs=pl.BlockSpec((1, s, LANES), lambda i, h: (i, 0, h)),
        out_shape=jax.ShapeDtypeStruct((b, s, V_COLS), BF16),
        scratch_shapes=[pltpu.VMEM((s, 2 * LANES), BF16)],
        compiler_params=pltpu.CompilerParams(
            dimension_semantics=("parallel", "parallel"), vmem_limit_bytes=VMEM_LIMIT),
        name="diff_attn",
    )(lam, far, proj3, proj3, proj3, bias, gain)


def _shift_matrices(taps, length, halo_rows):
    def build(rows, cols, offset):
        r = lax.broadcasted_iota(jnp.int32, (taps * rows, cols), 0)
        j = lax.broadcasted_iota(jnp.int32, (taps * rows, cols), 1)
        tap = r // rows
        return (j == offset + (r - tap * rows) - (taps - 1 - tap)).astype(BF16)

    return build(length, length, 0), build(halo_rows, halo_rows, halo_rows)


def _conv_silu(x, halo, w_ref, b_ref, shifts, length, cols):
    shift_main, shift_head = shifts
    hrows = halo.shape[0]
    shifted = _dot(shift_main, x)
    head = _dot(shift_head, halo)
    acc = b_ref[:, cols]
    for k in range(w_ref.shape[0]):
        tap = shifted[k * length:(k + 1) * length]
        tap = jnp.concatenate([tap[0:hrows] + head[k * hrows:(k + 1) * hrows], tap[hrows:]], axis=0)
        acc = acc + w_ref[k:k + 1, cols] * tap
    return _silu(acc)


def _ssd_kernel(xs_ref, xh_ref, bm_ref, bh_ref, cm_ref, ch_ref, z_ref, dt_ref, alog_ref,
                wx_ref, bx_ref, wb_ref, bb_ref, wc_ref, bc_ref, dskip_ref, ng_ref, e_ref,
                y_ref, state_scr, *, length, sub_chunks):
    c = pl.program_id(1)
    first = c == 0

    @pl.when(first)
    def _():
        state_scr[...] = jnp.zeros_like(state_scr)

    shift = _shift_matrices(SSM_CONV, length, BF16_SUBLANES)
    a = -jnp.exp(alog_ref[...])
    row = lax.broadcasted_iota(jnp.int32, (length, length), 0)
    col = lax.broadcasted_iota(jnp.int32, (length, length), 1)
    causal = row >= col
    tri = causal.astype(BF16)
    sel2 = e_ref[...]
    lane = lax.broadcasted_iota(jnp.int32, (length, LANES), 1)
    lo_half = lane < SSM_HEAD_DIM

    def conv(k, main_ref, halo_ref, w_ref, b_ref, cols=slice(None)):
        r0 = k * length
        if k == 0:
            halo = halo_ref[0, :, cols]
            halo = jnp.where(first, jnp.zeros_like(halo), halo)
        else:
            halo = main_ref[0, r0 - BF16_SUBLANES:r0, cols]
        return _conv_silu(main_ref[0, r0:r0 + length, cols], halo, w_ref, b_ref, shift, length, cols)

    def expand(v):
        return _dot(jnp.concatenate(_split(v, 2), axis=1), sel2)

    def prepare(k):
        rows = slice(k * length, (k + 1) * length)
        bm = conv(k, bm_ref, bh_ref, wb_ref, bb_ref).astype(BF16)
        cm = conv(k, cm_ref, ch_ref, wc_ref, bc_ref).astype(BF16)
        xs = [conv(k, xs_ref, xh_ref, wx_ref, bx_ref, slice(g * GROUP_COLS, (g + 1) * GROUP_COLS))
              for g in range(SSM_GROUPS)]
        dt = dt_ref[0, rows, :]
        cum = _dot_exact_lhs(tri, _split(dt * a, 3))
        cum2 = cum * LOG2E
        src_t = cum2.T - jnp.log2(dt.T)
        total = cum[length - 1:length, :]
        ecum_x = expand(jnp.exp(cum))
        edtw_x = expand(dt * jnp.exp(total - cum))
        return bm, cm, xs, cum2, src_t, ecum_x, edtw_x

    def scan(k, prepared):
        bm, cm, xs, cum2, src_t, ecum_x, edtw_x = prepared
        rows = slice(k * length, (k + 1) * length)
        etot_x = ecum_x[length - 1:length, :]
        for g in range(SSM_GROUPS):
            gs = slice(g * SSM_STATE, (g + 1) * SSM_STATE)
            cs = slice(g * GROUP_COLS, (g + 1) * GROUP_COLS)
            xs_f = xs[g]
            b_g = bm[:, gs]
            c_g = cm[:, gs]
            xs_g = xs_f.astype(BF16)
            cb = _dot_nt(c_g, b_g)
            st = state_scr[g]
            y_g = _dot(c_g, st.astype(BF16)) * ecum_x[:, cs]
            xw = (xs_f * edtw_x[:, cs]).astype(BF16)
            b_t = b_g.astype(F32).T.astype(BF16)
            state_scr[g] = st * etot_x[:, cs] + _dot(b_t, xw)
            y_parts = []
            for pair in range(SSM_HEADS_PER_GROUP // 2):
                ms = []
                for hh in range(2):
                    h = g * SSM_HEADS_PER_GROUP + pair * 2 + hh
                    seg = jnp.where(causal, cum2[:, h:h + 1] - src_t[h:h + 1, :], NEG_BIG)
                    ms.append((cb * jnp.exp2(seg)).astype(BF16))
                x_pair = xs_g[:, pair * LANES:(pair + 1) * LANES]
                zero = jnp.zeros_like(x_pair)
                rhs = jnp.concatenate([jnp.where(lo_half, x_pair, zero),
                                       jnp.where(lo_half, zero, x_pair)], axis=0)
                y_parts.append(_dot(jnp.concatenate(ms, axis=1), rhs))
            y_g = y_g + jnp.concatenate(y_parts, axis=1)
            y_g = y_g + xs_f * dskip_ref[:, cs]
            y_g = y_g * _silu(z_ref[0, rows, cs].astype(F32))
            msq = jnp.mean(y_g * y_g, axis=-1, keepdims=True)
            y_g = y_g * lax.rsqrt(msq + SUBLN_EPS) * ng_ref[:, cs]
            y_ref[0, rows, cs] = y_g.astype(y_ref.dtype)

    prepared = prepare(0)
    for k in range(sub_chunks):
        current = prepared
        if k + 1 < sub_chunks:
            prepared = prepare(k + 1)
        scan(k, current)


def _ssd(proj3, dt3, alog, wx, bx, wb, bb, wc, bc, dskip_x, ng, sel):
    b, s, _ = proj3.shape
    length = SSD_L
    rows = SSD_SUB * length
    nc = s // rows
    hb = rows // BF16_SUBLANES
    xs_blk, z_blk = OFF_XS // SSM_D_INNER, OFF_Z // SSM_D_INNER
    b_blk, c_blk = OFF_B // BC_COLS, OFF_C // BC_COLS

    def main(width, blk):
        return pl.BlockSpec((1, rows, width), lambda i, c: (i, c, blk))

    def halo(width, blk):
        return pl.BlockSpec((1, BF16_SUBLANES, width),
                            lambda i, c: (i, jnp.maximum(c * hb - 1, 0), blk))

    def const(shape):
        return pl.BlockSpec(shape, lambda i, c: (0,) * len(shape))

    kern = functools.partial(_ssd_kernel, length=length, sub_chunks=SSD_SUB)
    return pl.pallas_call(
        kern,
        grid=(b, nc),
        in_specs=[
            main(SSM_D_INNER, xs_blk), halo(SSM_D_INNER, xs_blk),
            main(BC_COLS, b_blk), halo(BC_COLS, b_blk),
            main(BC_COLS, c_blk), halo(BC_COLS, c_blk),
            main(SSM_D_INNER, z_blk),
            pl.BlockSpec((1, rows, LANES), lambda i, c: (i, c, 0)),
            const((1, LANES)),
            const((SSM_CONV, SSM_D_INNER)), const((1, SSM_D_INNER)),
            const((SSM_CONV, BC_COLS)), const((1, BC_COLS)),
            const((SSM_CONV, BC_COLS)), const((1, BC_COLS)),
            const((1, SSM_D_INNER)), const((1, SSM_D_INNER)),
            const((2 * LANES, SSM_D_INNER)),
        ],
        out_specs=pl.BlockSpec((1, rows, SSM_D_INNER), lambda i, c: (i, c, 0)),
        out_shape=jax.ShapeDtypeStruct((b, s, SSM_D_INNER), BF16),
        scratch_shapes=[pltpu.VMEM((SSM_GROUPS, SSM_STATE, GROUP_COLS), F32)],
        compiler_params=pltpu.CompilerParams(
            dimension_semantics=("parallel", "arbitrary"), vmem_limit_bytes=VMEM_LIMIT),
        name="ssd",
    )(proj3, proj3, proj3, proj3, proj3, proj3, proj3, dt3, alog,
      wx, bx, wb, bb, wc, bc, dskip_x, ng, sel)


def _merge_kernel(x_ref, ya_ref, ys_ref, ga_ref, gs_ref, wpa_ref, wps_ref, wo_ref, g_ref,
                  x1_ref, h2_ref):
    pa = _dot(ya_ref[...], wpa_ref[...])
    ps = _dot(ys_ref[...], wps_ref[...])
    ga = 1.0 / (1.0 + jnp.exp(-ga_ref[...].astype(F32)))
    gs = 1.0 / (1.0 + jnp.exp(-gs_ref[...].astype(F32)))
    mixed = (ga * pa + gs * ps).astype(BF16)
    x1 = x_ref[...] + _dot(mixed, wo_ref[...])
    x1_ref[...] = x1
    ms = jnp.mean(x1 * x1, axis=-1, keepdims=True)
    h2_ref[...] = (x1 * lax.rsqrt(ms + RMS_EPS) * g_ref[...]).astype(BF16)


def _merge(x2, y_attn, y_ssm, proj, wpa, wps, wo, g):
    m = x2.shape[0]
    tm = MERGE_TM
    ga_blk = OFF_GATE // D_MODEL

    def rows(width, blk=0):
        return pl.BlockSpec((tm, width), lambda i: (i, blk))

    def const(shape):
        return pl.BlockSpec(shape, lambda i: (0, 0))

    return pl.pallas_call(
        _merge_kernel,
        grid=(m // tm,),
        in_specs=[
            rows(D_MODEL), rows(V_COLS), rows(SSM_D_INNER),
            rows(D_MODEL, ga_blk), rows(D_MODEL, ga_blk + 1),
            const((V_COLS, D_MODEL)), const((SSM_D_INNER, D_MODEL)), const((D_MODEL, D_MODEL)),
            const((1, D_MODEL)),
        ],
        out_specs=[rows(D_MODEL), rows(D_MODEL)],
        out_shape=[jax.ShapeDtypeStruct((m, D_MODEL), F32),
                   jax.ShapeDtypeStruct((m, D_MODEL), BF16)],
        compiler_params=pltpu.CompilerParams(
            dimension_semantics=("parallel",), vmem_limit_bytes=VMEM_LIMIT),
        name="merge",
    )(x2, y_attn, y_ssm, proj, proj, wpa, wps, wo, g)


def _ffn_kernel(x1_ref, h_ref, hh_ref, wup_ref, cw_ref, cb_ref, wd_ref, o_ref, gscr, vscr,
                *, tm, tiles_per_seq):
    i = pl.program_id(0)
    first = (i % tiles_per_seq) == 0
    halo = hh_ref[...]
    halo = jnp.where(first, jnp.zeros_like(halo), halo)
    hrows = halo.shape[0]
    h_ext = jnp.concatenate([halo, h_ref[...]], axis=0)
    taps = cw_ref.shape[0]

    slots = gscr.shape[0]

    def up_project(idx):
        start, width = FFN_SPLITS[idx]
        for col0, scr in ((start, gscr), (D_FF + start, vscr)):
            scr[idx % slots, :, 0:width] = _dot(h_ext, wup_ref[:, col0:col0 + width])

    def conv(idx, col0, scr):
        width = FFN_SPLITS[idx][1]
        cols = slice(col0, col0 + width)
        slot = idx % slots
        out = cb_ref[:, cols] + cw_ref[taps - 1:taps, cols] * scr[slot, hrows:hrows + tm, 0:width]
        for k in range(taps - 1):
            off = hrows - (taps - 1) + k
            out = out + cw_ref[k:k + 1, cols] * scr[slot, off:off + tm, 0:width]
        return out

    def gated(idx):
        start = FFN_SPLITS[idx][0]
        return (_silu(conv(idx, start, gscr)) * conv(idx, D_FF + start, vscr)).astype(BF16)

    n_chunks = len(FFN_SPLITS)
    ahead = slots - 1
    acc = x1_ref[...]
    acts = {}
    for idx in range(min(ahead, n_chunks)):
        up_project(idx)
    for idx in range(n_chunks + 1):
        if idx + ahead < n_chunks:
            up_project(idx + ahead)
        if idx < n_chunks:
            acts[idx] = gated(idx)
        if idx >= 1:
            start, width = FFN_SPLITS[idx - 1]
            acc = acc + _dot(acts.pop(idx - 1), wd_ref[start:start + width, :])

    o_ref[...] = acc


def _ffn(x1, h2, w_up, cw, cb, w_down, seq):
    m = x1.shape[0]
    tm = FFN_TM
    hb = tm // BF16_SUBLANES
    kern = functools.partial(_ffn_kernel, tm=tm, tiles_per_seq=seq // tm)

    def resident(shape):
        return pl.BlockSpec(shape, lambda i: (0, 0), pipeline_mode=pl.Buffered(1))

    return pl.pallas_call(
        kern,
        grid=(m // tm,),
        in_specs=[
            pl.BlockSpec((tm, D_MODEL), lambda i: (i, 0)),
            pl.BlockSpec((tm, D_MODEL), lambda i: (i, 0)),
            pl.BlockSpec((BF16_SUBLANES, D_MODEL), lambda i: (jnp.maximum(i * hb - 1, 0), 0)),
            resident((D_MODEL, 2 * D_FF)),
            resident((FFN_CONV, 2 * D_FF)),
            resident((1, 2 * D_FF)),
            resident((D_FF, D_MODEL)),
        ],
        out_specs=pl.BlockSpec((tm, D_MODEL), lambda i: (i, 0)),
        out_shape=jax.ShapeDtypeStruct((m, D_MODEL), F32),
        scratch_shapes=[
            pltpu.VMEM((FFN_SLOTS, tm + BF16_SUBLANES, FFN_SCR), F32),
            pltpu.VMEM((FFN_SLOTS, tm + BF16_SUBLANES, FFN_SCR), F32),
        ],
        compiler_params=pltpu.CompilerParams(
            dimension_semantics=("parallel",), vmem_limit_bytes=VMEM_LIMIT),
        name="ffn",
    )(x1, h2, h2, w_up, cw, cb, w_down)


def _head_expand(rows, cols, chunk):
    r = lax.broadcasted_iota(jnp.int32, (rows, cols), 0)
    c = lax.broadcasted_iota(jnp.int32, (rows, cols), 1) // chunk
    return (r == c).astype(BF16)


def _layer(x, rel_bias, norm_mix_g, w_in, q_norm_g, k_norm_g, lq1, lk1, lq2, lk2, attn_subln_g,
           conv_ssm_w, conv_ssm_b, dt_bias, a_log, d_skip, ssm_norm_g, w_proj_attn, w_proj_ssm,
           w_out, norm_ffn_g, w_up, conv_ffn_w, conv_ffn_b, w_down):
    b, s, d = x.shape
    m = b * s
    x2 = x.reshape(m, d)

    o_dt = sum(width for name, width in W_SEGMENTS if name != "gate")
    w_main = jnp.concatenate([w_in[:, :o_dt].astype(BF16),
                              w_in[:, o_dt + SSM_HEADS:].astype(BF16)], axis=1)
    pad_h = LANES - SSM_HEADS
    w_dt = jnp.pad(w_in[:, o_dt:o_dt + SSM_HEADS], ((0, 0), (0, pad_h))).astype(BF16)
    dt_b = jnp.pad(dt_bias.astype(F32), (0, pad_h)).reshape(1, LANES)
    reps = Q_COLS // ATTN_HEAD_DIM
    qk_gain = jnp.stack([
        jnp.tile(q_norm_g.astype(F32) * (ATTN_HEAD_DIM ** -0.5 * LOG2E), reps),
        jnp.tile(k_norm_g.astype(F32), reps)]).reshape(2, 1, Q_COLS)

    proj, dt = _in_proj(x2, norm_mix_g.astype(F32).reshape(1, d), w_main, w_dt, dt_b, qk_gain)
    proj3 = proj.reshape(b, s, PROJ_COLS)

    vec = lambda v: v.astype(F32).reshape(1, -1)
    bias, far, lam = _bias_build(rel_bias.astype(F32), vec(lq1), vec(lk1), vec(lq2), vec(lk2), ATTN_T)
    y_attn = _diff_attn(lam[0, :1], far[:, 0, 0], proj3, bias,
                        vec(attn_subln_g) * (1.0 - LAM_INIT))

    cw = conv_ssm_w.astype(F32)
    cbias = conv_ssm_b.astype(F32).reshape(1, -1)
    xs_sl = slice(0, SSM_D_INNER)
    b_sl = slice(SSM_D_INNER, SSM_D_INNER + BC_COLS)
    c_sl = slice(SSM_D_INNER + BC_COLS, SSM_D_INNER + 2 * BC_COLS)
    y_ssm = _ssd(
        proj3, dt.reshape(b, s, LANES),
        jnp.pad(a_log.astype(F32), (0, pad_h)).reshape(1, LANES),
        cw[:, xs_sl], cbias[:, xs_sl], cw[:, b_sl], cbias[:, b_sl], cw[:, c_sl], cbias[:, c_sl],
        jnp.repeat(d_skip.astype(F32), SSM_HEAD_DIM).reshape(1, SSM_D_INNER),
        vec(ssm_norm_g), jnp.tile(_head_expand(LANES, SSM_D_INNER, SSM_HEAD_DIM), (2, 1)))

    x1, h2 = _merge(x2, y_attn.reshape(m, V_COLS), y_ssm.reshape(m, SSM_D_INNER), proj,
                    w_proj_attn.astype(BF16), w_proj_ssm.astype(BF16), w_out.astype(BF16),
                    vec(norm_ffn_g))

    out = _ffn(x1, h2, w_up.astype(BF16), conv_ffn_w.astype(F32),
               conv_ffn_b.astype(F32).reshape(1, -1), w_down.astype(BF16), s)
    return out.reshape(b, s, d)


def kernel(x, rel_bias, norm_mix_g, w_in, q_norm_g, k_norm_g, lambda_q1, lambda_k1, lambda_q2,
           lambda_k2, attn_subln_g, conv_ssm_w, conv_ssm_b, dt_bias, a_log, d_skip, ssm_norm_g,
           w_proj_attn, w_proj_ssm, w_out, norm_ffn_g, w_up, conv_ffn_w, conv_ffn_b, w_down):
    depth = w_in.shape[0]
    assert depth == 1, "lambda_init and the stacked-parameter layout are specialised to depth 1"
    return _layer(x, rel_bias, norm_mix_g[0], w_in[0], q_norm_g[0], k_norm_g[0], lambda_q1[0],
                  lambda_k1[0], lambda_q2[0], lambda_k2[0], attn_subln_g[0], conv_ssm_w[0],
                  conv_ssm_b[0], dt_bias[0], a_log[0], d_skip[0], ssm_norm_g[0], w_proj_attn[0],
                  w_proj_ssm[0], w_out[0], norm_ffn_g[0], w_up[0], conv_ffn_w[0], conv_ffn_b[0],
                  w_down[0])
```

```python
import functools
import math

import jax
import jax.numpy as jnp
from jax import lax
from jax.experimental import pallas as pl
from jax.experimental.pallas import tpu as pltpu

F32 = jnp.float32
BF16 = jnp.bfloat16

D_MODEL = 1024
ATTN_HEADS = 8
ATTN_HEAD_DIM = 64
ATTN_V_DIM = 2 * ATTN_HEAD_DIM
NUM_BUCKETS = 32
MAX_DISTANCE = 128
SSM_D_INNER = 2048
SSM_HEAD_DIM = 64
SSM_HEADS = 32
SSM_GROUPS = 4
SSM_HEADS_PER_GROUP = SSM_HEADS // SSM_GROUPS
SSM_STATE = 128
SSM_CONV = 4
D_FF = 2816
FFN_CONV = 3
RMS_EPS = 1e-6
SUBLN_EPS = 1e-5
LAM_INIT = 0.8 - 0.6 * math.exp(0.0)

Q_COLS = ATTN_HEADS * 2 * ATTN_HEAD_DIM
V_COLS = ATTN_HEADS * ATTN_V_DIM
BC_COLS = SSM_GROUPS * SSM_STATE
GROUP_COLS = SSM_D_INNER // SSM_GROUPS

LANES = 128
SUBLANES = 8
BF16_SUBLANES = 16
MXU_WIDTH = 256
VMEM_LIMIT = 56 * 1024 * 1024

LOG2E = math.log2(math.e)
NEG_BIG = -1e30

W_SEGMENTS = (("q", Q_COLS), ("k", Q_COLS), ("v", V_COLS), ("z", SSM_D_INNER), ("xs", SSM_D_INNER),
              ("b", BC_COLS), ("c", BC_COLS), ("gate", 2 * D_MODEL))
OFF_Z = 0
OFF_Q = OFF_Z + SSM_D_INNER
OFF_K = OFF_Q + Q_COLS
OFF_V = OFF_K + Q_COLS
OFF_B = OFF_V + V_COLS
OFF_C = OFF_B + BC_COLS
OFF_XS = OFF_C + BC_COLS
OFF_GATE = OFF_XS + SSM_D_INNER
PROJ_COLS = OFF_GATE + 2 * D_MODEL
OUT_OFFSETS = {"q": OFF_Q, "k": OFF_K, "v": OFF_V, "z": OFF_Z, "xs": OFF_XS, "b": OFF_B,
               "c": OFF_C, "gate": OFF_GATE}

IN_TM = 512
IN_TN = 512
IN_HALVES = 2
ATTN_T = 128
ATTN_HEADS_PER_STEP = 2
ATTN_LAGS = (2, 5)
SSD_L = 128
SSD_SUB = 4
MERGE_TM = 512
FFN_TM = 512
FFN_SPLITS = ((0, 512), (512, 512), (1024, 512), (1536, 512), (2048, 512), (2560, 256))
FFN_SCR = max(w for _, w in FFN_SPLITS)
FFN_SLOTS = 3


def _split(v, parts):
    out = []
    for _ in range(parts - 1):
        hi = v.astype(BF16)
        out.append(hi)
        v = v - hi.astype(F32)
    out.append(v.astype(BF16))
    return out


def _dot(a, b):
    return jnp.dot(a, b, preferred_element_type=F32)


def _dot_nt(a, b):
    return lax.dot_general(a, b, (((1,), (1,)), ((), ())), preferred_element_type=F32)


def _dot_exact_rhs(parts, sel):
    out = _dot(parts[0], sel)
    for p in parts[1:]:
        out = out + _dot(p, sel)
    return out


def _dot_exact_lhs(sel, parts):
    out = _dot(sel, parts[0])
    for p in parts[1:]:
        out = out + _dot(sel, p)
    return out


def _silu(v):
    h = 0.5 * v
    return h + h * jnp.tanh(h)


def _in_proj_kernel(x0_ref, xn_ref, g_ref, w_ref, wdt_ref, dtb_ref, qkg_ref, o_ref, dt_ref,
                    h_scr, *, tile_plan):
    i = pl.program_id(0)
    half = pl.program_id(1)
    slot = i % 2

    def normed(x):
        ms = jnp.mean(x * x, axis=-1, keepdims=True)
        return (x * lax.rsqrt(ms + RMS_EPS) * g_ref[...]).astype(BF16)

    @pl.when(jnp.logical_and(i == 0, half == 0))
    def _():
        h_scr[0] = normed(x0_ref[...])

    h = h_scr[slot]

    def tile_cols(t):
        return slice(t * IN_TN, (t + 1) * IN_TN)

    def store_plain(out0, acc):
        o_ref[:, out0:out0 + IN_TN] = acc.astype(BF16)

    def store_head_normed(out0, acc, gain_row, gain_col0):
        lo_half = lax.broadcasted_iota(jnp.int32, (IN_TM, LANES), 1) < ATTN_HEAD_DIM
        for cblk in range(IN_TN // LANES):
            a = acc[:, cblk * LANES:(cblk + 1) * LANES]
            sq = a * a
            s_lo = jnp.sum(jnp.where(lo_half, sq, 0.0), axis=-1, keepdims=True)
            s_hi = jnp.sum(jnp.where(lo_half, 0.0, sq), axis=-1, keepdims=True)
            ssum = jnp.where(lo_half, s_lo, s_hi)
            y = a * lax.rsqrt(ssum * (1.0 / ATTN_HEAD_DIM) + RMS_EPS)
            y = y * qkg_ref[gain_row, :, gain_col0 + cblk * LANES:gain_col0 + (cblk + 1) * LANES]
            o_ref[:, out0 + cblk * LANES:out0 + (cblk + 1) * LANES] = y.astype(BF16)

    for hv, plan in enumerate(tile_plan):
        @pl.when(half == hv)
        def _(plan=plan, hv=hv):
            acc_next = _dot(h, w_ref[:, tile_cols(0)])
            if hv == 0:
                t = _dot(h, wdt_ref[...]) + dtb_ref[...]
                dt_ref[...] = jnp.maximum(t, 0.0) + jnp.log1p(jnp.exp(-jnp.abs(t)))
            if hv == len(tile_plan) - 1:
                h_scr[1 - slot] = normed(xn_ref[...])
            for t, (out0, head_norm) in enumerate(plan):
                acc = acc_next
                if t + 1 < len(plan):
                    acc_next = _dot(h, w_ref[:, tile_cols(t + 1)])
                if head_norm is None:
                    store_plain(out0, acc)
                else:
                    store_head_normed(out0, acc, *head_norm)


def _in_proj_plan():
    half_cols = PROJ_COLS // IN_HALVES
    gain_rows = {"q": 0, "k": 1}
    plan = [[] for _ in range(IN_HALVES)]
    w_off = 0
    for name, width in W_SEGMENTS:
        for col in range(0, width, IN_TN):
            hv, out_col = (w_off + col) // half_cols, OUT_OFFSETS[name] + col
            assert out_col // half_cols == hv and width % IN_TN == 0
            head_norm = (gain_rows[name], col) if name in gain_rows else None
            plan[hv].append((out_col - hv * half_cols, head_norm))
        w_off += width
    return tuple(tuple(p) for p in plan)


def _in_proj(x2, g, w_main, w_dt, dt_bias, qk_gain):
    m = x2.shape[0]
    n_tiles = m // IN_TM
    half_cols = PROJ_COLS // IN_HALVES
    kern = functools.partial(_in_proj_kernel, tile_plan=_in_proj_plan())
    return pl.pallas_call(
        kern,
        grid=(m // IN_TM, IN_HALVES),
        in_specs=[
            pl.BlockSpec((IN_TM, D_MODEL), lambda i, hv: (0, 0)),
            pl.BlockSpec((IN_TM, D_MODEL), lambda i, hv: (jnp.minimum(i + 1, n_tiles - 1), 0)),
            pl.BlockSpec((1, D_MODEL), lambda i, hv: (0, 0)),
            pl.BlockSpec((D_MODEL, half_cols), lambda i, hv: (0, hv)),
            pl.BlockSpec((D_MODEL, LANES), lambda i, hv: (0, 0)),
            pl.BlockSpec((1, LANES), lambda i, hv: (0, 0)),
            pl.BlockSpec(qk_gain.shape, lambda i, hv: (0, 0, 0)),
        ],
        out_specs=[
            pl.BlockSpec((IN_TM, half_cols), lambda i, hv: (i, hv)),
            pl.BlockSpec((IN_TM, LANES), lambda i, hv: (i, 0)),
        ],
        out_shape=[
            jax.ShapeDtypeStruct((m, PROJ_COLS), BF16),
            jax.ShapeDtypeStruct((m, LANES), F32),
        ],
        scratch_shapes=[pltpu.VMEM((2, IN_TM, D_MODEL), BF16)],
        compiler_params=pltpu.CompilerParams(
            dimension_semantics=("arbitrary", "arbitrary"), vmem_limit_bytes=VMEM_LIMIT),
        name="in_proj",
    )(x2, x2, g, w_main, w_dt, dt_bias, qk_gain)


def _bias_kernel(rel_ref, lq1_ref, lk1_ref, lq2_ref, lk2_ref, bias_ref, far_ref, lam_ref, *, tile):
    h = pl.program_id(0)
    row = lax.broadcasted_iota(jnp.int32, (tile, tile), 0)
    col = lax.broadcasted_iota(jnp.int32, (tile, tile), 1)
    max_exact = NUM_BUCKETS // 2
    for t in range(2):
        d = row - col + t * tile
        n = jnp.maximum(d, 0)
        nf = jnp.maximum(n, 1).astype(F32)
        large = max_exact + (jnp.log(nf / max_exact) / math.log(MAX_DISTANCE / max_exact)
                             * (NUM_BUCKETS - max_exact)).astype(jnp.int32)
        large = jnp.minimum(large, NUM_BUCKETS - 1)
        bucket = jnp.where(n < max_exact, n, large)
        bias = jnp.zeros((tile, tile), F32)
        for bk in range(NUM_BUCKETS):
            bias = jnp.where(bucket == bk, rel_ref[bk, h], bias)
        bias = bias * LOG2E
        if t == 0:
            bias = jnp.where(d >= 0, bias, NEG_BIG)
        bias_ref[0, t] = bias
    far_ref[...] = jnp.full(far_ref.shape, rel_ref[NUM_BUCKETS - 1, h] * LOG2E, F32)
    s1 = jnp.sum(lq1_ref[...] * lk1_ref[...], axis=-1, keepdims=True)
    s2 = jnp.sum(lq2_ref[...] * lk2_ref[...], axis=-1, keepdims=True)
    lam = jnp.exp(s1) - jnp.exp(s2) + LAM_INIT
    lam_ref[...] = jnp.broadcast_to(lam, lam_ref.shape)


def _bias_build(rel_bias, lq1, lk1, lq2, lk2, tile):
    vec = pl.BlockSpec((1, ATTN_HEAD_DIM), lambda h: (0, 0))
    return pl.pallas_call(
        functools.partial(_bias_kernel, tile=tile),
        grid=(ATTN_HEADS,),
        in_specs=[pl.BlockSpec(memory_space=pltpu.SMEM), vec, vec, vec, vec],
        out_specs=[
            pl.BlockSpec((1, 2, tile, tile), lambda h: (h, 0, 0, 0)),
            pl.BlockSpec((1, SUBLANES, LANES), lambda h: (h, 0, 0)),
            pl.BlockSpec((SUBLANES, LANES), lambda h: (0, 0)),
        ],
        out_shape=[
            jax.ShapeDtypeStruct((ATTN_HEADS, 2, tile, tile), F32),
            jax.ShapeDtypeStruct((ATTN_HEADS, SUBLANES, LANES), F32),
            jax.ShapeDtypeStruct((SUBLANES, LANES), F32),
        ],
        compiler_params=pltpu.CompilerParams(dimension_semantics=("arbitrary",)),
        name="bias_build",
    )(rel_bias, lq1, lk1, lq2, lk2)


def _attn_kernel(lam_ref, far_ref, q_ref, k_ref, v_ref, bias_ref, g_ref, o_ref, vext_scr,
                 *, tile, nq, heads):
    h0 = pl.program_id(1) * heads
    lam = lam_ref[0]
    lane = lax.broadcasted_iota(jnp.int32, (tile, LANES), 1)
    lo_half = lane < ATTN_HEAD_DIM
    gain = g_ref[...]

    for j in range(heads):
        vals = v_ref[0, :, j * LANES:(j + 1) * LANES]
        vext_scr[j] = jnp.concatenate([vals, jnp.ones_like(vals)], axis=1)

    def scores(j, qi):
        q = q_ref[0, qi * tile:(qi + 1) * tile, j * LANES:(j + 1) * LANES]
        zero = jnp.zeros_like(q)
        q_both = jnp.concatenate([jnp.where(lo_half, q, zero), jnp.where(lo_half, zero, q)], axis=0)
        return _dot_nt(q_both, k_ref[0, 0:(qi + 1) * tile, j * LANES:(j + 1) * LANES])

    def softmax(j, qi, s):
        far = far_ref[h0 + j]
        nk = (qi + 1) * tile
        n_far = nk - min(qi + 1, 2) * tile
        if qi == 0:
            near_bias = bias_ref[j, 0]
        else:
            near_bias = jnp.concatenate([bias_ref[j, 1], bias_ref[j, 0]], axis=1)
        s_near = s[:, n_far:] + jnp.concatenate([near_bias, near_bias], axis=0)
        m = jnp.max(s_near, axis=-1, keepdims=True)
        if n_far:
            s_far = s[:, :n_far]
            m = jnp.maximum(m, jnp.max(s_far, axis=-1, keepdims=True) + far)
            p = jnp.concatenate([jnp.exp2(s_far - (m - far)), jnp.exp2(s_near - m)], axis=1)
        else:
            p = jnp.exp2(s_near - m)
        return p.astype(BF16)

    def weighted_values(j, qi, p):
        pv = _dot(p, vext_scr[j, 0:(qi + 1) * tile, :])
        return pv[:, :LANES] / pv[:, LANES:]

    units = [(j, qi) for qi in reversed(range(nq)) for j in range(heads)]
    n_units = len(units)
    s_q, p_q = {}, {}
    lag_s, lag_v = ATTN_LAGS
    for step in range(n_units + lag_v):
        if step < n_units:
            s_q[step] = scores(*units[step])
        if 0 <= step - lag_s < n_units:
            p_q[step - lag_s] = softmax(*units[step - lag_s], s_q.pop(step - lag_s))
        if 0 <= step - lag_v < n_units:
            j, qi = units[step - lag_v]
            both = weighted_values(j, qi, p_q.pop(step - lag_v))
            o = both[:tile] - lam * both[tile:]
            ms = jnp.mean(o * o, axis=-1, keepdims=True)
            o = o * lax.rsqrt(ms + SUBLN_EPS) * gain
            o_ref[0, qi * tile:(qi + 1) * tile, j * LANES:(j + 1) * LANES] = o.astype(o_ref.dtype)


def _diff_attn(lam, far, proj3, bias, gain):
    b, s, _ = proj3.shape
    tile = bias.shape[-1]
    nq = s // tile
    heads = ATTN_HEADS_PER_STEP
    width = heads * LANES
    qb, kb, vb = OFF_Q // width, OFF_K // width, OFF_V // width
    kern = functools.partial(_attn_kernel, tile=tile, nq=nq, heads=heads)
    return pl.pallas_call(
        kern,
        grid=(b, ATTN_HEADS // heads),
        in_specs=[
            pl.BlockSpec(memory_space=pltpu.SMEM),
            pl.BlockSpec(memory_space=pltpu.SMEM),
            pl.BlockSpec((1, s, width), lambda i, h: (i, 0, qb + h)),
            pl.BlockSpec((1, s, width), lambda i, h: (i, 0, kb + h)),
            pl.BlockSpec((1, s, width), lambda i, h: (i, 0, vb + h)),
            pl.BlockSpec((heads, 2, tile, tile), lambda i, h: (h, 0, 0, 0)),
            pl.BlockSpec((1, LANES), lambda i, h: (0, 0)),
        ],
        out_specs=pl.BlockSpec((1, s, width), lambda i, h: (i, 0, h)),
        out_shape=jax.ShapeDtypeStruct((b, s, V_COLS), BF16),
        scratch_shapes=[pltpu.VMEM((heads, s, 2 * LANES), BF16)],
        compiler_params=pltpu.CompilerParams(
            dimension_semantics=("parallel", "parallel"), vmem_limit_bytes=VMEM_LIMIT),
        name="diff_attn",
    )(lam, far, proj3, proj3, proj3, bias, gain)


def _shift_matrices(taps, length, halo_rows):
    def build(rows, cols, offset):
        r = lax.broadcasted_iota(jnp.int32, (taps * rows, cols), 0)
        j = lax.broadcasted_iota(jnp.int32, (taps * rows, cols), 1)
        tap = r // rows
        return (j == offset + (r - tap * rows) - (taps - 1 - tap)).astype(BF16)

    return build(length, length, 0), build(halo_rows, halo_rows, halo_rows)


def _conv_silu(x, halo, w_ref, b_ref, shifts, length, cols):
    shift_main, shift_head = shifts
    hrows = halo.shape[0]
    shifted = _dot(shift_main, x)
    head = _dot(shift_head, halo)
    acc = b_ref[:, cols]
    for k in range(w_ref.shape[0]):
        tap = shifted[k * length:(k + 1) * length]
        tap = jnp.concatenate([tap[0:hrows] + head[k * hrows:(k + 1) * hrows], tap[hrows:]], axis=0)
        acc = acc + w_ref[k:k + 1, cols] * tap
    return _silu(acc)


def _ssd_kernel(xs_ref, xh_ref, bm_ref, bh_ref, cm_ref, ch_ref, z_ref, dt_ref, alog_ref,
                wx_ref, bx_ref, wb_ref, bb_ref, wc_ref, bc_ref, dskip_ref, ng_ref, e_ref,
                y_ref, state_scr, *, length, sub_chunks):
    c = pl.program_id(1)
    first = c == 0

    @pl.when(first)
    def _():
        state_scr[...] = jnp.zeros_like(state_scr)

    shift = _shift_matrices(SSM_CONV, length, BF16_SUBLANES)
    a = -jnp.exp(alog_ref[...])
    row = lax.broadcasted_iota(jnp.int32, (length, length), 0)
    col = lax.broadcasted_iota(jnp.int32, (length, length), 1)
    causal = row >= col
    tri = causal.astype(BF16)
    sel2 = e_ref[...]
    lane = lax.broadcasted_iota(jnp.int32, (length, LANES), 1)
    lo_half = lane < SSM_HEAD_DIM

    def conv(k, main_ref, halo_ref, w_ref, b_ref, cols=slice(None)):
        r0 = k * length
        if k == 0:
            halo = halo_ref[0, :, cols]
            halo = jnp.where(first, jnp.zeros_like(halo), halo)
        else:
            halo = main_ref[0, r0 - BF16_SUBLANES:r0, cols]
        return _conv_silu(main_ref[0, r0:r0 + length, cols], halo, w_ref, b_ref, shift, length, cols)

    def expand(v):
        return _dot(jnp.concatenate(_split(v, 2), axis=1), sel2)

    def prepare(k):
        rows = slice(k * length, (k + 1) * length)
        bm = conv(k, bm_ref, bh_ref, wb_ref, bb_ref).astype(BF16)
        cm = conv(k, cm_ref, ch_ref, wc_ref, bc_ref).astype(BF16)
        xs = [conv(k, xs_ref, xh_ref, wx_ref, bx_ref, slice(g * GROUP_COLS, (g + 1) * GROUP_COLS))
              for g in range(SSM_GROUPS)]
        dt = dt_ref[0, rows, :]
        cum = _dot_exact_lhs(tri, _split(dt * a, 3))
        cum2 = cum * LOG2E
        src_t = cum2.T - jnp.log2(dt.T)
        total = cum[length - 1:length, :]
        ecum_x = expand(jnp.exp(cum))
        edtw_x = expand(dt * jnp.exp(total - cum))
        return bm, cm, xs, cum2, src_t, ecum_x, edtw_x

    def scan(k, prepared):
        bm, cm, xs, cum2, src_t, ecum_x, edtw_x = prepared
        rows = slice(k * length, (k + 1) * length)
        etot_x = ecum_x[length - 1:length, :]
        for g in range(SSM_GROUPS):
            gs = slice(g * SSM_STATE, (g + 1) * SSM_STATE)
            cs = slice(g * GROUP_COLS, (g + 1) * GROUP_COLS)
            xs_f = xs[g]
            b_g = bm[:, gs]
            c_g = cm[:, gs]
            xs_g = xs_f.astype(BF16)
            cb = _dot_nt(c_g, b_g)
            st = state_scr[g]
            y_g = _dot(c_g, st.astype(BF16)) * ecum_x[:, cs]
            xw = (xs_f * edtw_x[:, cs]).astype(BF16)
            b_t = b_g.astype(F32).T.astype(BF16)
            state_scr[g] = st * etot_x[:, cs] + _dot(b_t, xw)
            y_parts = []
            for pair in range(SSM_HEADS_PER_GROUP // 2):
                ms = []
                for hh in range(2):
                    h = g * SSM_HEADS_PER_GROUP + pair * 2 + hh
                    seg = jnp.where(causal, cum2[:, h:h + 1] - src_t[h:h + 1, :], NEG_BIG)
                    ms.append((cb * jnp.exp2(seg)).astype(BF16))
                x_pair = xs_g[:, pair * LANES:(pair + 1) * LANES]
                zero = jnp.zeros_like(x_pair)
                rhs = jnp.concatenate([jnp.where(lo_half, x_pair, zero),
                                       jnp.where(lo_half, zero, x_pair)], axis=0)
                y_parts.append(_dot(jnp.concatenate(ms, axis=1), rhs))
            y_g = y_g + jnp.concatenate(y_parts, axis=1)
            y_g = y_g + xs_f * dskip_ref[:, cs]
            y_g = y_g * _silu(z_ref[0, rows, cs].astype(F32))
            msq = jnp.mean(y_g * y_g, axis=-1, keepdims=True)
            y_g = y_g * lax.rsqrt(msq + SUBLN_EPS) * ng_ref[:, cs]
            y_ref[0, rows, cs] = y_g.astype(y_ref.dtype)

    prepared = prepare(0)
    for k in range(sub_chunks):
        current = prepared
        if k + 1 < sub_chunks:
            prepared = prepare(k + 1)
        scan(k, current)


def _ssd(proj3, dt3, alog, wx, bx, wb, bb, wc, bc, dskip_x, ng, sel):
    b, s, _ = proj3.shape
    length = SSD_L
    rows = SSD_SUB * length
    nc = s // rows
    hb = rows // BF16_SUBLANES
    xs_blk, z_blk = OFF_XS // SSM_D_INNER, OFF_Z // SSM_D_INNER
    b_blk, c_blk = OFF_B // BC_COLS, OFF_C // BC_COLS

    def main(width, blk):
        return pl.BlockSpec((1, rows, width), lambda i, c: (i, c, blk))

    def halo(width, blk):
        return pl.BlockSpec((1, BF16_SUBLANES, width),
                            lambda i, c: (i, jnp.maximum(c * hb - 1, 0), blk))

    def const(shape):
        return pl.BlockSpec(shape, lambda i, c: (0,) * len(shape))

    kern = functools.partial(_ssd_kernel, length=length, sub_chunks=SSD_SUB)
    return pl.pallas_call(
        kern,
        grid=(b, nc),
        in_specs=[
            main(SSM_D_INNER, xs_blk), halo(SSM_D_INNER, xs_blk),
            main(BC_COLS, b_blk), halo(BC_COLS, b_blk),
            main(BC_COLS, c_blk), halo(BC_COLS, c_blk),
            main(SSM_D_INNER, z_blk),
            pl.BlockSpec((1, rows, LANES), lambda i, c: (i, c, 0)),
            const((1, LANES)),
            const((SSM_CONV, SSM_D_INNER)), const((1, SSM_D_INNER)),
            const((SSM_CONV, BC_COLS)), const((1, BC_COLS)),
            const((SSM_CONV, BC_COLS)), const((1, BC_COLS)),
            const((1, SSM_D_INNER)), const((1, SSM_D_INNER)),
            const((2 * LANES, SSM_D_INNER)),
        ],
        out_specs=pl.BlockSpec((1, rows, SSM_D_INNER), lambda i, c: (i, c, 0)),
        out_shape=jax.ShapeDtypeStruct((b, s, SSM_D_INNER), BF16),
        scratch_shapes=[pltpu.VMEM((SSM_GROUPS, SSM_STATE, GROUP_COLS), F32)],
        compiler_params=pltpu.CompilerParams(
            dimension_semantics=("parallel", "arbitrary"), vmem_limit_bytes=VMEM_LIMIT),
        name="ssd",
    )(proj3, proj3, proj3, proj3, proj3, proj3, proj3, dt3, alog,
      wx, bx, wb, bb, wc, bc, dskip_x, ng, sel)


def _merge_kernel(x_ref, ya_ref, ys_ref, ga_ref, gs_ref, wpa_ref, wps_ref, wo_ref, g_ref,
                  x1_ref, h2_ref):
    pa = _dot(ya_ref[...], wpa_ref[...])
    ps = _dot(ys_ref[...], wps_ref[...])
    ga = 1.0 / (1.0 + jnp.exp(-ga_ref[...].astype(F32)))
    gs = 1.0 / (1.0 + jnp.exp(-gs_ref[...].astype(F32)))
    mixed = (ga * pa + gs * ps).astype(BF16)
    x1 = x_ref[...] + _dot(mixed, wo_ref[...])
    x1_ref[...] = x1
    ms = jnp.mean(x1 * x1, axis=-1, keepdims=True)
    h2_ref[...] = (x1 * lax.rsqrt(ms + RMS_EPS) * g_ref[...]).astype(BF16)


def _merge(x2, y_attn, y_ssm, proj, wpa, wps, wo, g):
    m = x2.shape[0]
    tm = MERGE_TM
    ga_blk = OFF_GATE // D_MODEL

    def rows(width, blk=0):
        return pl.BlockSpec((tm, width), lambda i: (i, blk))

    def const(shape):
        return pl.BlockSpec(shape, lambda i: (0, 0))

    return pl.pallas_call(
        _merge_kernel,
        grid=(m // tm,),
        in_specs=[
            rows(D_MODEL), rows(V_COLS), rows(SSM_D_INNER),
            rows(D_MODEL, ga_blk), rows(D_MODEL, ga_blk + 1),
            const((V_COLS, D_MODEL)), const((SSM_D_INNER, D_MODEL)), const((D_MODEL, D_MODEL)),
            const((1, D_MODEL)),
        ],
        out_specs=[rows(D_MODEL), rows(D_MODEL)],
        out_shape=[jax.ShapeDtypeStruct((m, D_MODEL), F32),
                   jax.ShapeDtypeStruct((m, D_MODEL), BF16)],
        compiler_params=pltpu.CompilerParams(
            dimension_semantics=("parallel",), vmem_limit_bytes=VMEM_LIMIT),
        name="merge",
    )(x2, y_attn, y_ssm, proj, proj, wpa, wps, wo, g)


def _ffn_kernel(x1_ref, h_ref, hh_ref, wup_ref, cw_ref, cb_ref, wd_ref, o_ref, gscr, vscr,
                *, tm, tiles_per_seq):
    i = pl.program_id(0)
    first = (i % tiles_per_seq) == 0
    halo = hh_ref[...]
    halo = jnp.where(first, jnp.zeros_like(halo), halo)
    hrows = halo.shape[0]
    h_ext = jnp.concatenate([halo, h_ref[...]], axis=0)
    taps = cw_ref.shape[0]

    slots = gscr.shape[0]

    def up_project(idx):
        start, width = FFN_SPLITS[idx]
        for col0, scr in ((start, gscr), (D_FF + start, vscr)):
            scr[idx % slots, :, 0:width] = _dot(h_ext, wup_ref[:, col0:col0 + width])

    def conv(idx, col0, scr):
        width = FFN_SPLITS[idx][1]
        cols = slice(col0, col0 + width)
        slot = idx % slots
        out = cb_ref[:, cols] + cw_ref[taps - 1:taps, cols] * scr[slot, hrows:hrows + tm, 0:width]
        for k in range(taps - 1):
            off = hrows - (taps - 1) + k
            out = out + cw_ref[k:k + 1, cols] * scr[slot, off:off + tm, 0:width]
        return out

    def gated(idx):
        start = FFN_SPLITS[idx][0]
        return (_silu(conv(idx, start, gscr)) * conv(idx, D_FF + start, vscr)).astype(BF16)

    n_chunks = len(FFN_SPLITS)
    ahead = slots - 1
    acc = x1_ref[...]
    acts = {}
    for idx in range(min(ahead, n_chunks)):
        up_project(idx)
    for idx in range(n_chunks + 1):
        if idx + ahead < n_chunks:
            up_project(idx + ahead)
        if idx < n_chunks:
            acts[idx] = gated(idx)
        if idx >= 1:
            start, width = FFN_SPLITS[idx - 1]
            acc = acc + _dot(acts.pop(idx - 1), wd_ref[start:start + width, :])

    o_ref[...] = acc


def _ffn(x1, h2, w_up, cw, cb, w_down, seq):
    m = x1.shape[0]
    tm = FFN_TM
    hb = tm // BF16_SUBLANES
    kern = functools.partial(_ffn_kernel, tm=tm, tiles_per_seq=seq // tm)

    def resident(shape):
        return pl.BlockSpec(shape, lambda i: (0, 0), pipeline_mode=pl.Buffered(1))

    return pl.pallas_call(
        kern,
        grid=(m // tm,),
        in_specs=[
            pl.BlockSpec((tm, D_MODEL), lambda i: (i, 0)),
            pl.BlockSpec((tm, D_MODEL), lambda i: (i, 0)),
            pl.BlockSpec((BF16_SUBLANES, D_MODEL), lambda i: (jnp.maximum(i * hb - 1, 0), 0)),
            resident((D_MODEL, 2 * D_FF)),
            resident((FFN_CONV, 2 * D_FF)),
            resident((1, 2 * D_FF)),
            resident((D_FF, D_MODEL)),
        ],
        out_specs=pl.BlockSpec((tm, D_MODEL), lambda i: (i, 0)),
        out_shape=jax.ShapeDtypeStruct((m, D_MODEL), F32),
        scratch_shapes=[
            pltpu.VMEM((FFN_SLOTS, tm + BF16_SUBLANES, FFN_SCR), F32),
            pltpu.VMEM((FFN_SLOTS, tm + BF16_SUBLANES, FFN_SCR), F32),
        ],
        compiler_params=pltpu.CompilerParams(
            dimension_semantics=("parallel",), vmem_limit_bytes=VMEM_LIMIT),
        name="ffn",
    )(x1, h2, h2, w_up, cw, cb, w_down)


def _head_expand(rows, cols, chunk):
    r = lax.broadcasted_iota(jnp.int32, (rows, cols), 0)
    c = lax.broadcasted_iota(jnp.int32, (rows, cols), 1) // chunk
    return (r == c).astype(BF16)


def _layer(x, rel_bias, norm_mix_g, w_in, q_norm_g, k_norm_g, lq1, lk1, lq2, lk2, attn_subln_g,
           conv_ssm_w, conv_ssm_b, dt_bias, a_log, d_skip, ssm_norm_g, w_proj_attn, w_proj_ssm,
           w_out, norm_ffn_g, w_up, conv_ffn_w, conv_ffn_b, w_down):
    b, s, d = x.shape
    m = b * s
    x2 = x.reshape(m, d)

    o_dt = sum(width for name, width in W_SEGMENTS if name != "gate")
    w_main = jnp.concatenate([w_in[:, :o_dt].astype(BF16),
                              w_in[:, o_dt + SSM_HEADS:].astype(BF16)], axis=1)
    pad_h = LANES - SSM_HEADS
    w_dt = jnp.pad(w_in[:, o_dt:o_dt + SSM_HEADS], ((0, 0), (0, pad_h))).astype(BF16)
    dt_b = jnp.pad(dt_bias.astype(F32), (0, pad_h)).reshape(1, LANES)
    reps = Q_COLS // ATTN_HEAD_DIM
    qk_gain = jnp.stack([
        jnp.tile(q_norm_g.astype(F32) * (ATTN_HEAD_DIM ** -0.5 * LOG2E), reps),
        jnp.tile(k_norm_g.astype(F32), reps)]).reshape(2, 1, Q_COLS)

    proj, dt = _in_proj(x2, norm_mix_g.astype(F32).reshape(1, d), w_main, w_dt, dt_b, qk_gain)
    proj3 = proj.reshape(b, s, PROJ_COLS)

    vec = lambda v: v.astype(F32).reshape(1, -1)
    bias, far, lam = _bias_build(rel_bias.astype(F32), vec(lq1), vec(lk1), vec(lq2), vec(lk2), ATTN_T)
    y_attn = _diff_attn(lam[0, :1], far[:, 0, 0], proj3, bias,
                        vec(attn_subln_g) * (1.0 - LAM_INIT))

    cw = conv_ssm_w.astype(F32)
    cbias = conv_ssm_b.astype(F32).reshape(1, -1)
    xs_sl = slice(0, SSM_D_INNER)
    b_sl = slice(SSM_D_INNER, SSM_D_INNER + BC_COLS)
    c_sl = slice(SSM_D_INNER + BC_COLS, SSM_D_INNER + 2 * BC_COLS)
    y_ssm = _ssd(
        proj3, dt.reshape(b, s, LANES),
        jnp.pad(a_log.astype(F32), (0, pad_h)).reshape(1, LANES),
        cw[:, xs_sl], cbias[:, xs_sl], cw[:, b_sl], cbias[:, b_sl], cw[:, c_sl], cbias[:, c_sl],
        jnp.repeat(d_skip.astype(F32), SSM_HEAD_DIM).reshape(1, SSM_D_INNER),
        vec(ssm_norm_g), jnp.tile(_head_expand(LANES, SSM_D_INNER, SSM_HEAD_DIM), (2, 1)))

    x1, h2 = _merge(x2, y_attn.reshape(m, V_COLS), y_ssm.reshape(m, SSM_D_INNER), proj,
                    w_proj_attn.astype(BF16), w_proj_ssm.astype(BF16), w_out.astype(BF16),
                    vec(norm_ffn_g))

    out = _ffn(x1, h2, w_up.astype(BF16), conv_ffn_w.astype(F32),
               conv_ffn_b.astype(F32).reshape(1, -1), w_down.astype(BF16), s)
    return out.reshape(b, s, d)


def kernel(x, rel_bias, norm_mix_g, w_in, q_norm_g, k_norm_g, lambda_q1, lambda_k1, lambda_q2,
           lambda_k2, attn_subln_g, conv_ssm_w, conv_ssm_b, dt_bias, a_log, d_skip, ssm_norm_g,
           w_proj_attn, w_proj_ssm, w_out, norm_ffn_g, w_up, conv_ffn_w, conv_ffn_b, w_down):
    depth = w_in.shape[0]
    assert depth == 1, "lambda_init and the stacked-parameter layout are specialised to depth 1"
    return _layer(x, rel_bias, norm_mix_g[0], w_in[0], q_norm_g[0], k_norm_g[0], lambda_q1[0],
                  lambda_k1[0], lambda_q2[0], lambda_k2[0], attn_subln_g[0], conv_ssm_w[0],
                  conv_ssm_b[0], dt_bias[0], a_log[0], d_skip[0], ssm_norm_g[0], w_proj_attn[0],
                  w_proj_ssm[0], w_out[0], norm_ffn_g[0], w_up[0], conv_ffn_w[0], conv_ffn_b[0],
                  w_down[0])
```

```python
import functools
import math

import jax
import jax.numpy as jnp
from jax import lax
from jax.experimental import pallas as pl
from jax.experimental.pallas import tpu as pltpu

F32 = jnp.float32
BF16 = jnp.bfloat16

D_MODEL = 1024
ATTN_HEADS = 8
ATTN_HEAD_DIM = 64
ATTN_V_DIM = 2 * ATTN_HEAD_DIM
NUM_BUCKETS = 32
MAX_DISTANCE = 128
SSM_D_INNER = 2048
SSM_HEAD_DIM = 64
SSM_HEADS = 32
SSM_GROUPS = 4
SSM_HEADS_PER_GROUP = SSM_HEADS // SSM_GROUPS
SSM_STATE = 128
SSM_CONV = 4
D_FF = 2816
FFN_CONV = 3
RMS_EPS = 1e-6
SUBLN_EPS = 1e-5
LAM_INIT = 0.8 - 0.6 * math.exp(0.0)

Q_COLS = ATTN_HEADS * 2 * ATTN_HEAD_DIM
V_COLS = ATTN_HEADS * ATTN_V_DIM
BC_COLS = SSM_GROUPS * SSM_STATE
GROUP_COLS = SSM_D_INNER // SSM_GROUPS

LANES = 128
SUBLANES = 8
BF16_SUBLANES = 16
VMEM_LIMIT = 56 * 1024 * 1024

LOG2E = math.log2(math.e)
NEG_BIG = -1e30

W_SEGMENTS = (("q", Q_COLS), ("k", Q_COLS), ("v", V_COLS), ("z", SSM_D_INNER), ("xs", SSM_D_INNER),
              ("b", BC_COLS), ("c", BC_COLS), ("gate", 2 * D_MODEL))
OFF_Z = 0
OFF_Q = OFF_Z + SSM_D_INNER
OFF_K = OFF_Q + Q_COLS
OFF_V = OFF_K + Q_COLS
OFF_B = OFF_V + V_COLS
OFF_C = OFF_B + BC_COLS
OFF_XS = OFF_C + BC_COLS
OFF_GATE = OFF_XS + SSM_D_INNER
PROJ_COLS = OFF_GATE + 2 * D_MODEL
OUT_OFFSETS = {"q": OFF_Q, "k": OFF_K, "v": OFF_V, "z": OFF_Z, "xs": OFF_XS, "b": OFF_B,
               "c": OFF_C, "gate": OFF_GATE}

IN_TM = 512
IN_TN = 512
IN_HALVES = 2
ATTN_T = 128
ATTN_HEADS_PER_STEP = 2
ATTN_LAGS = (2, 5)
SSD_L = 128
SSD_SUB = 4
MERGE_TM = 512
FFN_TM = 512
FFN_SPLITS = ((0, 512), (512, 512), (1024, 512), (1536, 512), (2048, 512), (2560, 256))
FFN_SCR = max(w for _, w in FFN_SPLITS)
FFN_SLOTS = 3


def _split(v, parts):
    out = []
    for _ in range(parts - 1):
        hi = v.astype(BF16)
        out.append(hi)
        v = v - hi.astype(F32)
    out.append(v.astype(BF16))
    return out


def _dot(a, b):
    return jnp.dot(a, b, preferred_element_type=F32)


def _dot_nt(a, b):
    return lax.dot_general(a, b, (((1,), (1,)), ((), ())), preferred_element_type=F32)


def _dot_exact_lhs(sel, parts):
    out = _dot(sel, parts[0])
    for p in parts[1:]:
        out = out + _dot(sel, p)
    return out


def _silu(v):
    h = 0.5 * v
    return h + h * jnp.tanh(h)


def _in_proj_kernel(x0_ref, xn_ref, g_ref, w_ref, wdt_ref, dtb_ref, qkg_ref, o_ref, dt_ref,
                    h_scr, *, tile_plan):
    i = pl.program_id(0)
    half = pl.program_id(1)
    slot = i % 2

    def normed(x):
        ms = jnp.mean(x * x, axis=-1, keepdims=True)
        return (x * lax.rsqrt(ms + RMS_EPS) * g_ref[...]).astype(BF16)

    @pl.when(jnp.logical_and(i == 0, half == 0))
    def _():
        h_scr[0] = normed(x0_ref[...])

    h = h_scr[slot]

    def tile_cols(t):
        return slice(t * IN_TN, (t + 1) * IN_TN)

    def store_plain(out0, acc):
        o_ref[:, out0:out0 + IN_TN] = acc.astype(BF16)

    def store_head_normed(out0, acc, gain_row, gain_col0):
        lo_half = lax.broadcasted_iota(jnp.int32, (IN_TM, LANES), 1) < ATTN_HEAD_DIM
        for cblk in range(IN_TN // LANES):
            a = acc[:, cblk * LANES:(cblk + 1) * LANES]
            sq = a * a
            s_lo = jnp.sum(jnp.where(lo_half, sq, 0.0), axis=-1, keepdims=True)
            s_hi = jnp.sum(jnp.where(lo_half, 0.0, sq), axis=-1, keepdims=True)
            ssum = jnp.where(lo_half, s_lo, s_hi)
            y = a * lax.rsqrt(ssum * (1.0 / ATTN_HEAD_DIM) + RMS_EPS)
            y = y * qkg_ref[gain_row, :, gain_col0 + cblk * LANES:gain_col0 + (cblk + 1) * LANES]
            o_ref[:, out0 + cblk * LANES:out0 + (cblk + 1) * LANES] = y.astype(BF16)

    for hv, plan in enumerate(tile_plan):
        @pl.when(half == hv)
        def _(plan=plan, hv=hv):
            acc_next = _dot(h, w_ref[:, tile_cols(0)])
            if hv == 0:
                t = _dot(h, wdt_ref[...]) + dtb_ref[...]
                dt_ref[...] = jnp.maximum(t, 0.0) + jnp.log1p(jnp.exp(-jnp.abs(t)))
            if hv == len(tile_plan) - 1:
                h_scr[1 - slot] = normed(xn_ref[...])
            for t, (out0, head_norm) in enumerate(plan):
                acc = acc_next
                if t + 1 < len(plan):
                    acc_next = _dot(h, w_ref[:, tile_cols(t + 1)])
                if head_norm is None:
                    store_plain(out0, acc)
                else:
                    store_head_normed(out0, acc, *head_norm)


def _in_proj_plan():
    half_cols = PROJ_COLS // IN_HALVES
    gain_rows = {"q": 0, "k": 1}
    plan = [[] for _ in range(IN_HALVES)]
    w_off = 0
    for name, width in W_SEGMENTS:
        for col in range(0, width, IN_TN):
            hv, out_col = (w_off + col) // half_cols, OUT_OFFSETS[name] + col
            assert out_col // half_cols == hv and width % IN_TN == 0
            head_norm = (gain_rows[name], col) if name in gain_rows else None
            plan[hv].append((out_col - hv * half_cols, head_norm))
        w_off += width
    return tuple(tuple(p) for p in plan)


def _in_proj(x2, g, w_main, w_dt, dt_bias, qk_gain):
    m = x2.shape[0]
    n_tiles = m // IN_TM
    half_cols = PROJ_COLS // IN_HALVES
    kern = functools.partial(_in_proj_kernel, tile_plan=_in_proj_plan())
    return pl.pallas_call(
        kern,
        grid=(m // IN_TM, IN_HALVES),
        in_specs=[
            pl.BlockSpec((IN_TM, D_MODEL), lambda i, hv: (0, 0)),
            pl.BlockSpec((IN_TM, D_MODEL), lambda i, hv: (jnp.minimum(i + 1, n_tiles - 1), 0)),
            pl.BlockSpec((1, D_MODEL), lambda i, hv: (0, 0)),
            pl.BlockSpec((D_MODEL, half_cols), lambda i, hv: (0, hv)),
            pl.BlockSpec((D_MODEL, LANES), lambda i, hv: (0, 0)),
            pl.BlockSpec((1, LANES), lambda i, hv: (0, 0)),
            pl.BlockSpec(qk_gain.shape, lambda i, hv: (0, 0, 0)),
        ],
        out_specs=[
            pl.BlockSpec((IN_TM, half_cols), lambda i, hv: (i, hv)),
            pl.BlockSpec((IN_TM, LANES), lambda i, hv: (i, 0)),
        ],
        out_shape=[
            jax.ShapeDtypeStruct((m, PROJ_COLS), BF16),
            jax.ShapeDtypeStruct((m, LANES), F32),
        ],
        scratch_shapes=[pltpu.VMEM((2, IN_TM, D_MODEL), BF16)],
        compiler_params=pltpu.CompilerParams(
            dimension_semantics=("arbitrary", "arbitrary"), vmem_limit_bytes=VMEM_LIMIT),
        name="in_proj",
    )(x2, x2, g, w_main, w_dt, dt_bias, qk_gain)


def _bias_kernel(rel_ref, lq1_ref, lk1_ref, lq2_ref, lk2_ref, bias_ref, far_ref, lam_ref, *, tile):
    h = pl.program_id(0)
    row = lax.broadcasted_iota(jnp.int32, (tile, tile), 0)
    col = lax.broadcasted_iota(jnp.int32, (tile, tile), 1)
    max_exact = NUM_BUCKETS // 2
    for t in range(2):
        d = row - col + t * tile
        n = jnp.maximum(d, 0)
        nf = jnp.maximum(n, 1).astype(F32)
        large = max_exact + (jnp.log(nf / max_exact) / math.log(MAX_DISTANCE / max_exact)
                             * (NUM_BUCKETS - max_exact)).astype(jnp.int32)
        large = jnp.minimum(large, NUM_BUCKETS - 1)
        bucket = jnp.where(n < max_exact, n, large)
        bias = jnp.zeros((tile, tile), F32)
        for bk in range(NUM_BUCKETS):
            bias = jnp.where(bucket == bk, rel_ref[bk, h], bias)
        bias = bias * LOG2E
        if t == 0:
            bias = jnp.where(d >= 0, bias, NEG_BIG)
        bias_ref[0, t] = bias
    far_ref[...] = jnp.full(far_ref.shape, rel_ref[NUM_BUCKETS - 1, h] * LOG2E, F32)
    s1 = jnp.sum(lq1_ref[...] * lk1_ref[...], axis=-1, keepdims=True)
    s2 = jnp.sum(lq2_ref[...] * lk2_ref[...], axis=-1, keepdims=True)
    lam = jnp.exp(s1) - jnp.exp(s2) + LAM_INIT
    lam_ref[...] = jnp.broadcast_to(lam, lam_ref.shape)


def _bias_build(rel_bias, lq1, lk1, lq2, lk2, tile):
    vec = pl.BlockSpec((1, ATTN_HEAD_DIM), lambda h: (0, 0))
    return pl.pallas_call(
        functools.partial(_bias_kernel, tile=tile),
        grid=(ATTN_HEADS,),
        in_specs=[pl.BlockSpec(memory_space=pltpu.SMEM), vec, vec, vec, vec],
        out_specs=[
            pl.BlockSpec((1, 2, tile, tile), lambda h: (h, 0, 0, 0)),
            pl.BlockSpec((1, SUBLANES, LANES), lambda h: (h, 0, 0)),
            pl.BlockSpec((SUBLANES, LANES), lambda h: (0, 0)),
        ],
        out_shape=[
            jax.ShapeDtypeStruct((ATTN_HEADS, 2, tile, tile), F32),
            jax.ShapeDtypeStruct((ATTN_HEADS, SUBLANES, LANES), F32),
            jax.ShapeDtypeStruct((SUBLANES, LANES), F32),
        ],
        compiler_params=pltpu.CompilerParams(dimension_semantics=("arbitrary",)),
        name="bias_build",
    )(rel_bias, lq1, lk1, lq2, lk2)


def _attn_kernel(lam_ref, far_ref, q_ref, k_ref, v_ref, bias_ref, g_ref, o_ref, vext_scr,
                 *, tile, nq, heads):
    h0 = pl.program_id(1) * heads
    lam = lam_ref[0]
    lane = lax.broadcasted_iota(jnp.int32, (tile, LANES), 1)
    lo_half = lane < ATTN_HEAD_DIM
    gain = g_ref[...]

    for j in range(heads):
        vals = v_ref[0, :, j * LANES:(j + 1) * LANES]
        vext_scr[j] = jnp.concatenate([vals, jnp.ones_like(vals)], axis=1)

    def scores(j, qi):
        q = q_ref[0, qi * tile:(qi + 1) * tile, j * LANES:(j + 1) * LANES]
        zero = jnp.zeros_like(q)
        q_both = jnp.concatenate([jnp.where(lo_half, q, zero), jnp.where(lo_half, zero, q)], axis=0)
        return _dot_nt(q_both, k_ref[0, 0:(qi + 1) * tile, j * LANES:(j + 1) * LANES])

    def softmax(j, qi, s):
        far = far_ref[h0 + j]
        nk = (qi + 1) * tile
        n_far = nk - min(qi + 1, 2) * tile
        if qi == 0:
            near_bias = bias_ref[j, 0]
        else:
            near_bias = jnp.concatenate([bias_ref[j, 1], bias_ref[j, 0]], axis=1)
        s_near = s[:, n_far:] + jnp.concatenate([near_bias, near_bias], axis=0)
        m = jnp.max(s_near, axis=-1, keepdims=True)
        if n_far:
            s_far = s[:, :n_far]
            m = jnp.maximum(m, jnp.max(s_far, axis=-1, keepdims=True) + far)
            p = jnp.concatenate([jnp.exp2(s_far - (m - far)), jnp.exp2(s_near - m)], axis=1)
        else:
            p = jnp.exp2(s_near - m)
        return p.astype(BF16)

    def weighted_values(j, qi, p):
        pv = _dot(p, vext_scr[j, 0:(qi + 1) * tile, :])
        return pv[:, :LANES] / pv[:, LANES:]

    units = [(j, qi) for qi in reversed(range(nq)) for j in range(heads)]
    n_units = len(units)
    s_q, p_q = {}, {}
    lag_s, lag_v = ATTN_LAGS
    for step in range(n_units + lag_v):
        if step < n_units:
            s_q[step] = scores(*units[step])
        if 0 <= step - lag_s < n_units:
            p_q[step - lag_s] = softmax(*units[step - lag_s], s_q.pop(step - lag_s))
        if 0 <= step - lag_v < n_units:
            j, qi = units[step - lag_v]
            both = weighted_values(j, qi, p_q.pop(step - lag_v))
            o = both[:tile] - lam * both[tile:]
            ms = jnp.mean(o * o, axis=-1, keepdims=True)
            o = o * lax.rsqrt(ms + SUBLN_EPS) * gain
            o_ref[0, qi * tile:(qi + 1) * tile, j * LANES:(j + 1) * LANES] = o.astype(o_ref.dtype)


def _diff_attn(lam, far, proj3, bias, gain):
    b, s, _ = proj3.shape
    tile = bias.shape[-1]
    nq = s // tile
    heads = ATTN_HEADS_PER_STEP
    width = heads * LANES
    qb, kb, vb = OFF_Q // width, OFF_K // width, OFF_V // width
    kern = functools.partial(_attn_kernel, tile=tile, nq=nq, heads=heads)
    return pl.pallas_call(
        kern,
        grid=(b, ATTN_HEADS // heads),
        in_specs=[
            pl.BlockSpec(memory_space=pltpu.SMEM),
            pl.BlockSpec(memory_space=pltpu.SMEM),
            pl.BlockSpec((1, s, width), lambda i, h: (i, 0, qb + h)),
            pl.BlockSpec((1, s, width), lambda i, h: (i, 0, kb + h)),
            pl.BlockSpec((1, s, width), lambda i, h: (i, 0, vb + h)),
            pl.BlockSpec((heads, 2, tile, tile), lambda i, h: (h, 0, 0, 0)),
            pl.BlockSpec((1, LANES), lambda i, h: (0, 0)),
        ],
        out_specs=pl.BlockSpec((1, s, width), lambda i, h: (i, 0, h)),
        out_shape=jax.ShapeDtypeStruct((b, s, V_COLS), BF16),
        scratch_shapes=[pltpu.VMEM((heads, s, 2 * LANES), BF16)],
        compiler_params=pltpu.CompilerParams(
            dimension_semantics=("parallel", "parallel"), vmem_limit_bytes=VMEM_LIMIT),
        name="diff_attn",
    )(lam, far, proj3, proj3, proj3, bias, gain)


def _shift_matrices(taps, length, halo_rows):
    def build(rows, offset):
        t = lax.broadcasted_iota(jnp.int32, (rows, taps * rows), 0)
        c = lax.broadcasted_iota(jnp.int32, (rows, taps * rows), 1)
        tap = c // rows
        return ((c - tap * rows) == offset + t - (taps - 1 - tap)).astype(BF16)

    return build(length, 0), build(halo_rows, halo_rows)


def _conv_silu(x, halo, w_ref, b_ref, shifts, cols):
    shift_main, shift_head = shifts
    hrows = halo.shape[0]
    def weighted(v):
        v3 = v.reshape(v.shape[0] // hrows, hrows, v.shape[1])
        return jnp.concatenate([(v3 * w_ref[k, :, cols][None]).reshape(v.shape)
                                for k in range(w_ref.shape[0])], axis=0)

    acc = b_ref[:, cols] + _dot(shift_main, weighted(x))
    head = _dot(shift_head, weighted(halo))
    acc = jnp.concatenate([acc[0:hrows] + head, acc[hrows:]], axis=0)
    return _silu(acc)


def _ssd_kernel(xs_ref, xh_ref, bm_ref, bh_ref, cm_ref, ch_ref, z_ref, dt_ref, alog_ref,
                wx_ref, bx_ref, wb_ref, bb_ref, wc_ref, bc_ref, dskip_ref, ng_ref, e_ref,
                y_ref, state_scr, *, length, sub_chunks):
    c = pl.program_id(1)
    first = c == 0

    @pl.when(first)
    def _():
        state_scr[...] = jnp.zeros_like(state_scr)

    shift = _shift_matrices(SSM_CONV, length, BF16_SUBLANES)
    a = -jnp.exp(alog_ref[...])
    row = lax.broadcasted_iota(jnp.int32, (length, length), 0)
    col = lax.broadcasted_iota(jnp.int32, (length, length), 1)
    causal = row >= col
    tri = causal.astype(BF16)
    sel2 = e_ref[...]
    lane = lax.broadcasted_iota(jnp.int32, (length, LANES), 1)
    lo_half = lane < SSM_HEAD_DIM

    def conv(k, main_ref, halo_ref, w_ref, b_ref, cols=slice(None)):
        r0 = k * length
        if k == 0:
            halo = halo_ref[0, :, cols]
            halo = jnp.where(first, jnp.zeros_like(halo), halo)
        else:
            halo = main_ref[0, r0 - BF16_SUBLANES:r0, cols]
        return _conv_silu(main_ref[0, r0:r0 + length, cols], halo, w_ref, b_ref, shift, cols)

    def expand(v):
        return _dot(jnp.concatenate(_split(v, 2), axis=1), sel2)

    def prepare(k):
        rows = slice(k * length, (k + 1) * length)
        bm = conv(k, bm_ref, bh_ref, wb_ref, bb_ref).astype(BF16)
        cm = conv(k, cm_ref, ch_ref, wc_ref, bc_ref).astype(BF16)
        xs = [conv(k, xs_ref, xh_ref, wx_ref, bx_ref, slice(g * GROUP_COLS, (g + 1) * GROUP_COLS))
              for g in range(SSM_GROUPS)]
        dt = dt_ref[0, rows, :]
        cum = _dot_exact_lhs(tri, _split(dt * a, 3))
        cum2 = cum * LOG2E
        src_t = cum2.T - jnp.log2(dt.T)
        total = cum[length - 1:length, :]
        ecum_x = expand(jnp.exp(cum))
        edtw_x = expand(dt * jnp.exp(total - cum))
        return bm, cm, xs, cum2, src_t, ecum_x, edtw_x

    def scan(k, prepared):
        bm, cm, xs, cum2, src_t, ecum_x, edtw_x = prepared
        rows = slice(k * length, (k + 1) * length)
        etot_x = ecum_x[length - 1:length, :]
        for g in range(SSM_GROUPS):
            gs = slice(g * SSM_STATE, (g + 1) * SSM_STATE)
            cs = slice(g * GROUP_COLS, (g + 1) * GROUP_COLS)
            xs_f = xs[g]
            b_g = bm[:, gs]
            c_g = cm[:, gs]
            xs_g = xs_f.astype(BF16)
            cb = _dot_nt(c_g, b_g)
            st = state_scr[g]
            y_g = _dot(c_g, st.astype(BF16)) * ecum_x[:, cs]
            xw = (xs_f * edtw_x[:, cs]).astype(BF16)
            b_t = b_g.astype(F32).T.astype(BF16)
            state_scr[g] = st * etot_x[:, cs] + _dot(b_t, xw)
            y_parts = []
            for pair in range(SSM_HEADS_PER_GROUP // 2):
                ms = []
                for hh in range(2):
                    h = g * SSM_HEADS_PER_GROUP + pair * 2 + hh
                    seg = jnp.where(causal, cum2[:, h:h + 1] - src_t[h:h + 1, :], NEG_BIG)
                    ms.append((cb * jnp.exp2(seg)).astype(BF16))
                x_pair = xs_g[:, pair * LANES:(pair + 1) * LANES]
                zero = jnp.zeros_like(x_pair)
                rhs = jnp.concatenate([jnp.where(lo_half, x_pair, zero),
                                       jnp.where(lo_half, zero, x_pair)], axis=0)
                y_parts.append(_dot(jnp.concatenate(ms, axis=1), rhs))
            y_g = y_g + jnp.concatenate(y_parts, axis=1)
            y_g = y_g + xs_f * dskip_ref[:, cs]
            y_g = y_g * _silu(z_ref[0, rows, cs].astype(F32))
            msq = jnp.mean(y_g * y_g, axis=-1, keepdims=True)
            y_g = y_g * lax.rsqrt(msq + SUBLN_EPS) * ng_ref[:, cs]
            y_ref[0, rows, cs] = y_g.astype(y_ref.dtype)

    prepared = prepare(0)
    for k in range(sub_chunks):
        current = prepared
        if k + 1 < sub_chunks:
            prepared = prepare(k + 1)
        scan(k, current)


def _ssd(proj3, dt3, alog, wx, bx, wb, bb, wc, bc, dskip_x, ng, sel):
    b, s, _ = proj3.shape
    length = SSD_L
    rows = SSD_SUB * length
    nc = s // rows
    hb = rows // BF16_SUBLANES
    xs_blk, z_blk = OFF_XS // SSM_D_INNER, OFF_Z // SSM_D_INNER
    b_blk, c_blk = OFF_B // BC_COLS, OFF_C // BC_COLS

    def main(width, blk):
        return pl.BlockSpec((1, rows, width), lambda i, c: (i, c, blk))

    def halo(width, blk):
        return pl.BlockSpec((1, BF16_SUBLANES, width),
                            lambda i, c: (i, jnp.maximum(c * hb - 1, 0), blk))

    def const(shape):
        return pl.BlockSpec(shape, lambda i, c: (0,) * len(shape))

    kern = functools.partial(_ssd_kernel, length=length, sub_chunks=SSD_SUB)
    return pl.pallas_call(
        kern,
        grid=(b, nc),
        in_specs=[
            main(SSM_D_INNER, xs_blk), halo(SSM_D_INNER, xs_blk),
            main(BC_COLS, b_blk), halo(BC_COLS, b_blk),
            main(BC_COLS, c_blk), halo(BC_COLS, c_blk),
            main(SSM_D_INNER, z_blk),
            pl.BlockSpec((1, rows, LANES), lambda i, c: (i, c, 0)),
            const((1, LANES)),
            const((SSM_CONV, BF16_SUBLANES, SSM_D_INNER)), const((1, SSM_D_INNER)),
            const((SSM_CONV, BF16_SUBLANES, BC_COLS)), const((1, BC_COLS)),
            const((SSM_CONV, BF16_SUBLANES, BC_COLS)), const((1, BC_COLS)),
            const((1, SSM_D_INNER)), const((1, SSM_D_INNER)),
            const((2 * LANES, SSM_D_INNER)),
        ],
        out_specs=pl.BlockSpec((1, rows, SSM_D_INNER), lambda i, c: (i, c, 0)),
        out_shape=jax.ShapeDtypeStruct((b, s, SSM_D_INNER), BF16),
        scratch_shapes=[pltpu.VMEM((SSM_GROUPS, SSM_STATE, GROUP_COLS), F32)],
        compiler_params=pltpu.CompilerParams(
            dimension_semantics=("parallel", "arbitrary"), vmem_limit_bytes=VMEM_LIMIT),
        name="ssd",
    )(proj3, proj3, proj3, proj3, proj3, proj3, proj3, dt3, alog,
      wx, bx, wb, bb, wc, bc, dskip_x, ng, sel)


def _merge_kernel(x_ref, ya_ref, ys_ref, ga_ref, gs_ref, wpa_ref, wps_ref, wo_ref, g_ref,
                  x1_ref, h2_ref):
    pa = _dot(ya_ref[...], wpa_ref[...])
    ps = _dot(ys_ref[...], wps_ref[...])
    ga = 1.0 / (1.0 + jnp.exp(-ga_ref[...].astype(F32)))
    gs = 1.0 / (1.0 + jnp.exp(-gs_ref[...].astype(F32)))
    mixed = (ga * pa + gs * ps).astype(BF16)
    x1 = x_ref[...] + _dot(mixed, wo_ref[...])
    x1_ref[...] = x1
    ms = jnp.mean(x1 * x1, axis=-1, keepdims=True)
    h2_ref[...] = (x1 * lax.rsqrt(ms + RMS_EPS) * g_ref[...]).astype(BF16)


def _merge(x2, y_attn, y_ssm, proj, wpa, wps, wo, g):
    m = x2.shape[0]
    tm = MERGE_TM
    ga_blk = OFF_GATE // D_MODEL

    def rows(width, blk=0):
        return pl.BlockSpec((tm, width), lambda i: (i, blk))

    def const(shape):
        return pl.BlockSpec(shape, lambda i: (0, 0))

    return pl.pallas_call(
        _merge_kernel,
        grid=(m // tm,),
        in_specs=[
            rows(D_MODEL), rows(V_COLS), rows(SSM_D_INNER),
            rows(D_MODEL, ga_blk), rows(D_MODEL, ga_blk + 1),
            const((V_COLS, D_MODEL)), const((SSM_D_INNER, D_MODEL)), const((D_MODEL, D_MODEL)),
            const((1, D_MODEL)),
        ],
        out_specs=[rows(D_MODEL), rows(D_MODEL)],
        out_shape=[jax.ShapeDtypeStruct((m, D_MODEL), F32),
                   jax.ShapeDtypeStruct((m, D_MODEL), BF16)],
        compiler_params=pltpu.CompilerParams(
            dimension_semantics=("parallel",), vmem_limit_bytes=VMEM_LIMIT),
        name="merge",
    )(x2, y_attn, y_ssm, proj, proj, wpa, wps, wo, g)


def _ffn_kernel(x1_ref, h_ref, hh_ref, wup_ref, cw_ref, cb_ref, wd_ref, o_ref, gscr, vscr,
                *, tm, tiles_per_seq):
    i = pl.program_id(0)
    first = (i % tiles_per_seq) == 0
    halo = hh_ref[...]
    halo = jnp.where(first, jnp.zeros_like(halo), halo)
    hrows = halo.shape[0]
    h_ext = jnp.concatenate([halo, h_ref[...]], axis=0)
    taps = cw_ref.shape[0]

    slots = gscr.shape[0]

    def up_project(idx):
        start, width = FFN_SPLITS[idx]
        for col0, scr in ((start, gscr), (D_FF + start, vscr)):
            scr[idx % slots, :, 0:width] = _dot(h_ext, wup_ref[:, col0:col0 + width])

    def conv(idx, col0, scr):
        width = FFN_SPLITS[idx][1]
        cols = slice(col0, col0 + width)
        slot = idx % slots
        out = cb_ref[:, cols] + cw_ref[taps - 1:taps, cols] * scr[slot, hrows:hrows + tm, 0:width]
        for k in range(taps - 1):
            off = hrows - (taps - 1) + k
            out = out + cw_ref[k:k + 1, cols] * scr[slot, off:off + tm, 0:width]
        return out

    def gated(idx):
        start = FFN_SPLITS[idx][0]
        return (_silu(conv(idx, start, gscr)) * conv(idx, D_FF + start, vscr)).astype(BF16)

    n_chunks = len(FFN_SPLITS)
    ahead = slots - 1
    acc = x1_ref[...]
    acts = {}
    for idx in range(min(ahead, n_chunks)):
        up_project(idx)
    for idx in range(n_chunks + 1):
        if idx + ahead < n_chunks:
            up_project(idx + ahead)
        if idx < n_chunks:
            acts[idx] = gated(idx)
        if idx >= 1:
            start, width = FFN_SPLITS[idx - 1]
            acc = acc + _dot(acts.pop(idx - 1), wd_ref[start:start + width, :])

    o_ref[...] = acc


def _ffn(x1, h2, w_up, cw, cb, w_down, seq):
    m = x1.shape[0]
    tm = FFN_TM
    hb = tm // BF16_SUBLANES
    kern = functools.partial(_ffn_kernel, tm=tm, tiles_per_seq=seq // tm)

    def resident(shape):
        return pl.BlockSpec(shape, lambda i: (0, 0), pipeline_mode=pl.Buffered(1))

    return pl.pallas_call(
        kern,
        grid=(m // tm,),
        in_specs=[
            pl.BlockSpec((tm, D_MODEL), lambda i: (i, 0)),
            pl.BlockSpec((tm, D_MODEL), lambda i: (i, 0)),
            pl.BlockSpec((BF16_SUBLANES, D_MODEL), lambda i: (jnp.maximum(i * hb - 1, 0), 0)),
            resident((D_MODEL, 2 * D_FF)),
            resident((FFN_CONV, 2 * D_FF)),
            resident((1, 2 * D_FF)),
            resident((D_FF, D_MODEL)),
        ],
        out_specs=pl.BlockSpec((tm, D_MODEL), lambda i: (i, 0)),
        out_shape=jax.ShapeDtypeStruct((m, D_MODEL), F32),
        scratch_shapes=[
            pltpu.VMEM((FFN_SLOTS, tm + BF16_SUBLANES, FFN_SCR), F32),
            pltpu.VMEM((FFN_SLOTS, tm + BF16_SUBLANES, FFN_SCR), F32),
        ],
        compiler_params=pltpu.CompilerParams(
            dimension_semantics=("parallel",), vmem_limit_bytes=VMEM_LIMIT),
        name="ffn",
    )(x1, h2, h2, w_up, cw, cb, w_down)


def _head_expand(rows, cols, chunk):
    r = lax.broadcasted_iota(jnp.int32, (rows, cols), 0)
    c = lax.broadcasted_iota(jnp.int32, (rows, cols), 1) // chunk
    return (r == c).astype(BF16)


def _layer(x, rel_bias, norm_mix_g, w_in, q_norm_g, k_norm_g, lq1, lk1, lq2, lk2, attn_subln_g,
           conv_ssm_w, conv_ssm_b, dt_bias, a_log, d_skip, ssm_norm_g, w_proj_attn, w_proj_ssm,
           w_out, norm_ffn_g, w_up, conv_ffn_w, conv_ffn_b, w_down):
    b, s, d = x.shape
    m = b * s
    x2 = x.reshape(m, d)

    o_dt = sum(width for name, width in W_SEGMENTS if name != "gate")
    w_main = jnp.concatenate([w_in[:, :o_dt].astype(BF16),
                              w_in[:, o_dt + SSM_HEADS:].astype(BF16)], axis=1)
    pad_h = LANES - SSM_HEADS
    w_dt = jnp.pad(w_in[:, o_dt:o_dt + SSM_HEADS], ((0, 0), (0, pad_h))).astype(BF16)
    dt_b = jnp.pad(dt_bias.astype(F32), (0, pad_h)).reshape(1, LANES)
    reps = Q_COLS // ATTN_HEAD_DIM
    qk_gain = jnp.stack([
        jnp.tile(q_norm_g.astype(F32) * (ATTN_HEAD_DIM ** -0.5 * LOG2E), reps),
        jnp.tile(k_norm_g.astype(F32), reps)]).reshape(2, 1, Q_COLS)

    proj, dt = _in_proj(x2, norm_mix_g.astype(F32).reshape(1, d), w_main, w_dt, dt_b, qk_gain)
    proj3 = proj.reshape(b, s, PROJ_COLS)

    vec = lambda v: v.astype(F32).reshape(1, -1)
    bias, far, lam = _bias_build(rel_bias.astype(F32), vec(lq1), vec(lk1), vec(lq2), vec(lk2), ATTN_T)
    y_attn = _diff_attn(lam[0, :1], far[:, 0, 0], proj3, bias,
                        vec(attn_subln_g) * (1.0 - LAM_INIT))

    cw = jnp.broadcast_to(conv_ssm_w.astype(BF16)[:, None, :],
                          (SSM_CONV, BF16_SUBLANES, conv_ssm_w.shape[-1]))
    cbias = conv_ssm_b.astype(F32).reshape(1, -1)
    xs_sl = slice(0, SSM_D_INNER)
    b_sl = slice(SSM_D_INNER, SSM_D_INNER + BC_COLS)
    c_sl = slice(SSM_D_INNER + BC_COLS, SSM_D_INNER + 2 * BC_COLS)
    y_ssm = _ssd(
        proj3, dt.reshape(b, s, LANES),
        jnp.pad(a_log.astype(F32), (0, pad_h)).reshape(1, LANES),
        cw[..., xs_sl], cbias[:, xs_sl], cw[..., b_sl], cbias[:, b_sl], cw[..., c_sl], cbias[:, c_sl],
        jnp.repeat(d_skip.astype(F32), SSM_HEAD_DIM).reshape(1, SSM_D_INNER),
        vec(ssm_norm_g), jnp.tile(_head_expand(LANES, SSM_D_INNER, SSM_HEAD_DIM), (2, 1)))

    x1, h2 = _merge(x2, y_attn.reshape(m, V_COLS), y_ssm.reshape(m, SSM_D_INNER), proj,
                    w_proj_attn.astype(BF16), w_proj_ssm.astype(BF16), w_out.astype(BF16),
                    vec(norm_ffn_g))

    out = _ffn(x1, h2, w_up.astype(BF16), conv_ffn_w.astype(F32),
               conv_ffn_b.astype(F32).reshape(1, -1), w_down.astype(BF16), s)
    return out.reshape(b, s, d)


def kernel(x, rel_bias, norm_mix_g, w_in, q_norm_g, k_norm_g, lambda_q1, lambda_k1, lambda_q2,
           lambda_k2, attn_subln_g, conv_ssm_w, conv_ssm_b, dt_bias, a_log, d_skip, ssm_norm_g,
           w_proj_attn, w_proj_ssm, w_out, norm_ffn_g, w_up, conv_ffn_w, conv_ffn_b, w_down):
    depth = w_in.shape[0]
    assert depth == 1, "lambda_init and the stacked-parameter layout are specialised to depth 1"
    return _layer(x, rel_bias, norm_mix_g[0], w_in[0], q_norm_g[0], k_norm_g[0], lambda_q1[0],
                  lambda_k1[0], lambda_q2[0], lambda_k2[0], attn_subln_g[0], conv_ssm_w[0],
                  conv_ssm_b[0], dt_bias[0], a_log[0], d_skip[0], ssm_norm_g[0], w_proj_attn[0],
                  w_proj_ssm[0], w_out[0], norm_ffn_g[0], w_up[0], conv_ffn_w[0], conv_ffn_b[0],
                  w_down[0])
```

```python
import functools
import math

import jax
import jax.numpy as jnp
from jax import lax
from jax.experimental import pallas as pl
from jax.experimental.pallas import tpu as pltpu

F32 = jnp.float32
BF16 = jnp.bfloat16

D_MODEL = 1024
ATTN_HEADS = 8
ATTN_HEAD_DIM = 64
ATTN_V_DIM = 2 * ATTN_HEAD_DIM
NUM_BUCKETS = 32
MAX_DISTANCE = 128
SSM_D_INNER = 2048
SSM_HEAD_DIM = 64
SSM_HEADS = 32
SSM_GROUPS = 4
SSM_HEADS_PER_GROUP = SSM_HEADS // SSM_GROUPS
SSM_STATE = 128
SSM_CONV = 4
D_FF = 2816
FFN_CONV = 3
RMS_EPS = 1e-6
SUBLN_EPS = 1e-5
LAM_INIT = 0.8 - 0.6 * math.exp(0.0)

Q_COLS = ATTN_HEADS * 2 * ATTN_HEAD_DIM
V_COLS = ATTN_HEADS * ATTN_V_DIM
BC_COLS = SSM_GROUPS * SSM_STATE
GROUP_COLS = SSM_D_INNER // SSM_GROUPS

LANES = 128
SUBLANES = 8
BF16_SUBLANES = 16
VMEM_LIMIT = 56 * 1024 * 1024

LOG2E = math.log2(math.e)
NEG_BIG = -1e30

W_SEGMENTS = (("q", Q_COLS), ("k", Q_COLS), ("v", V_COLS), ("z", SSM_D_INNER), ("xs", SSM_D_INNER),
              ("b", BC_COLS), ("c", BC_COLS), ("gate", 2 * D_MODEL))
OFF_Z = 0
OFF_Q = OFF_Z + SSM_D_INNER
OFF_K = OFF_Q + Q_COLS
OFF_V = OFF_K + Q_COLS
OFF_B = OFF_V + V_COLS
OFF_C = OFF_B + BC_COLS
OFF_XS = OFF_C + BC_COLS
OFF_GATE = OFF_XS + SSM_D_INNER
PROJ_COLS = OFF_GATE + 2 * D_MODEL
OUT_OFFSETS = {"q": OFF_Q, "k": OFF_K, "v": OFF_V, "z": OFF_Z, "xs": OFF_XS, "b": OFF_B,
               "c": OFF_C, "gate": OFF_GATE}

IN_TM = 512
IN_TN = 512
IN_HALVES = 2
ATTN_T = 128
ATTN_HEADS_PER_STEP = 2
ATTN_LAGS = (2, 5)
SSD_L = 128
SSD_SUB = 4
MERGE_TM = 512
FFN_TM = 512
FFN_SPLITS = ((0, 512), (512, 512), (1024, 512), (1536, 512), (2048, 512), (2560, 256))
FFN_SCR = max(w for _, w in FFN_SPLITS)
FFN_SLOTS = 3


def _split(v, parts):
    out = []
    for _ in range(parts - 1):
        hi = v.astype(BF16)
        out.append(hi)
        v = v - hi.astype(F32)
    out.append(v.astype(BF16))
    return out


def _dot(a, b):
    return jnp.dot(a, b, preferred_element_type=F32)


def _dot_nt(a, b):
    return lax.dot_general(a, b, (((1,), (1,)), ((), ())), preferred_element_type=F32)


def _dot_exact_lhs(sel, parts):
    out = _dot(sel, parts[0])
    for p in parts[1:]:
        out = out + _dot(sel, p)
    return out


def _silu(v):
    h = 0.5 * v
    return h + h * jnp.tanh(h)


def _in_proj_kernel(x0_ref, xn_ref, g_ref, wdt_ref, dtb_ref, qkg_ref, *rest, tile_plan):
    w_refs, (o_ref, dt_ref, h_scr) = rest[:len(tile_plan)], rest[len(tile_plan):]
    i = pl.program_id(0)
    half = pl.program_id(1)
    slot = i % 2

    def normed(x):
        ms = jnp.mean(x * x, axis=-1, keepdims=True)
        return (x * lax.rsqrt(ms + RMS_EPS) * g_ref[...]).astype(BF16)

    @pl.when(jnp.logical_and(i == 0, half == 0))
    def _():
        h_scr[0] = normed(x0_ref[...])

    h = h_scr[slot]

    def tile_cols(t):
        return slice(t * IN_TN, (t + 1) * IN_TN)

    def store_plain(out0, acc):
        o_ref[:, out0:out0 + IN_TN] = acc.astype(BF16)

    def store_head_normed(out0, acc, gain_row, gain_col0):
        lo_half = lax.broadcasted_iota(jnp.int32, (IN_TM, LANES), 1) < ATTN_HEAD_DIM
        for cblk in range(IN_TN // LANES):
            a = acc[:, cblk * LANES:(cblk + 1) * LANES]
            sq = a * a
            s_lo = jnp.sum(jnp.where(lo_half, sq, 0.0), axis=-1, keepdims=True)
            s_hi = jnp.sum(jnp.where(lo_half, 0.0, sq), axis=-1, keepdims=True)
            ssum = jnp.where(lo_half, s_lo, s_hi)
            y = a * lax.rsqrt(ssum * (1.0 / ATTN_HEAD_DIM) + RMS_EPS)
            y = y * qkg_ref[gain_row, :, gain_col0 + cblk * LANES:gain_col0 + (cblk + 1) * LANES]
            o_ref[:, out0 + cblk * LANES:out0 + (cblk + 1) * LANES] = y.astype(BF16)

    for hv, plan in enumerate(tile_plan):
        @pl.when(half == hv)
        def _(plan=plan, hv=hv):
            w_ref = w_refs[hv]
            acc_next = _dot(h, w_ref[:, tile_cols(0)])
            if hv == 0:
                t = _dot(h, wdt_ref[...]) + dtb_ref[...]
                dt_ref[...] = jnp.maximum(t, 0.0) + jnp.log1p(jnp.exp(-jnp.abs(t)))
            if hv == len(tile_plan) - 1:
                h_scr[1 - slot] = normed(xn_ref[...])
            for t, (out0, head_norm) in enumerate(plan):
                acc = acc_next
                if t + 1 < len(plan):
                    acc_next = _dot(h, w_ref[:, tile_cols(t + 1)])
                if head_norm is None:
                    store_plain(out0, acc)
                else:
                    store_head_normed(out0, acc, *head_norm)


def _in_proj_plan():
    half_cols = PROJ_COLS // IN_HALVES
    gain_rows = {"q": 0, "k": 1}
    plan = [[] for _ in range(IN_HALVES)]
    w_off = 0
    for name, width in W_SEGMENTS:
        for col in range(0, width, IN_TN):
            hv, out_col = (w_off + col) // half_cols, OUT_OFFSETS[name] + col
            assert out_col // half_cols == hv and width % IN_TN == 0
            head_norm = (gain_rows[name], col) if name in gain_rows else None
            plan[hv].append((out_col - hv * half_cols, head_norm))
        w_off += width
    return tuple(tuple(p) for p in plan)


def _in_proj(x2, g, w_halves, w_dt, dt_bias, qk_gain):
    m = x2.shape[0]
    n_tiles = m // IN_TM
    half_cols = PROJ_COLS // IN_HALVES
    kern = functools.partial(_in_proj_kernel, tile_plan=_in_proj_plan())
    return pl.pallas_call(
        kern,
        grid=(m // IN_TM, IN_HALVES),
        in_specs=[
            pl.BlockSpec((IN_TM, D_MODEL), lambda i, hv: (0, 0)),
            pl.BlockSpec((IN_TM, D_MODEL), lambda i, hv: (jnp.minimum(i + 1, n_tiles - 1), 0)),
            pl.BlockSpec((1, D_MODEL), lambda i, hv: (0, 0)),
            pl.BlockSpec((D_MODEL, LANES), lambda i, hv: (0, 0)),
            pl.BlockSpec((1, LANES), lambda i, hv: (0, 0)),
            pl.BlockSpec(qk_gain.shape, lambda i, hv: (0, 0, 0)),
        ] + [pl.BlockSpec((D_MODEL, half_cols), lambda i, hv: (0, 0), pipeline_mode=pl.Buffered(1))
             for _ in w_halves],
        out_specs=[
            pl.BlockSpec((IN_TM, half_cols), lambda i, hv: (i, hv)),
            pl.BlockSpec((IN_TM, LANES), lambda i, hv: (i, 0)),
        ],
        out_shape=[
            jax.ShapeDtypeStruct((m, PROJ_COLS), BF16),
            jax.ShapeDtypeStruct((m, LANES), F32),
        ],
        scratch_shapes=[pltpu.VMEM((2, IN_TM, D_MODEL), BF16)],
        compiler_params=pltpu.CompilerParams(
            dimension_semantics=("arbitrary", "arbitrary"), vmem_limit_bytes=VMEM_LIMIT),
        name="in_proj",
    )(x2, x2, g, w_dt, dt_bias, qk_gain, *w_halves)


def _bias_kernel(rel_ref, lq1_ref, lk1_ref, lq2_ref, lk2_ref, bias_ref, far_ref, lam_ref, *, tile):
    h = pl.program_id(0)
    row = lax.broadcasted_iota(jnp.int32, (tile, tile), 0)
    col = lax.broadcasted_iota(jnp.int32, (tile, tile), 1)
    max_exact = NUM_BUCKETS // 2
    for t in range(2):
        d = row - col + t * tile
        n = jnp.maximum(d, 0)
        nf = jnp.maximum(n, 1).astype(F32)
        large = max_exact + (jnp.log(nf / max_exact) / math.log(MAX_DISTANCE / max_exact)
                             * (NUM_BUCKETS - max_exact)).astype(jnp.int32)
        large = jnp.minimum(large, NUM_BUCKETS - 1)
        bucket = jnp.where(n < max_exact, n, large)
        bias = jnp.zeros((tile, tile), F32)
        for bk in range(NUM_BUCKETS):
            bias = jnp.where(bucket == bk, rel_ref[bk, h], bias)
        bias = bias * LOG2E
        if t == 0:
            bias = jnp.where(d >= 0, bias, NEG_BIG)
        bias_ref[0, t] = bias
    far_ref[...] = jnp.full(far_ref.shape, rel_ref[NUM_BUCKETS - 1, h] * LOG2E, F32)
    s1 = jnp.sum(lq1_ref[...] * lk1_ref[...], axis=-1, keepdims=True)
    s2 = jnp.sum(lq2_ref[...] * lk2_ref[...], axis=-1, keepdims=True)
    lam = jnp.exp(s1) - jnp.exp(s2) + LAM_INIT
    lam_ref[...] = jnp.broadcast_to(lam, lam_ref.shape)


def _bias_build(rel_bias, lq1, lk1, lq2, lk2, tile):
    vec = pl.BlockSpec((1, ATTN_HEAD_DIM), lambda h: (0, 0))
    return pl.pallas_call(
        functools.partial(_bias_kernel, tile=tile),
        grid=(ATTN_HEADS,),
        in_specs=[pl.BlockSpec(memory_space=pltpu.SMEM), vec, vec, vec, vec],
        out_specs=[
            pl.BlockSpec((1, 2, tile, tile), lambda h: (h, 0, 0, 0)),
            pl.BlockSpec((1, SUBLANES, LANES), lambda h: (h, 0, 0)),
            pl.BlockSpec((SUBLANES, LANES), lambda h: (0, 0)),
        ],
        out_shape=[
            jax.ShapeDtypeStruct((ATTN_HEADS, 2, tile, tile), F32),
            jax.ShapeDtypeStruct((ATTN_HEADS, SUBLANES, LANES), F32),
            jax.ShapeDtypeStruct((SUBLANES, LANES), F32),
        ],
        compiler_params=pltpu.CompilerParams(dimension_semantics=("arbitrary",)),
        name="bias_build",
    )(rel_bias, lq1, lk1, lq2, lk2)


def _attn_kernel(lam_ref, far_ref, q_ref, k_ref, v_ref, bias_ref, g_ref, o_ref, vext_scr,
                 *, tile, nq, heads):
    h0 = pl.program_id(1) * heads
    lam = lam_ref[0]
    lane = lax.broadcasted_iota(jnp.int32, (tile, LANES), 1)
    lo_half = lane < ATTN_HEAD_DIM
    gain = g_ref[...]

    for j in range(heads):
        vals = v_ref[0, :, j * LANES:(j + 1) * LANES]
        vext_scr[j] = jnp.concatenate([vals, jnp.ones_like(vals)], axis=1)

    def scores(j, qi):
        q = q_ref[0, qi * tile:(qi + 1) * tile, j * LANES:(j + 1) * LANES]
        zero = jnp.zeros_like(q)
        q_both = jnp.concatenate([jnp.where(lo_half, q, zero), jnp.where(lo_half, zero, q)], axis=0)
        return _dot_nt(q_both, k_ref[0, 0:(qi + 1) * tile, j * LANES:(j + 1) * LANES])

    def softmax(j, qi, s):
        far = far_ref[h0 + j]
        nk = (qi + 1) * tile
        n_far = nk - min(qi + 1, 2) * tile
        if qi == 0:
            near_bias = bias_ref[j, 0]
        else:
            near_bias = jnp.concatenate([bias_ref[j, 1], bias_ref[j, 0]], axis=1)
        s_near = s[:, n_far:] + jnp.concatenate([near_bias, near_bias], axis=0)
        m = jnp.max(s_near, axis=-1, keepdims=True)
        if n_far:
            s_far = s[:, :n_far]
            m = jnp.maximum(m, jnp.max(s_far, axis=-1, keepdims=True) + far)
            p = jnp.concatenate([jnp.exp2(s_far - (m - far)), jnp.exp2(s_near - m)], axis=1)
        else:
            p = jnp.exp2(s_near - m)
        return p.astype(BF16)

    def weighted_values(j, qi, p):
        pv = _dot(p, vext_scr[j, 0:(qi + 1) * tile, :])
        return pv[:, :LANES] / pv[:, LANES:]

    units = [(j, qi) for qi in reversed(range(nq)) for j in range(heads)]
    n_units = len(units)
    s_q, p_q = {}, {}
    lag_s, lag_v = ATTN_LAGS
    for step in range(n_units + lag_v):
        if step < n_units:
            s_q[step] = scores(*units[step])
        if 0 <= step - lag_s < n_units:
            p_q[step - lag_s] = softmax(*units[step - lag_s], s_q.pop(step - lag_s))
        if 0 <= step - lag_v < n_units:
            j, qi = units[step - lag_v]
            both = weighted_values(j, qi, p_q.pop(step - lag_v))
            o = both[:tile] - lam * both[tile:]
            ms = jnp.mean(o * o, axis=-1, keepdims=True)
            o = o * lax.rsqrt(ms + SUBLN_EPS) * gain
            o_ref[0, qi * tile:(qi + 1) * tile, j * LANES:(j + 1) * LANES] = o.astype(o_ref.dtype)


def _diff_attn(lam, far, proj3, bias, gain):
    b, s, _ = proj3.shape
    tile = bias.shape[-1]
    nq = s // tile
    heads = ATTN_HEADS_PER_STEP
    width = heads * LANES
    qb, kb, vb = OFF_Q // width, OFF_K // width, OFF_V // width
    kern = functools.partial(_attn_kernel, tile=tile, nq=nq, heads=heads)
    return pl.pallas_call(
        kern,
        grid=(b, ATTN_HEADS // heads),
        in_specs=[
            pl.BlockSpec(memory_space=pltpu.SMEM),
            pl.BlockSpec(memory_space=pltpu.SMEM),
            pl.BlockSpec((1, s, width), lambda i, h: (i, 0, qb + h)),
            pl.BlockSpec((1, s, width), lambda i, h: (i, 0, kb + h)),
            pl.BlockSpec((1, s, width), lambda i, h: (i, 0, vb + h)),
            pl.BlockSpec((heads, 2, tile, tile), lambda i, h: (h, 0, 0, 0)),
            pl.BlockSpec((1, LANES), lambda i, h: (0, 0)),
        ],
        out_specs=pl.BlockSpec((1, s, width), lambda i, h: (i, 0, h)),
        out_shape=jax.ShapeDtypeStruct((b, s, V_COLS), BF16),
        scratch_shapes=[pltpu.VMEM((heads, s, 2 * LANES), BF16)],
        compiler_params=pltpu.CompilerParams(
            dimension_semantics=("parallel", "parallel"), vmem_limit_bytes=VMEM_LIMIT),
        name="diff_attn",
    )(lam, far, proj3, proj3, proj3, bias, gain)


def _shift_matrices(taps, length, halo_rows):
    def build(rows, offset):
        t = lax.broadcasted_iota(jnp.int32, (rows, taps * rows), 0)
        c = lax.broadcasted_iota(jnp.int32, (rows, taps * rows), 1)
        tap = c // rows
        return ((c - tap * rows) == offset + t - (taps - 1 - tap)).astype(BF16)

    return build(length, 0), build(halo_rows, halo_rows)


def _conv_silu(x, halo, w_ref, b_ref, shifts, cols):
    shift_main, shift_head = shifts
    hrows = halo.shape[0]
    def weighted(v):
        v3 = v.reshape(v.shape[0] // hrows, hrows, v.shape[1])
        return jnp.concatenate([(v3 * w_ref[k, :, cols][None]).reshape(v.shape)
                                for k in range(w_ref.shape[0])], axis=0)

    acc = b_ref[:, cols] + _dot(shift_main, weighted(x))
    head = _dot(shift_head, weighted(halo))
    acc = jnp.concatenate([acc[0:hrows] + head, acc[hrows:]], axis=0)
    return _silu(acc)


def _ssd_kernel(xs_ref, xh_ref, bm_ref, bh_ref, cm_ref, ch_ref, z_ref, dt_ref, alog_ref,
                wx_ref, bx_ref, wb_ref, bb_ref, wc_ref, bc_ref, dskip_ref, ng_ref, e_ref,
                y_ref, state_scr, *, length, sub_chunks):
    c = pl.program_id(1)
    first = c == 0

    @pl.when(first)
    def _():
        state_scr[...] = jnp.zeros_like(state_scr)

    shift = _shift_matrices(SSM_CONV, length, BF16_SUBLANES)
    a = -jnp.exp(alog_ref[...])
    row = lax.broadcasted_iota(jnp.int32, (length, length), 0)
    col = lax.broadcasted_iota(jnp.int32, (length, length), 1)
    causal = row >= col
    tri = causal.astype(BF16)
    sel2 = e_ref[...]
    lane = lax.broadcasted_iota(jnp.int32, (length, LANES), 1)
    lo_half = lane < SSM_HEAD_DIM

    def conv(k, main_ref, halo_ref, w_ref, b_ref, cols=slice(None)):
        r0 = k * length
        if k == 0:
            halo = halo_ref[0, :, cols]
            halo = jnp.where(first, jnp.zeros_like(halo), halo)
        else:
            halo = main_ref[0, r0 - BF16_SUBLANES:r0, cols]
        return _conv_silu(main_ref[0, r0:r0 + length, cols], halo, w_ref, b_ref, shift, cols)

    def expand(v):
        return _dot(jnp.concatenate(_split(v, 2), axis=1), sel2)

    def prepare(k):
        rows = slice(k * length, (k + 1) * length)
        bm = conv(k, bm_ref, bh_ref, wb_ref, bb_ref).astype(BF16)
        cm = conv(k, cm_ref, ch_ref, wc_ref, bc_ref).astype(BF16)
        xs = [conv(k, xs_ref, xh_ref, wx_ref, bx_ref, slice(g * GROUP_COLS, (g + 1) * GROUP_COLS))
              for g in range(SSM_GROUPS)]
        dt = dt_ref[0, rows, :]
        cum = _dot_exact_lhs(tri, _split(dt * a, 3))
        cum2 = cum * LOG2E
        src_t = cum2.T - jnp.log2(dt.T)
        total = cum[length - 1:length, :]
        ecum_x = expand(jnp.exp(cum))
        edtw_x = expand(dt * jnp.exp(total - cum))
        return bm, cm, xs, cum2, src_t, ecum_x, edtw_x

    def scan(k, prepared):
        bm, cm, xs, cum2, src_t, ecum_x, edtw_x = prepared
        rows = slice(k * length, (k + 1) * length)
        etot_x = ecum_x[length - 1:length, :]
        for g in range(SSM_GROUPS):
            gs = slice(g * SSM_STATE, (g + 1) * SSM_STATE)
            cs = slice(g * GROUP_COLS, (g + 1) * GROUP_COLS)
            xs_f = xs[g]
            b_g = bm[:, gs]
            c_g = cm[:, gs]
            xs_g = xs_f.astype(BF16)
            cb = _dot_nt(c_g, b_g)
            st = state_scr[g]
            y_g = _dot(c_g, st.astype(BF16)) * ecum_x[:, cs]
            xw = (xs_f * edtw_x[:, cs]).astype(BF16)
            b_t = b_g.astype(F32).T.astype(BF16)
            state_scr[g] = st * etot_x[:, cs] + _dot(b_t, xw)
            y_parts = []
            for pair in range(SSM_HEADS_PER_GROUP // 2):
                ms = []
                for hh in range(2):
                    h = g * SSM_HEADS_PER_GROUP + pair * 2 + hh
                    seg = jnp.where(causal, cum2[:, h:h + 1] - src_t[h:h + 1, :], NEG_BIG)
                    ms.append((cb * jnp.exp2(seg)).astype(BF16))
                x_pair = xs_g[:, pair * LANES:(pair + 1) * LANES]
                zero = jnp.zeros_like(x_pair)
                rhs = jnp.concatenate([jnp.where(lo_half, x_pair, zero),
                                       jnp.where(lo_half, zero, x_pair)], axis=0)
                y_parts.append(_dot(jnp.concatenate(ms, axis=1), rhs))
            y_g = y_g + jnp.concatenate(y_parts, axis=1)
            y_g = y_g + xs_f * dskip_ref[:, cs]
            y_g = y_g * _silu(z_ref[0, rows, cs].astype(F32))
            msq = jnp.mean(y_g * y_g, axis=-1, keepdims=True)
            y_g = y_g * lax.rsqrt(msq + SUBLN_EPS) * ng_ref[:, cs]
            y_ref[0, rows, cs] = y_g.astype(y_ref.dtype)

    prepared = prepare(0)
    for k in range(sub_chunks):
        current = prepared
        if k + 1 < sub_chunks:
            prepared = prepare(k + 1)
        scan(k, current)


def _ssd(proj3, dt3, alog, wx, bx, wb, bb, wc, bc, dskip_x, ng, sel):
    b, s, _ = proj3.shape
    length = SSD_L
    rows = SSD_SUB * length
    nc = s // rows
    hb = rows // BF16_SUBLANES
    xs_blk, z_blk = OFF_XS // SSM_D_INNER, OFF_Z // SSM_D_INNER
    b_blk, c_blk = OFF_B // BC_COLS, OFF_C // BC_COLS

    def main(width, blk):
        return pl.BlockSpec((1, rows, width), lambda i, c: (i, c, blk))

    def halo(width, blk):
        return pl.BlockSpec((1, BF16_SUBLANES, width),
                            lambda i, c: (i, jnp.maximum(c * hb - 1, 0), blk))

    def const(shape):
        return pl.BlockSpec(shape, lambda i, c: (0,) * len(shape))

    kern = functools.partial(_ssd_kernel, length=length, sub_chunks=SSD_SUB)
    return pl.pallas_call(
        kern,
        grid=(b, nc),
        in_specs=[
            main(SSM_D_INNER, xs_blk), halo(SSM_D_INNER, xs_blk),
            main(BC_COLS, b_blk), halo(BC_COLS, b_blk),
            main(BC_COLS, c_blk), halo(BC_COLS, c_blk),
            main(SSM_D_INNER, z_blk),
            pl.BlockSpec((1, rows, LANES), lambda i, c: (i, c, 0)),
            const((1, LANES)),
            const((SSM_CONV, BF16_SUBLANES, SSM_D_INNER)), const((1, SSM_D_INNER)),
            const((SSM_CONV, BF16_SUBLANES, BC_COLS)), const((1, BC_COLS)),
            const((SSM_CONV, BF16_SUBLANES, BC_COLS)), const((1, BC_COLS)),
            const((1, SSM_D_INNER)), const((1, SSM_D_INNER)),
            const((2 * LANES, SSM_D_INNER)),
        ],
        out_specs=pl.BlockSpec((1, rows, SSM_D_INNER), lambda i, c: (i, c, 0)),
        out_shape=jax.ShapeDtypeStruct((b, s, SSM_D_INNER), BF16),
        scratch_shapes=[pltpu.VMEM((SSM_GROUPS, SSM_STATE, GROUP_COLS), F32)],
        compiler_params=pltpu.CompilerParams(
            dimension_semantics=("parallel", "arbitrary"), vmem_limit_bytes=VMEM_LIMIT),
        name="ssd",
    )(proj3, proj3, proj3, proj3, proj3, proj3, proj3, dt3, alog,
      wx, bx, wb, bb, wc, bc, dskip_x, ng, sel)


def _merge_kernel(x_ref, ya_ref, ys_ref, ga_ref, gs_ref, wpa_ref, wps_ref, wo_ref, g_ref,
                  x1_ref, h2_ref):
    pa = _dot(ya_ref[...], wpa_ref[...])
    ps = _dot(ys_ref[...], wps_ref[...])
    ga = 1.0 / (1.0 + jnp.exp(-ga_ref[...].astype(F32)))
    gs = 1.0 / (1.0 + jnp.exp(-gs_ref[...].astype(F32)))
    mixed = (ga * pa + gs * ps).astype(BF16)
    x1 = x_ref[...] + _dot(mixed, wo_ref[...])
    x1_ref[...] = x1
    ms = jnp.mean(x1 * x1, axis=-1, keepdims=True)
    h2_ref[...] = (x1 * lax.rsqrt(ms + RMS_EPS) * g_ref[...]).astype(BF16)


def _merge(x2, y_attn, y_ssm, proj, wpa, wps, wo, g):
    m = x2.shape[0]
    tm = MERGE_TM
    ga_blk = OFF_GATE // D_MODEL

    def rows(width, blk=0):
        return pl.BlockSpec((tm, width), lambda i: (i, blk))

    def const(shape):
        return pl.BlockSpec(shape, lambda i: (0, 0))

    return pl.pallas_call(
        _merge_kernel,
        grid=(m // tm,),
        in_specs=[
            rows(D_MODEL), rows(V_COLS), rows(SSM_D_INNER),
            rows(D_MODEL, ga_blk), rows(D_MODEL, ga_blk + 1),
            const((V_COLS, D_MODEL)), const((SSM_D_INNER, D_MODEL)), const((D_MODEL, D_MODEL)),
            const((1, D_MODEL)),
        ],
        out_specs=[rows(D_MODEL), rows(D_MODEL)],
        out_shape=[jax.ShapeDtypeStruct((m, D_MODEL), F32),
                   jax.ShapeDtypeStruct((m, D_MODEL), BF16)],
        compiler_params=pltpu.CompilerParams(
            dimension_semantics=("parallel",), vmem_limit_bytes=VMEM_LIMIT),
        name="merge",
    )(x2, y_attn, y_ssm, proj, proj, wpa, wps, wo, g)


def _ffn_kernel(x1_ref, h_ref, hh_ref, wup_ref, cw_ref, cb_ref, wd_ref, o_ref, gscr, vscr,
                *, tm, tiles_per_seq):
    i = pl.program_id(0)
    first = (i % tiles_per_seq) == 0
    halo = hh_ref[...]
    halo = jnp.where(first, jnp.zeros_like(halo), halo)
    hrows = halo.shape[0]
    h_ext = jnp.concatenate([halo, h_ref[...]], axis=0)
    taps = cw_ref.shape[0]

    slots = gscr.shape[0]

    def up_project(idx):
        start, width = FFN_SPLITS[idx]
        for col0, scr in ((start, gscr), (D_FF + start, vscr)):
            scr[idx % slots, :, 0:width] = _dot(h_ext, wup_ref[:, col0:col0 + width])

    def conv(idx, col0, scr):
        width = FFN_SPLITS[idx][1]
        cols = slice(col0, col0 + width)
        slot = idx % slots
        out = cb_ref[:, cols] + cw_ref[taps - 1:taps, cols] * scr[slot, hrows:hrows + tm, 0:width]
        for k in range(taps - 1):
            off = hrows - (taps - 1) + k
            out = out + cw_ref[k:k + 1, cols] * scr[slot, off:off + tm, 0:width]
        return out

    def gated(idx):
        start = FFN_SPLITS[idx][0]
        return (_silu(conv(idx, start, gscr)) * conv(idx, D_FF + start, vscr)).astype(BF16)

    n_chunks = len(FFN_SPLITS)
    ahead = slots - 1
    acc = x1_ref[...]
    acts = {}
    for idx in range(min(ahead, n_chunks)):
        up_project(idx)
    for idx in range(n_chunks + 1):
        if idx + ahead < n_chunks:
            up_project(idx + ahead)
        if idx < n_chunks:
            acts[idx] = gated(idx)
        if idx >= 1:
            start, width = FFN_SPLITS[idx - 1]
            acc = acc + _dot(acts.pop(idx - 1), wd_ref[start:start + width, :])

    o_ref[...] = acc


def _ffn(x1, h2, w_up, cw, cb, w_down, seq):
    m = x1.shape[0]
    tm = FFN_TM
    hb = tm // BF16_SUBLANES
    kern = functools.partial(_ffn_kernel, tm=tm, tiles_per_seq=seq // tm)

    def resident(shape):
        return pl.BlockSpec(shape, lambda i: (0, 0), pipeline_mode=pl.Buffered(1))

    return pl.pallas_call(
        kern,
        grid=(m // tm,),
        in_specs=[
            pl.BlockSpec((tm, D_MODEL), lambda i: (i, 0)),
            pl.BlockSpec((tm, D_MODEL), lambda i: (i, 0)),
            pl.BlockSpec((BF16_SUBLANES, D_MODEL), lambda i: (jnp.maximum(i * hb - 1, 0), 0)),
            resident((D_MODEL, 2 * D_FF)),
            resident((FFN_CONV, 2 * D_FF)),
            resident((1, 2 * D_FF)),
            resident((D_FF, D_MODEL)),
        ],
        out_specs=pl.BlockSpec((tm, D_MODEL), lambda i: (i, 0)),
        out_shape=jax.ShapeDtypeStruct((m, D_MODEL), F32),
        scratch_shapes=[
            pltpu.VMEM((FFN_SLOTS, tm + BF16_SUBLANES, FFN_SCR), F32),
            pltpu.VMEM((FFN_SLOTS, tm + BF16_SUBLANES, FFN_SCR), F32),
        ],
        compiler_params=pltpu.CompilerParams(
            dimension_semantics=("parallel",), vmem_limit_bytes=VMEM_LIMIT),
        name="ffn",
    )(x1, h2, h2, w_up, cw, cb, w_down)


def _head_expand(rows, cols, chunk):
    r = lax.broadcasted_iota(jnp.int32, (rows, cols), 0)
    c = lax.broadcasted_iota(jnp.int32, (rows, cols), 1) // chunk
    return (r == c).astype(BF16)


def _layer(x, rel_bias, norm_mix_g, w_in, q_norm_g, k_norm_g, lq1, lk1, lq2, lk2, attn_subln_g,
           conv_ssm_w, conv_ssm_b, dt_bias, a_log, d_skip, ssm_norm_g, w_proj_attn, w_proj_ssm,
           w_out, norm_ffn_g, w_up, conv_ffn_w, conv_ffn_b, w_down):
    b, s, d = x.shape
    m = b * s
    x2 = x.reshape(m, d)

    o_dt = sum(width for name, width in W_SEGMENTS if name != "gate")

    def w_cols(lo, hi):
        parts = []
        if lo < o_dt:
            parts.append(w_in[:, lo:min(hi, o_dt)])
        if hi > o_dt:
            parts.append(w_in[:, max(lo, o_dt) + SSM_HEADS:hi + SSM_HEADS])
        return (parts[0] if len(parts) == 1 else jnp.concatenate(parts, axis=1)).astype(BF16)

    half_cols = PROJ_COLS // IN_HALVES
    w_halves = [w_cols(hv * half_cols, (hv + 1) * half_cols) for hv in range(IN_HALVES)]
    pad_h = LANES - SSM_HEADS
    w_dt = jnp.pad(w_in[:, o_dt:o_dt + SSM_HEADS], ((0, 0), (0, pad_h))).astype(BF16)
    dt_b = jnp.pad(dt_bias.astype(F32), (0, pad_h)).reshape(1, LANES)
    reps = Q_COLS // ATTN_HEAD_DIM
    qk_gain = jnp.stack([
        jnp.tile(q_norm_g.astype(F32) * (ATTN_HEAD_DIM ** -0.5 * LOG2E), reps),
        jnp.tile(k_norm_g.astype(F32), reps)]).reshape(2, 1, Q_COLS)

    proj, dt = _in_proj(x2, norm_mix_g.astype(F32).reshape(1, d), w_halves, w_dt, dt_b, qk_gain)
    proj3 = proj.reshape(b, s, PROJ_COLS)

    vec = lambda v: v.astype(F32).reshape(1, -1)
    bias, far, lam = _bias_build(rel_bias.astype(F32), vec(lq1), vec(lk1), vec(lq2), vec(lk2), ATTN_T)
    y_attn = _diff_attn(lam[0, :1], far[:, 0, 0], proj3, bias,
                        vec(attn_subln_g) * (1.0 - LAM_INIT))

    cw = jnp.broadcast_to(conv_ssm_w.astype(BF16)[:, None, :],
                          (SSM_CONV, BF16_SUBLANES, conv_ssm_w.shape[-1]))
    cbias = conv_ssm_b.astype(F32).reshape(1, -1)
    xs_sl = slice(0, SSM_D_INNER)
    b_sl = slice(SSM_D_INNER, SSM_D_INNER + BC_COLS)
    c_sl = slice(SSM_D_INNER + BC_COLS, SSM_D_INNER + 2 * BC_COLS)
    y_ssm = _ssd(
        proj3, dt.reshape(b, s, LANES),
        jnp.pad(a_log.astype(F32), (0, pad_h)).reshape(1, LANES),
        cw[..., xs_sl], cbias[:, xs_sl], cw[..., b_sl], cbias[:, b_sl], cw[..., c_sl], cbias[:, c_sl],
        jnp.repeat(d_skip.astype(F32), SSM_HEAD_DIM).reshape(1, SSM_D_INNER),
        vec(ssm_norm_g), jnp.tile(_head_expand(LANES, SSM_D_INNER, SSM_HEAD_DIM), (2, 1)))

    x1, h2 = _merge(x2, y_attn.reshape(m, V_COLS), y_ssm.reshape(m, SSM_D_INNER), proj,
                    w_proj_attn.astype(BF16), w_proj_ssm.astype(BF16), w_out.astype(BF16),
                    vec(norm_ffn_g))

    out = _ffn(x1, h2, w_up.astype(BF16), conv_ffn_w.astype(F32),
               conv_ffn_b.astype(F32).reshape(1, -1), w_down.astype(BF16), s)
    return out.reshape(b, s, d)


def kernel(x, rel_bias, norm_mix_g, w_in, q_norm_g, k_norm_g, lambda_q1, lambda_k1, lambda_q2,
           lambda_k2, attn_subln_g, conv_ssm_w, conv_ssm_b, dt_bias, a_log, d_skip, ssm_norm_g,
           w_proj_attn, w_proj_ssm, w_out, norm_ffn_g, w_up, conv_ffn_w, conv_ffn_b, w_down):
    depth = w_in.shape[0]
    assert depth == 1, "lambda_init and the stacked-parameter layout are specialised to depth 1"
    return _layer(x, rel_bias, norm_mix_g[0], w_in[0], q_norm_g[0], k_norm_g[0], lambda_q1[0],
                  lambda_k1[0], lambda_q2[0], lambda_k2[0], attn_subln_g[0], conv_ssm_w[0],
                  conv_ssm_b[0], dt_bias[0], a_log[0], d_skip[0], ssm_norm_g[0], w_proj_attn[0],
                  w_proj_ssm[0], w_out[0], norm_ffn_g[0], w_up[0], conv_ffn_w[0], conv_ffn_b[0],
                  w_down[0])
```

```python
import functools
import math

import jax
import jax.numpy as jnp
from jax import lax
from jax.experimental import pallas as pl
from jax.experimental.pallas import tpu as pltpu

F32 = jnp.float32
BF16 = jnp.bfloat16

D_MODEL = 1024
ATTN_HEADS = 8
ATTN_HEAD_DIM = 64
ATTN_V_DIM = 2 * ATTN_HEAD_DIM
NUM_BUCKETS = 32
MAX_DISTANCE = 128
SSM_D_INNER = 2048
SSM_HEAD_DIM = 64
SSM_HEADS = 32
SSM_GROUPS = 4
SSM_HEADS_PER_GROUP = SSM_HEADS // SSM_GROUPS
SSM_STATE = 128
SSM_CONV = 4
D_FF = 2816
FFN_CONV = 3
RMS_EPS = 1e-6
SUBLN_EPS = 1e-5
LAM_INIT = 0.8 - 0.6 * math.exp(0.0)

Q_COLS = ATTN_HEADS * 2 * ATTN_HEAD_DIM
V_COLS = ATTN_HEADS * ATTN_V_DIM
BC_COLS = SSM_GROUPS * SSM_STATE
GROUP_COLS = SSM_D_INNER // SSM_GROUPS

LANES = 128
SUBLANES = 8
BF16_SUBLANES = 16
VMEM_LIMIT = 56 * 1024 * 1024

LOG2E = math.log2(math.e)
NEG_BIG = -1e30

W_SEGMENTS = (("q", Q_COLS), ("k", Q_COLS), ("v", V_COLS), ("z", SSM_D_INNER), ("xs", SSM_D_INNER),
              ("b", BC_COLS), ("c", BC_COLS), ("gate", 2 * D_MODEL))
OFF_Z = 0
OFF_Q = OFF_Z + SSM_D_INNER
OFF_K = OFF_Q + Q_COLS
OFF_V = OFF_K + Q_COLS
OFF_B = OFF_V + V_COLS
OFF_C = OFF_B + BC_COLS
OFF_XS = OFF_C + BC_COLS
OFF_GATE = OFF_XS + SSM_D_INNER
PROJ_COLS = OFF_GATE + 2 * D_MODEL
OUT_OFFSETS = {"q": OFF_Q, "k": OFF_K, "v": OFF_V, "z": OFF_Z, "xs": OFF_XS, "b": OFF_B,
               "c": OFF_C, "gate": OFF_GATE}

IN_TM = 512
IN_TN = 512
IN_HALVES = 2
ATTN_T = 128
ATTN_HEADS_PER_STEP = 2
ATTN_LAGS = (2, 5)
SSD_L = 128
SSD_SUB = 4
MERGE_TM = 512
FFN_TM = 512
FFN_SPLITS = ((0, 512), (512, 512), (1024, 512), (1536, 512), (2048, 512), (2560, 256))
FFN_SCR = max(w for _, w in FFN_SPLITS)
FFN_SLOTS = 3


def _split(v, parts):
    out = []
    for _ in range(parts - 1):
        hi = v.astype(BF16)
        out.append(hi)
        v = v - hi.astype(F32)
    out.append(v.astype(BF16))
    return out


def _dot(a, b):
    return jnp.dot(a, b, preferred_element_type=F32)


def _dot_nt(a, b):
    return lax.dot_general(a, b, (((1,), (1,)), ((), ())), preferred_element_type=F32)


def _dot_exact_lhs(sel, parts):
    out = _dot(sel, parts[0])
    for p in parts[1:]:
        out = out + _dot(sel, p)
    return out


def _silu(v):
    h = 0.5 * v
    return h + h * jnp.tanh(h)


def _in_proj_kernel(x0_ref, xn_ref, g_ref, wdt_ref, dtb_ref, qkg_ref, *rest, tile_plan):
    w_refs, (o_ref, dt_ref, h_scr) = rest[:len(tile_plan)], rest[len(tile_plan):]
    i = pl.program_id(0)
    half = pl.program_id(1)
    slot = i % 2

    def normed(x):
        ms = jnp.mean(x * x, axis=-1, keepdims=True)
        return (x * lax.rsqrt(ms + RMS_EPS) * g_ref[...]).astype(BF16)

    @pl.when(jnp.logical_and(i == 0, half == 0))
    def _():
        h_scr[0] = normed(x0_ref[...])

    h = h_scr[slot]

    def tile_cols(t):
        return slice(t * IN_TN, (t + 1) * IN_TN)

    def store_plain(out0, acc):
        o_ref[:, out0:out0 + IN_TN] = acc.astype(BF16)

    def store_head_normed(out0, acc, gain_row, gain_col0):
        lo_half = lax.broadcasted_iota(jnp.int32, (IN_TM, LANES), 1) < ATTN_HEAD_DIM
        for cblk in range(IN_TN // LANES):
            a = acc[:, cblk * LANES:(cblk + 1) * LANES]
            sq = a * a
            s_lo = jnp.sum(jnp.where(lo_half, sq, 0.0), axis=-1, keepdims=True)
            s_hi = jnp.sum(jnp.where(lo_half, 0.0, sq), axis=-1, keepdims=True)
            ssum = jnp.where(lo_half, s_lo, s_hi)
            y = a * lax.rsqrt(ssum * (1.0 / ATTN_HEAD_DIM) + RMS_EPS)
            y = y * qkg_ref[gain_row, :, gain_col0 + cblk * LANES:gain_col0 + (cblk + 1) * LANES]
            o_ref[:, out0 + cblk * LANES:out0 + (cblk + 1) * LANES] = y.astype(BF16)

    for hv, plan in enumerate(tile_plan):
        @pl.when(half == hv)
        def _(plan=plan, hv=hv):
            w_ref = w_refs[hv]
            acc_next = _dot(h, w_ref[:, tile_cols(0)])
            if hv == 0:
                t = _dot(h, wdt_ref[...]) + dtb_ref[...]
                dt_ref[...] = jnp.maximum(t, 0.0) + jnp.log1p(jnp.exp(-jnp.abs(t)))
            if hv == len(tile_plan) - 1:
                h_scr[1 - slot] = normed(xn_ref[...])
            for t, (out0, head_norm) in enumerate(plan):
                acc = acc_next
                if t + 1 < len(plan):
                    acc_next = _dot(h, w_ref[:, tile_cols(t + 1)])
                if head_norm is None:
                    store_plain(out0, acc)
                else:
                    store_head_normed(out0, acc, *head_norm)


def _in_proj_plan():
    half_cols = PROJ_COLS // IN_HALVES
    gain_rows = {"q": 0, "k": 1}
    plan = [[] for _ in range(IN_HALVES)]
    w_off = 0
    for name, width in W_SEGMENTS:
        for col in range(0, width, IN_TN):
            hv, out_col = (w_off + col) // half_cols, OUT_OFFSETS[name] + col
            assert out_col // half_cols == hv and width % IN_TN == 0
            head_norm = (gain_rows[name], col) if name in gain_rows else None
            plan[hv].append((out_col - hv * half_cols, head_norm))
        w_off += width
    return tuple(tuple(p) for p in plan)


def _in_proj(x2, g, w_halves, w_dt, dt_bias, qk_gain):
    m = x2.shape[0]
    n_tiles = m // IN_TM
    half_cols = PROJ_COLS // IN_HALVES
    kern = functools.partial(_in_proj_kernel, tile_plan=_in_proj_plan())
    return pl.pallas_call(
        kern,
        grid=(m // IN_TM, IN_HALVES),
        in_specs=[
            pl.BlockSpec((IN_TM, D_MODEL), lambda i, hv: (0, 0)),
            pl.BlockSpec((IN_TM, D_MODEL), lambda i, hv: (jnp.minimum(i + 1, n_tiles - 1), 0)),
            pl.BlockSpec((1, D_MODEL), lambda i, hv: (0, 0)),
            pl.BlockSpec((D_MODEL, LANES), lambda i, hv: (0, 0)),
            pl.BlockSpec((1, LANES), lambda i, hv: (0, 0)),
            pl.BlockSpec(qk_gain.shape, lambda i, hv: (0, 0, 0)),
        ] + [pl.BlockSpec((D_MODEL, half_cols), lambda i, hv: (0, 0), pipeline_mode=pl.Buffered(1))
             for _ in w_halves],
        out_specs=[
            pl.BlockSpec((IN_TM, half_cols), lambda i, hv: (i, hv)),
            pl.BlockSpec((IN_TM, LANES), lambda i, hv: (i, 0)),
        ],
        out_shape=[
            jax.ShapeDtypeStruct((m, PROJ_COLS), BF16),
            jax.ShapeDtypeStruct((m, LANES), F32),
        ],
        scratch_shapes=[pltpu.VMEM((2, IN_TM, D_MODEL), BF16)],
        compiler_params=pltpu.CompilerParams(
            dimension_semantics=("arbitrary", "arbitrary"), vmem_limit_bytes=VMEM_LIMIT),
        name="in_proj",
    )(x2, x2, g, w_dt, dt_bias, qk_gain, *w_halves)


def _bias_kernel(rel_ref, lq1_ref, lk1_ref, lq2_ref, lk2_ref, bias_ref, far_ref, lam_ref, *, tile):
    h = pl.program_id(0)
    row = lax.broadcasted_iota(jnp.int32, (tile, tile), 0)
    col = lax.broadcasted_iota(jnp.int32, (tile, tile), 1)
    max_exact = NUM_BUCKETS // 2
    for t in range(2):
        d = row - col + t * tile
        n = jnp.maximum(d, 0)
        nf = jnp.maximum(n, 1).astype(F32)
        large = max_exact + (jnp.log(nf / max_exact) / math.log(MAX_DISTANCE / max_exact)
                             * (NUM_BUCKETS - max_exact)).astype(jnp.int32)
        large = jnp.minimum(large, NUM_BUCKETS - 1)
        bucket = jnp.where(n < max_exact, n, large)
        bias = jnp.zeros((tile, tile), F32)
        for bk in range(NUM_BUCKETS):
            bias = jnp.where(bucket == bk, rel_ref[bk, h], bias)
        bias = bias * LOG2E
        if t == 0:
            bias = jnp.where(d >= 0, bias, NEG_BIG)
        bias_ref[0, t] = bias
    far_ref[...] = jnp.full(far_ref.shape, rel_ref[NUM_BUCKETS - 1, h] * LOG2E, F32)
    s1 = jnp.sum(lq1_ref[...] * lk1_ref[...], axis=-1, keepdims=True)
    s2 = jnp.sum(lq2_ref[...] * lk2_ref[...], axis=-1, keepdims=True)
    lam = jnp.exp(s1) - jnp.exp(s2) + LAM_INIT
    lam_ref[...] = jnp.broadcast_to(lam, lam_ref.shape)


def _bias_build(rel_bias, lq1, lk1, lq2, lk2, tile):
    vec = pl.BlockSpec((1, ATTN_HEAD_DIM), lambda h: (0, 0))
    return pl.pallas_call(
        functools.partial(_bias_kernel, tile=tile),
        grid=(ATTN_HEADS,),
        in_specs=[pl.BlockSpec(memory_space=pltpu.SMEM), vec, vec, vec, vec],
        out_specs=[
            pl.BlockSpec((1, 2, tile, tile), lambda h: (h, 0, 0, 0)),
            pl.BlockSpec((1, SUBLANES, LANES), lambda h: (h, 0, 0)),
            pl.BlockSpec((SUBLANES, LANES), lambda h: (0, 0)),
        ],
        out_shape=[
            jax.ShapeDtypeStruct((ATTN_HEADS, 2, tile, tile), F32),
            jax.ShapeDtypeStruct((ATTN_HEADS, SUBLANES, LANES), F32),
            jax.ShapeDtypeStruct((SUBLANES, LANES), F32),
        ],
        compiler_params=pltpu.CompilerParams(dimension_semantics=("arbitrary",)),
        name="bias_build",
    )(rel_bias, lq1, lk1, lq2, lk2)


def _attn_kernel(lam_ref, far_ref, q_ref, k_ref, v_ref, bias_ref, g_ref, o_ref, vext_scr,
                 *, tile, nq, heads):
    h0 = pl.program_id(1) * heads
    lam = lam_ref[0]
    lane = lax.broadcasted_iota(jnp.int32, (tile, LANES), 1)
    lo_half = lane < ATTN_HEAD_DIM
    gain = g_ref[...]

    for j in range(heads):
        vals = v_ref[0, :, j * LANES:(j + 1) * LANES]
        vext_scr[j] = jnp.concatenate([vals, jnp.ones_like(vals)], axis=1)

    def scores(j, qi):
        q = q_ref[0, qi * tile:(qi + 1) * tile, j * LANES:(j + 1) * LANES]
        zero = jnp.zeros_like(q)
        q_both = jnp.concatenate([jnp.where(lo_half, q, zero), jnp.where(lo_half, zero, q)], axis=0)
        return _dot_nt(q_both, k_ref[0, 0:(qi + 1) * tile, j * LANES:(j + 1) * LANES])

    def softmax(j, qi, s):
        far = far_ref[h0 + j]
        nk = (qi + 1) * tile
        n_far = nk - min(qi + 1, 2) * tile
        if qi == 0:
            near_bias = bias_ref[j, 0]
        else:
            near_bias = jnp.concatenate([bias_ref[j, 1], bias_ref[j, 0]], axis=1)
        s_near = s[:, n_far:] + jnp.concatenate([near_bias, near_bias], axis=0)
        m = jnp.max(s_near, axis=-1, keepdims=True)
        if n_far:
            s_far = s[:, :n_far]
            m = jnp.maximum(m, jnp.max(s_far, axis=-1, keepdims=True) + far)
            p = jnp.concatenate([jnp.exp2(s_far - (m - far)), jnp.exp2(s_near - m)], axis=1)
        else:
            p = jnp.exp2(s_near - m)
        return p.astype(BF16)

    def weighted_values(j, qi, p):
        pv = _dot(p, vext_scr[j, 0:(qi + 1) * tile, :])
        return pv[:, :LANES] / pv[:, LANES:]

    units = [(j, qi) for qi in reversed(range(nq)) for j in range(heads)]
    n_units = len(units)
    s_q, p_q = {}, {}
    lag_s, lag_v = ATTN_LAGS
    for step in range(n_units + lag_v):
        if step < n_units:
            s_q[step] = scores(*units[step])
        if 0 <= step - lag_s < n_units:
            p_q[step - lag_s] = softmax(*units[step - lag_s], s_q.pop(step - lag_s))
        if 0 <= step - lag_v < n_units:
            j, qi = units[step - lag_v]
            both = weighted_values(j, qi, p_q.pop(step - lag_v))
            o = both[:tile] - lam * both[tile:]
            ms = jnp.mean(o * o, axis=-1, keepdims=True)
            o = o * lax.rsqrt(ms + SUBLN_EPS) * gain
            o_ref[0, qi * tile:(qi + 1) * tile, j * LANES:(j + 1) * LANES] = o.astype(o_ref.dtype)


def _diff_attn(lam, far, proj3, bias, gain):
    b, s, _ = proj3.shape
    tile = bias.shape[-1]
    nq = s // tile
    heads = ATTN_HEADS_PER_STEP
    width = heads * LANES
    qb, kb, vb = OFF_Q // width, OFF_K // width, OFF_V // width
    kern = functools.partial(_attn_kernel, tile=tile, nq=nq, heads=heads)
    return pl.pallas_call(
        kern,
        grid=(b, ATTN_HEADS // heads),
        in_specs=[
            pl.BlockSpec(memory_space=pltpu.SMEM),
            pl.BlockSpec(memory_space=pltpu.SMEM),
            pl.BlockSpec((1, s, width), lambda i, h: (i, 0, qb + h)),
            pl.BlockSpec((1, s, width), lambda i, h: (i, 0, kb + h)),
            pl.BlockSpec((1, s, width), lambda i, h: (i, 0, vb + h)),
            pl.BlockSpec((heads, 2, tile, tile), lambda i, h: (h, 0, 0, 0)),
            pl.BlockSpec((1, LANES), lambda i, h: (0, 0)),
        ],
        out_specs=pl.BlockSpec((1, s, width), lambda i, h: (i, 0, h)),
        out_shape=jax.ShapeDtypeStruct((b, s, V_COLS), BF16),
        scratch_shapes=[pltpu.VMEM((heads, s, 2 * LANES), BF16)],
        compiler_params=pltpu.CompilerParams(
            dimension_semantics=("parallel", "parallel"), vmem_limit_bytes=VMEM_LIMIT),
        name="diff_attn",
    )(lam, far, proj3, proj3, proj3, bias, gain)


def _shift_matrices(taps, length, halo_rows):
    def build(rows, offset):
        t = lax.broadcasted_iota(jnp.int32, (rows, taps * rows), 0)
        c = lax.broadcasted_iota(jnp.int32, (rows, taps * rows), 1)
        tap = c // rows
        return ((c - tap * rows) == offset + t - (taps - 1 - tap)).astype(BF16)

    return build(length, 0), build(halo_rows, halo_rows)


def _conv_silu(x, halo, w_ref, b_ref, shifts, cols):
    shift_main, shift_head = shifts
    hrows = halo.shape[0]
    def weighted(v):
        v3 = v.reshape(v.shape[0] // hrows, hrows, v.shape[1])
        return jnp.concatenate([(v3 * w_ref[k, :, cols][None]).reshape(v.shape)
                                for k in range(w_ref.shape[0])], axis=0)

    acc = b_ref[:, cols] + _dot(shift_main, weighted(x))
    head = _dot(shift_head, weighted(halo))
    acc = jnp.concatenate([acc[0:hrows] + head, acc[hrows:]], axis=0)
    return _silu(acc)


def _ssd_kernel(xs_ref, bc_ref, z_ref, dt_ref, alog_ref, wx_ref, bx_ref, wbc_ref, bbc_ref,
                dskip_ref, ng_ref, e_ref, y_ref, state_scr, xs_halo, bc_halo,
                *, length, sub_chunks):
    c = pl.program_id(1)

    @pl.when(c == 0)
    def _():
        state_scr[...] = jnp.zeros_like(state_scr)
        xs_halo[...] = jnp.zeros_like(xs_halo)
        bc_halo[...] = jnp.zeros_like(bc_halo)

    shift = _shift_matrices(SSM_CONV, length, BF16_SUBLANES)
    a = -jnp.exp(alog_ref[...])
    row = lax.broadcasted_iota(jnp.int32, (length, length), 0)
    col = lax.broadcasted_iota(jnp.int32, (length, length), 1)
    causal = row >= col
    tri = causal.astype(BF16)
    sel2 = e_ref[...]
    lane = lax.broadcasted_iota(jnp.int32, (length, LANES), 1)
    lo_half = lane < SSM_HEAD_DIM

    def conv(k, main_ref, halo_scr, w_ref, b_ref, cols=slice(None)):
        r0 = k * length
        if k == 0:
            halo = halo_scr[:, cols]
        else:
            halo = main_ref[0, r0 - BF16_SUBLANES:r0, cols]
        return _conv_silu(main_ref[0, r0:r0 + length, cols], halo, w_ref, b_ref, shift, cols)

    def expand(v):
        return _dot(jnp.concatenate(_split(v, 2), axis=1), sel2)

    def prepare(k):
        rows = slice(k * length, (k + 1) * length)
        bc = conv(k, bc_ref, bc_halo, wbc_ref, bbc_ref).astype(BF16)
        bm, cm = bc[:, :BC_COLS], bc[:, BC_COLS:]
        xs = [conv(k, xs_ref, xs_halo, wx_ref, bx_ref, slice(g * GROUP_COLS, (g + 1) * GROUP_COLS))
              for g in range(SSM_GROUPS)]
        dt = dt_ref[0, rows, :]
        cum = _dot_exact_lhs(tri, _split(dt * a, 3))
        cum2 = cum * LOG2E
        src_t = cum2.T - jnp.log2(dt.T)
        total = cum[length - 1:length, :]
        ecum_x = expand(jnp.exp(cum))
        edtw_x = expand(dt * jnp.exp(total - cum))
        return bm, cm, xs, cum2, src_t, ecum_x, edtw_x

    def scan(k, prepared):
        bm, cm, xs, cum2, src_t, ecum_x, edtw_x = prepared
        rows = slice(k * length, (k + 1) * length)
        etot_x = ecum_x[length - 1:length, :]
        for g in range(SSM_GROUPS):
            gs = slice(g * SSM_STATE, (g + 1) * SSM_STATE)
            cs = slice(g * GROUP_COLS, (g + 1) * GROUP_COLS)
            xs_f = xs[g]
            b_g = bm[:, gs]
            c_g = cm[:, gs]
            xs_g = xs_f.astype(BF16)
            cb = _dot_nt(c_g, b_g)
            st = state_scr[g]
            y_g = _dot(c_g, st.astype(BF16)) * ecum_x[:, cs]
            xw = (xs_f * edtw_x[:, cs]).astype(BF16)
            b_t = b_g.astype(F32).T.astype(BF16)
            state_scr[g] = st * etot_x[:, cs] + _dot(b_t, xw)
            y_parts = []
            for pair in range(SSM_HEADS_PER_GROUP // 2):
                ms = []
                for hh in range(2):
                    h = g * SSM_HEADS_PER_GROUP + pair * 2 + hh
                    seg = jnp.where(causal, cum2[:, h:h + 1] - src_t[h:h + 1, :], NEG_BIG)
                    ms.append((cb * jnp.exp2(seg)).astype(BF16))
                x_pair = xs_g[:, pair * LANES:(pair + 1) * LANES]
                zero = jnp.zeros_like(x_pair)
                rhs = jnp.concatenate([jnp.where(lo_half, x_pair, zero),
                                       jnp.where(lo_half, zero, x_pair)], axis=0)
                y_parts.append(_dot(jnp.concatenate(ms, axis=1), rhs))
            y_g = y_g + jnp.concatenate(y_parts, axis=1)
            y_g = y_g + xs_f * dskip_ref[:, cs]
            y_g = y_g * _silu(z_ref[0, rows, cs].astype(F32))
            msq = jnp.mean(y_g * y_g, axis=-1, keepdims=True)
            y_g = y_g * lax.rsqrt(msq + SUBLN_EPS) * ng_ref[:, cs]
            y_ref[0, rows, cs] = y_g.astype(y_ref.dtype)

    prepared = prepare(0)
    for k in range(sub_chunks):
        current = prepared
        if k + 1 < sub_chunks:
            prepared = prepare(k + 1)
        scan(k, current)

    last = sub_chunks * length
    xs_halo[...] = xs_ref[0, last - BF16_SUBLANES:last, :]
    bc_halo[...] = bc_ref[0, last - BF16_SUBLANES:last, :]


def _ssd(proj3, dt3, alog, wx, bx, wbc, bbc, dskip_x, ng, sel):
    b, s, _ = proj3.shape
    length = SSD_L
    rows = SSD_SUB * length
    nc = s // rows
    xs_blk, z_blk = OFF_XS // SSM_D_INNER, OFF_Z // SSM_D_INNER
    assert OFF_C == OFF_B + BC_COLS and OFF_B % (2 * BC_COLS) == 0
    bc_blk = OFF_B // (2 * BC_COLS)

    def main(width, blk):
        return pl.BlockSpec((1, rows, width), lambda i, c: (i, c, blk))

    def const(shape):
        return pl.BlockSpec(shape, lambda i, c: (0,) * len(shape))

    kern = functools.partial(_ssd_kernel, length=length, sub_chunks=SSD_SUB)
    return pl.pallas_call(
        kern,
        grid=(b, nc),
        in_specs=[
            main(SSM_D_INNER, xs_blk), main(2 * BC_COLS, bc_blk), main(SSM_D_INNER, z_blk),
            pl.BlockSpec((1, rows, LANES), lambda i, c: (i, c, 0)),
            const((1, LANES)),
            const((SSM_CONV, BF16_SUBLANES, SSM_D_INNER)), const((1, SSM_D_INNER)),
            const((SSM_CONV, BF16_SUBLANES, 2 * BC_COLS)), const((1, 2 * BC_COLS)),
            const((1, SSM_D_INNER)), const((1, SSM_D_INNER)),
            const((2 * LANES, SSM_D_INNER)),
        ],
        out_specs=pl.BlockSpec((1, rows, SSM_D_INNER), lambda i, c: (i, c, 0)),
        out_shape=jax.ShapeDtypeStruct((b, s, SSM_D_INNER), BF16),
        scratch_shapes=[
            pltpu.VMEM((SSM_GROUPS, SSM_STATE, GROUP_COLS), F32),
            pltpu.VMEM((BF16_SUBLANES, SSM_D_INNER), BF16),
            pltpu.VMEM((BF16_SUBLANES, 2 * BC_COLS), BF16),
        ],
        compiler_params=pltpu.CompilerParams(
            dimension_semantics=("parallel", "arbitrary"), vmem_limit_bytes=VMEM_LIMIT),
        name="ssd",
    )(proj3, proj3, proj3, dt3, alog, wx, bx, wbc, bbc, dskip_x, ng, sel)


def _merge_kernel(x_ref, ya_ref, ys_ref, ga_ref, gs_ref, wpa_ref, wps_ref, wo_ref, g_ref,
                  x1_ref, h2_ref):
    pa = _dot(ya_ref[...], wpa_ref[...])
    ps = _dot(ys_ref[...], wps_ref[...])
    ga = 1.0 / (1.0 + jnp.exp(-ga_ref[...].astype(F32)))
    gs = 1.0 / (1.0 + jnp.exp(-gs_ref[...].astype(F32)))
    mixed = (ga * pa + gs * ps).astype(BF16)
    x1 = x_ref[...] + _dot(mixed, wo_ref[...])
    x1_ref[...] = x1
    ms = jnp.mean(x1 * x1, axis=-1, keepdims=True)
    h2_ref[...] = (x1 * lax.rsqrt(ms + RMS_EPS) * g_ref[...]).astype(BF16)


def _merge(x2, y_attn, y_ssm, proj, wpa, wps, wo, g):
    m = x2.shape[0]
    tm = MERGE_TM
    ga_blk = OFF_GATE // D_MODEL

    def rows(width, blk=0):
        return pl.BlockSpec((tm, width), lambda i: (i, blk))

    def const(shape):
        return pl.BlockSpec(shape, lambda i: (0, 0))

    return pl.pallas_call(
        _merge_kernel,
        grid=(m // tm,),
        in_specs=[
            rows(D_MODEL), rows(V_COLS), rows(SSM_D_INNER),
            rows(D_MODEL, ga_blk), rows(D_MODEL, ga_blk + 1),
            const((V_COLS, D_MODEL)), const((SSM_D_INNER, D_MODEL)), const((D_MODEL, D_MODEL)),
            const((1, D_MODEL)),
        ],
        out_specs=[rows(D_MODEL), rows(D_MODEL)],
        out_shape=[jax.ShapeDtypeStruct((m, D_MODEL), F32),
                   jax.ShapeDtypeStruct((m, D_MODEL), BF16)],
        compiler_params=pltpu.CompilerParams(
            dimension_semantics=("parallel",), vmem_limit_bytes=VMEM_LIMIT),
        name="merge",
    )(x2, y_attn, y_ssm, proj, proj, wpa, wps, wo, g)


def _ffn_kernel(x1_ref, h_ref, hh_ref, wup_ref, cw_ref, cb_ref, wd_ref, o_ref, gscr, vscr,
                *, tm, tiles_per_seq):
    i = pl.program_id(0)
    first = (i % tiles_per_seq) == 0
    halo = hh_ref[...]
    halo = jnp.where(first, jnp.zeros_like(halo), halo)
    hrows = halo.shape[0]
    h_ext = jnp.concatenate([halo, h_ref[...]], axis=0)
    taps = cw_ref.shape[0]

    slots = gscr.shape[0]

    def up_project(idx):
        start, width = FFN_SPLITS[idx]
        for col0, scr in ((start, gscr), (D_FF + start, vscr)):
            scr[idx % slots, :, 0:width] = _dot(h_ext, wup_ref[:, col0:col0 + width])

    def conv(idx, col0, scr):
        width = FFN_SPLITS[idx][1]
        cols = slice(col0, col0 + width)
        slot = idx % slots
        out = cb_ref[:, cols] + cw_ref[taps - 1:taps, cols] * scr[slot, hrows:hrows + tm, 0:width]
        for k in range(taps - 1):
            off = hrows - (taps - 1) + k
            out = out + cw_ref[k:k + 1, cols] * scr[slot, off:off + tm, 0:width]
        return out

    def gated(idx):
        start = FFN_SPLITS[idx][0]
        return (_silu(conv(idx, start, gscr)) * conv(idx, D_FF + start, vscr)).astype(BF16)

    n_chunks = len(FFN_SPLITS)
    ahead = slots - 1
    acc = x1_ref[...]
    acts = {}
    for idx in range(min(ahead, n_chunks)):
        up_project(idx)
    for idx in range(n_chunks + 1):
        if idx + ahead < n_chunks:
            up_project(idx + ahead)
        if idx < n_chunks:
            acts[idx] = gated(idx)
        if idx >= 1:
            start, width = FFN_SPLITS[idx - 1]
            acc = acc + _dot(acts.pop(idx - 1), wd_ref[start:start + width, :])

    o_ref[...] = acc


def _ffn(x1, h2, w_up, cw, cb, w_down, seq):
    m = x1.shape[0]
    tm = FFN_TM
    hb = tm // BF16_SUBLANES
    kern = functools.partial(_ffn_kernel, tm=tm, tiles_per_seq=seq // tm)

    def resident(shape):
        return pl.BlockSpec(shape, lambda i: (0, 0), pipeline_mode=pl.Buffered(1))

    return pl.pallas_call(
        kern,
        grid=(m // tm,),
        in_specs=[
            pl.BlockSpec((tm, D_MODEL), lambda i: (i, 0)),
            pl.BlockSpec((tm, D_MODEL), lambda i: (i, 0)),
            pl.BlockSpec((BF16_SUBLANES, D_MODEL), lambda i: (jnp.maximum(i * hb - 1, 0), 0)),
            resident((D_MODEL, 2 * D_FF)),
            resident((FFN_CONV, 2 * D_FF)),
            resident((1, 2 * D_FF)),
            resident((D_FF, D_MODEL)),
        ],
        out_specs=pl.BlockSpec((tm, D_MODEL), lambda i: (i, 0)),
        out_shape=jax.ShapeDtypeStruct((m, D_MODEL), F32),
        scratch_shapes=[
            pltpu.VMEM((FFN_SLOTS, tm + BF16_SUBLANES, FFN_SCR), F32),
            pltpu.VMEM((FFN_SLOTS, tm + BF16_SUBLANES, FFN_SCR), F32),
        ],
        compiler_params=pltpu.CompilerParams(
            dimension_semantics=("parallel",), vmem_limit_bytes=VMEM_LIMIT),
        name="ffn",
    )(x1, h2, h2, w_up, cw, cb, w_down)


def _head_expand(rows, cols, chunk):
    r = lax.broadcasted_iota(jnp.int32, (rows, cols), 0)
    c = lax.broadcasted_iota(jnp.int32, (rows, cols), 1) // chunk
    return (r == c).astype(BF16)


def _layer(x, rel_bias, norm_mix_g, w_in, q_norm_g, k_norm_g, lq1, lk1, lq2, lk2, attn_subln_g,
           conv_ssm_w, conv_ssm_b, dt_bias, a_log, d_skip, ssm_norm_g, w_proj_attn, w_proj_ssm,
           w_out, norm_ffn_g, w_up, conv_ffn_w, conv_ffn_b, w_down):
    b, s, d = x.shape
    m = b * s
    x2 = x.reshape(m, d)

    o_dt = sum(width for name, width in W_SEGMENTS if name != "gate")

    def w_cols(lo, hi):
        parts = []
        if lo < o_dt:
            parts.append(w_in[:, lo:min(hi, o_dt)])
        if hi > o_dt:
            parts.append(w_in[:, max(lo, o_dt) + SSM_HEADS:hi + SSM_HEADS])
        return (parts[0] if len(parts) == 1 else jnp.concatenate(parts, axis=1)).astype(BF16)

    half_cols = PROJ_COLS // IN_HALVES
    w_halves = [w_cols(hv * half_cols, (hv + 1) * half_cols) for hv in range(IN_HALVES)]
    pad_h = LANES - SSM_HEADS
    w_dt = jnp.pad(w_in[:, o_dt:o_dt + SSM_HEADS], ((0, 0), (0, pad_h))).astype(BF16)
    dt_b = jnp.pad(dt_bias.astype(F32), (0, pad_h)).reshape(1, LANES)
    reps = Q_COLS // ATTN_HEAD_DIM
    qk_gain = jnp.stack([
        jnp.tile(q_norm_g.astype(F32) * (ATTN_HEAD_DIM ** -0.5 * LOG2E), reps),
        jnp.tile(k_norm_g.astype(F32), reps)]).reshape(2, 1, Q_COLS)

    proj, dt = _in_proj(x2, norm_mix_g.astype(F32).reshape(1, d), w_halves, w_dt, dt_b, qk_gain)
    proj3 = proj.reshape(b, s, PROJ_COLS)

    vec = lambda v: v.astype(F32).reshape(1, -1)
    bias, far, lam = _bias_build(rel_bias.astype(F32), vec(lq1), vec(lk1), vec(lq2), vec(lk2), ATTN_T)
    y_attn = _diff_attn(lam[0, :1], far[:, 0, 0], proj3, bias,
                        vec(attn_subln_g) * (1.0 - LAM_INIT))

    cw = jnp.broadcast_to(conv_ssm_w.astype(BF16)[:, None, :],
                          (SSM_CONV, BF16_SUBLANES, conv_ssm_w.shape[-1]))
    cbias = conv_ssm_b.astype(F32).reshape(1, -1)
    xs_sl = slice(0, SSM_D_INNER)
    bc_sl = slice(SSM_D_INNER, SSM_D_INNER + 2 * BC_COLS)
    y_ssm = _ssd(
        proj3, dt.reshape(b, s, LANES),
        jnp.pad(a_log.astype(F32), (0, pad_h)).reshape(1, LANES),
        cw[..., xs_sl], cbias[:, xs_sl], cw[..., bc_sl], cbias[:, bc_sl],
        jnp.repeat(d_skip.astype(F32), SSM_HEAD_DIM).reshape(1, SSM_D_INNER),
        vec(ssm_norm_g), jnp.tile(_head_expand(LANES, SSM_D_INNER, SSM_HEAD_DIM), (2, 1)))

    x1, h2 = _merge(x2, y_attn.reshape(m, V_COLS), y_ssm.reshape(m, SSM_D_INNER), proj,
                    w_proj_attn.astype(BF16), w_proj_ssm.astype(BF16), w_out.astype(BF16),
                    vec(norm_ffn_g))

    out = _ffn(x1, h2, w_up.astype(BF16), conv_ffn_w.astype(F32),
               conv_ffn_b.astype(F32).reshape(1, -1), w_down.astype(BF16), s)
    return out.reshape(b, s, d)


def kernel(x, rel_bias, norm_mix_g, w_in, q_norm_g, k_norm_g, lambda_q1, lambda_k1, lambda_q2,
           lambda_k2, attn_subln_g, conv_ssm_w, conv_ssm_b, dt_bias, a_log, d_skip, ssm_norm_g,
           w_proj_attn, w_proj_ssm, w_out, norm_ffn_g, w_up, conv_ffn_w, conv_ffn_b, w_down):
    depth = w_in.shape[0]
    assert depth == 1, "lambda_init and the stacked-parameter layout are specialised to depth 1"
    return _layer(x, rel_bias, norm_mix_g[0], w_in[0], q_norm_g[0], k_norm_g[0], lambda_q1[0],
                  lambda_k1[0], lambda_q2[0], lambda_k2[0], attn_subln_g[0], conv_ssm_w[0],
                  conv_ssm_b[0], dt_bias[0], a_log[0], d_skip[0], ssm_norm_g[0], w_proj_attn[0],
                  w_proj_ssm[0], w_out[0], norm_ffn_g[0], w_up[0], conv_ffn_w[0], conv_ffn_b[0],
                  w_down[0])
```

```python
import functools
import math

import jax
import jax.numpy as jnp
from jax import lax
from jax.experimental import pallas as pl
from jax.experimental.pallas import tpu as pltpu

F32 = jnp.float32
BF16 = jnp.bfloat16

D_MODEL = 1024
ATTN_HEADS = 8
ATTN_HEAD_DIM = 64
ATTN_V_DIM = 2 * ATTN_HEAD_DIM
NUM_BUCKETS = 32
MAX_DISTANCE = 128
SSM_D_INNER = 2048
SSM_HEAD_DIM = 64
SSM_HEADS = 32
SSM_GROUPS = 4
SSM_HEADS_PER_GROUP = SSM_HEADS // SSM_GROUPS
SSM_STATE = 128
SSM_CONV = 4
D_FF = 2816
FFN_CONV = 3
RMS_EPS = 1e-6
SUBLN_EPS = 1e-5
LAM_INIT = 0.8 - 0.6 * math.exp(0.0)

Q_COLS = ATTN_HEADS * 2 * ATTN_HEAD_DIM
V_COLS = ATTN_HEADS * ATTN_V_DIM
BC_COLS = SSM_GROUPS * SSM_STATE
GROUP_COLS = SSM_D_INNER // SSM_GROUPS

LANES = 128
SUBLANES = 8
BF16_SUBLANES = 16
VMEM_LIMIT = 56 * 1024 * 1024

LOG2E = math.log2(math.e)
NEG_BIG = -1e30

W_SEGMENTS = (("q", Q_COLS), ("k", Q_COLS), ("v", V_COLS), ("z", SSM_D_INNER), ("xs", SSM_D_INNER),
              ("b", BC_COLS), ("c", BC_COLS), ("gate", 2 * D_MODEL))
OFF_Z = 0
OFF_Q = OFF_Z + SSM_D_INNER
OFF_K = OFF_Q + Q_COLS
OFF_V = OFF_K + Q_COLS
OFF_B = OFF_V + V_COLS
OFF_C = OFF_B + BC_COLS
OFF_XS = OFF_C + BC_COLS
OFF_GATE = OFF_XS + SSM_D_INNER
PROJ_COLS = OFF_GATE + 2 * D_MODEL
OUT_OFFSETS = {"q": OFF_Q, "k": OFF_K, "v": OFF_V, "z": OFF_Z, "xs": OFF_XS, "b": OFF_B,
               "c": OFF_C, "gate": OFF_GATE}

IN_TM = 512
IN_TN = 512
IN_HALVES = 2
ATTN_T = 128
ATTN_HEADS_PER_STEP = 2
ATTN_LAGS = (2, 5)
SSD_L = 128
SSD_SUB = 4
MERGE_TM = 512
FFN_TM = 512
FFN_SPLITS = ((0, 512), (512, 512), (1024, 512), (1536, 512), (2048, 512), (2560, 256))
FFN_SCR = max(w for _, w in FFN_SPLITS)
FFN_SLOTS = 3


def _split(v, parts):
    out = []
    for _ in range(parts - 1):
        hi = v.astype(BF16)
        out.append(hi)
        v = v - hi.astype(F32)
    out.append(v.astype(BF16))
    return out


def _dot(a, b):
    return jnp.dot(a, b, preferred_element_type=F32)


def _dot_nt(a, b):
    return lax.dot_general(a, b, (((1,), (1,)), ((), ())), preferred_element_type=F32)


def _dot_exact_lhs(sel, parts):
    out = _dot(sel, parts[0])
    for p in parts[1:]:
        out = out + _dot(sel, p)
    return out


def _silu(v):
    h = 0.5 * v
    return h + h * jnp.tanh(h)


def _in_proj_kernel(x0_ref, xn_ref, g_ref, wdt_ref, dtb_ref, qkg_ref, *rest, tile_plan):
    w_refs, (o_ref, dt_ref, h_scr) = rest[:len(tile_plan)], rest[len(tile_plan):]
    i = pl.program_id(0)
    half = pl.program_id(1)
    slot = i % 2

    def normed(x):
        ms = jnp.mean(x * x, axis=-1, keepdims=True)
        return (x * lax.rsqrt(ms + RMS_EPS) * g_ref[...]).astype(BF16)

    @pl.when(jnp.logical_and(i == 0, half == 0))
    def _():
        h_scr[0] = normed(x0_ref[...])

    h = h_scr[slot]

    def tile_cols(t):
        return slice(t * IN_TN, (t + 1) * IN_TN)

    def store_plain(out0, acc):
        o_ref[:, out0:out0 + IN_TN] = acc.astype(BF16)

    def store_head_normed(out0, acc, gain_row, gain_col0):
        lo_half = lax.broadcasted_iota(jnp.int32, (IN_TM, LANES), 1) < ATTN_HEAD_DIM
        for cblk in range(IN_TN // LANES):
            a = acc[:, cblk * LANES:(cblk + 1) * LANES]
            sq = a * a
            s_lo = jnp.sum(jnp.where(lo_half, sq, 0.0), axis=-1, keepdims=True)
            s_hi = jnp.sum(jnp.where(lo_half, 0.0, sq), axis=-1, keepdims=True)
            ssum = jnp.where(lo_half, s_lo, s_hi)
            y = a * lax.rsqrt(ssum * (1.0 / ATTN_HEAD_DIM) + RMS_EPS)
            y = y * qkg_ref[gain_row, :, gain_col0 + cblk * LANES:gain_col0 + (cblk + 1) * LANES]
            o_ref[:, out0 + cblk * LANES:out0 + (cblk + 1) * LANES] = y.astype(BF16)

    for hv, plan in enumerate(tile_plan):
        @pl.when(half == hv)
        def _(plan=plan, hv=hv):
            w_ref = w_refs[hv]
            acc_next = _dot(h, w_ref[:, tile_cols(0)])
            if hv == 0:
                t = _dot(h, wdt_ref[...]) + dtb_ref[...]
                dt_ref[...] = jnp.maximum(t, 0.0) + jnp.log1p(jnp.exp(-jnp.abs(t)))
            if hv == len(tile_plan) - 1:
                h_scr[1 - slot] = normed(xn_ref[...])
            for t, (out0, head_norm) in enumerate(plan):
                acc = acc_next
                if t + 1 < len(plan):
                    acc_next = _dot(h, w_ref[:, tile_cols(t + 1)])
                if head_norm is None:
                    store_plain(out0, acc)
                else:
                    store_head_normed(out0, acc, *head_norm)


def _in_proj_plan():
    half_cols = PROJ_COLS // IN_HALVES
    gain_rows = {"q": 0, "k": 1}
    plan = [[] for _ in range(IN_HALVES)]
    w_off = 0
    for name, width in W_SEGMENTS:
        for col in range(0, width, IN_TN):
            hv, out_col = (w_off + col) // half_cols, OUT_OFFSETS[name] + col
            assert out_col // half_cols == hv and width % IN_TN == 0
            head_norm = (gain_rows[name], col) if name in gain_rows else None
            plan[hv].append((out_col - hv * half_cols, head_norm))
        w_off += width
    return tuple(tuple(p) for p in plan)


def _in_proj(x2, g, w_halves, w_dt, dt_bias, qk_gain):
    m = x2.shape[0]
    n_tiles = m // IN_TM
    half_cols = PROJ_COLS // IN_HALVES
    kern = functools.partial(_in_proj_kernel, tile_plan=_in_proj_plan())
    return pl.pallas_call(
        kern,
        grid=(m // IN_TM, IN_HALVES),
        in_specs=[
            pl.BlockSpec((IN_TM, D_MODEL), lambda i, hv: (0, 0)),
            pl.BlockSpec((IN_TM, D_MODEL), lambda i, hv: (jnp.minimum(i + 1, n_tiles - 1), 0)),
            pl.BlockSpec((1, D_MODEL), lambda i, hv: (0, 0)),
            pl.BlockSpec((D_MODEL, LANES), lambda i, hv: (0, 0)),
            pl.BlockSpec((1, LANES), lambda i, hv: (0, 0)),
            pl.BlockSpec(qk_gain.shape, lambda i, hv: (0, 0, 0)),
        ] + [pl.BlockSpec((D_MODEL, half_cols), lambda i, hv: (0, 0), pipeline_mode=pl.Buffered(1))
             for _ in w_halves],
        out_specs=[
            pl.BlockSpec((IN_TM, half_cols), lambda i, hv: (i, hv)),
            pl.BlockSpec((IN_TM, LANES), lambda i, hv: (i, 0)),
        ],
        out_shape=[
            jax.ShapeDtypeStruct((m, PROJ_COLS), BF16),
            jax.ShapeDtypeStruct((m, LANES), F32),
        ],
        scratch_shapes=[pltpu.VMEM((2, IN_TM, D_MODEL), BF16)],
        compiler_params=pltpu.CompilerParams(
            dimension_semantics=("arbitrary", "arbitrary"), vmem_limit_bytes=VMEM_LIMIT),
        name="in_proj",
    )(x2, x2, g, w_dt, dt_bias, qk_gain, *w_halves)


def _bias_kernel(rel_ref, lq1_ref, lk1_ref, lq2_ref, lk2_ref, bias_ref, far_ref, lam_ref, *, tile):
    h = pl.program_id(0)
    row = lax.broadcasted_iota(jnp.int32, (tile, tile), 0)
    col = lax.broadcasted_iota(jnp.int32, (tile, tile), 1)
    max_exact = NUM_BUCKETS // 2
    for t in range(2):
        d = row - col + t * tile
        n = jnp.maximum(d, 0)
        nf = jnp.maximum(n, 1).astype(F32)
        large = max_exact + (jnp.log(nf / max_exact) / math.log(MAX_DISTANCE / max_exact)
                             * (NUM_BUCKETS - max_exact)).astype(jnp.int32)
        large = jnp.minimum(large, NUM_BUCKETS - 1)
        bucket = jnp.where(n < max_exact, n, large)
        bias = jnp.zeros((tile, tile), F32)
        for bk in range(NUM_BUCKETS):
            bias = jnp.where(bucket == bk, rel_ref[bk, h], bias)
        bias = bias * LOG2E
        if t == 0:
            bias = jnp.where(d >= 0, bias, NEG_BIG)
        bias_ref[0, t] = bias
    far_ref[...] = jnp.full(far_ref.shape, rel_ref[NUM_BUCKETS - 1, h] * LOG2E, F32)
    s1 = jnp.sum(lq1_ref[...] * lk1_ref[...], axis=-1, keepdims=True)
    s2 = jnp.sum(lq2_ref[...] * lk2_ref[...], axis=-1, keepdims=True)
    lam = jnp.exp(s1) - jnp.exp(s2) + LAM_INIT
    lam_ref[...] = jnp.broadcast_to(lam, lam_ref.shape)


def _bias_build(rel_bias, lq1, lk1, lq2, lk2, tile):
    vec = pl.BlockSpec((1, ATTN_HEAD_DIM), lambda h: (0, 0))
    return pl.pallas_call(
        functools.partial(_bias_kernel, tile=tile),
        grid=(ATTN_HEADS,),
        in_specs=[pl.BlockSpec(memory_space=pltpu.SMEM), vec, vec, vec, vec],
        out_specs=[
            pl.BlockSpec((1, 2, tile, tile), lambda h: (h, 0, 0, 0)),
            pl.BlockSpec((1, SUBLANES, LANES), lambda h: (h, 0, 0)),
            pl.BlockSpec((SUBLANES, LANES), lambda h: (0, 0)),
        ],
        out_shape=[
            jax.ShapeDtypeStruct((ATTN_HEADS, 2, tile, tile), F32),
            jax.ShapeDtypeStruct((ATTN_HEADS, SUBLANES, LANES), F32),
            jax.ShapeDtypeStruct((SUBLANES, LANES), F32),
        ],
        compiler_params=pltpu.CompilerParams(dimension_semantics=("arbitrary",)),
        name="bias_build",
    )(rel_bias, lq1, lk1, lq2, lk2)


def _attn_kernel(lam_ref, far_ref, q_ref, k_ref, v_ref, bias_ref, g_ref, o_ref, vext_scr,
                 *, tile, nq, heads):
    h0 = pl.program_id(1) * heads
    lam = lam_ref[0]
    lane = lax.broadcasted_iota(jnp.int32, (tile, LANES), 1)
    lo_half = lane < ATTN_HEAD_DIM
    gain = g_ref[...]

    for j in range(heads):
        vals = v_ref[0, :, j * LANES:(j + 1) * LANES]
        vext_scr[j] = jnp.concatenate([vals, jnp.ones_like(vals)], axis=1)

    def scores(j, qi):
        q = q_ref[0, qi * tile:(qi + 1) * tile, j * LANES:(j + 1) * LANES]
        zero = jnp.zeros_like(q)
        q_both = jnp.concatenate([jnp.where(lo_half, q, zero), jnp.where(lo_half, zero, q)], axis=0)
        return _dot_nt(q_both, k_ref[0, 0:(qi + 1) * tile, j * LANES:(j + 1) * LANES])

    def softmax(j, qi, s):
        far = far_ref[h0 + j]
        nk = (qi + 1) * tile
        n_far = nk - min(qi + 1, 2) * tile
        if qi == 0:
            near_bias = bias_ref[j, 0]
        else:
            near_bias = jnp.concatenate([bias_ref[j, 1], bias_ref[j, 0]], axis=1)
        s_near = s[:, n_far:] + jnp.concatenate([near_bias, near_bias], axis=0)
        m = jnp.max(s_near, axis=-1, keepdims=True)
        if n_far:
            s_far = s[:, :n_far]
            m = jnp.maximum(m, jnp.max(s_far, axis=-1, keepdims=True) + far)
            p = jnp.concatenate([jnp.exp2(s_far - (m - far)), jnp.exp2(s_near - m)], axis=1)
        else:
            p = jnp.exp2(s_near - m)
        return p.astype(BF16)

    def weighted_values(j, qi, p):
        pv = _dot(p, vext_scr[j, 0:(qi + 1) * tile, :])
        return pv[:, :LANES] / pv[:, LANES:]

    units = [(j, qi) for qi in reversed(range(nq)) for j in range(heads)]
    n_units = len(units)
    s_q, p_q = {}, {}
    lag_s, lag_v = ATTN_LAGS
    for step in range(n_units + lag_v):
        if step < n_units:
            s_q[step] = scores(*units[step])
        if 0 <= step - lag_s < n_units:
            p_q[step - lag_s] = softmax(*units[step - lag_s], s_q.pop(step - lag_s))
        if 0 <= step - lag_v < n_units:
            j, qi = units[step - lag_v]
            both = weighted_values(j, qi, p_q.pop(step - lag_v))
            o = both[:tile] - lam * both[tile:]
            ms = jnp.mean(o * o, axis=-1, keepdims=True)
            o = o * lax.rsqrt(ms + SUBLN_EPS) * gain
            o_ref[0, qi * tile:(qi + 1) * tile, j * LANES:(j + 1) * LANES] = o.astype(o_ref.dtype)


def _diff_attn(lam, far, proj3, bias, gain):
    b, s, _ = proj3.shape
    tile = bias.shape[-1]
    nq = s // tile
    heads = ATTN_HEADS_PER_STEP
    width = heads * LANES
    qb, kb, vb = OFF_Q // width, OFF_K // width, OFF_V // width
    kern = functools.partial(_attn_kernel, tile=tile, nq=nq, heads=heads)
    return pl.pallas_call(
        kern,
        grid=(b, ATTN_HEADS // heads),
        in_specs=[
            pl.BlockSpec(memory_space=pltpu.SMEM),
            pl.BlockSpec(memory_space=pltpu.SMEM),
            pl.BlockSpec((1, s, width), lambda i, h: (i, 0, qb + h)),
            pl.BlockSpec((1, s, width), lambda i, h: (i, 0, kb + h)),
            pl.BlockSpec((1, s, width), lambda i, h: (i, 0, vb + h)),
            pl.BlockSpec((heads, 2, tile, tile), lambda i, h: (h, 0, 0, 0)),
            pl.BlockSpec((1, LANES), lambda i, h: (0, 0)),
        ],
        out_specs=pl.BlockSpec((1, s, width), lambda i, h: (i, 0, h)),
        out_shape=jax.ShapeDtypeStruct((b, s, V_COLS), BF16),
        scratch_shapes=[pltpu.VMEM((heads, s, 2 * LANES), BF16)],
        compiler_params=pltpu.CompilerParams(
            dimension_semantics=("parallel", "parallel"), vmem_limit_bytes=VMEM_LIMIT),
        name="diff_attn",
    )(lam, far, proj3, proj3, proj3, bias, gain)


def _shift_matrices(taps, length, halo_rows):
    def build(rows, offset):
        t = lax.broadcasted_iota(jnp.int32, (rows, taps * rows), 0)
        c = lax.broadcasted_iota(jnp.int32, (rows, taps * rows), 1)
        tap = c // rows
        return ((c - tap * rows) == offset + t - (taps - 1 - tap)).astype(BF16)

    return build(length, 0), build(halo_rows, halo_rows)


def _conv_silu(x, halo, w_ref, b_ref, shifts, cols):
    shift_main, shift_head = shifts
    hrows = halo.shape[0]
    def weighted(v):
        v3 = v.reshape(v.shape[0] // hrows, hrows, v.shape[1])
        return jnp.concatenate([(v3 * w_ref[k, :, cols][None]).reshape(v.shape)
                                for k in range(w_ref.shape[0])], axis=0)

    acc = b_ref[:, cols] + _dot(shift_main, weighted(x))
    head = _dot(shift_head, weighted(halo))
    acc = jnp.concatenate([acc[0:hrows] + head, acc[hrows:]], axis=0)
    return _silu(acc)


def _ssd_kernel(xs_ref, bc_ref, z_ref, dt_ref, alog_ref, wx_ref, bx_ref, wbc_ref, bbc_ref,
                dskip_ref, ng_ref, e_ref, y_ref, state_scr, xs_halo, bc_halo,
                *, length, sub_chunks):
    c = pl.program_id(1)

    @pl.when(c == 0)
    def _():
        state_scr[...] = jnp.zeros_like(state_scr)
        xs_halo[...] = jnp.zeros_like(xs_halo)
        bc_halo[...] = jnp.zeros_like(bc_halo)

    shift = _shift_matrices(SSM_CONV, length, BF16_SUBLANES)
    a = -jnp.exp(alog_ref[...])
    row = lax.broadcasted_iota(jnp.int32, (length, length), 0)
    col = lax.broadcasted_iota(jnp.int32, (length, length), 1)
    causal = row >= col
    tri = causal.astype(BF16)
    sel2 = e_ref[...]
    lane = lax.broadcasted_iota(jnp.int32, (length, LANES), 1)
    lo_half = lane < SSM_HEAD_DIM

    def conv(k, main_ref, halo_scr, w_ref, b_ref, cols=slice(None)):
        r0 = k * length
        if k == 0:
            halo = halo_scr[:, cols]
        else:
            halo = main_ref[0, r0 - BF16_SUBLANES:r0, cols]
        return _conv_silu(main_ref[0, r0:r0 + length, cols], halo, w_ref, b_ref, shift, cols)

    def expand(v):
        return jnp.concatenate(_split(v, 2), axis=1)

    def prepare(k):
        rows = slice(k * length, (k + 1) * length)
        bc = conv(k, bc_ref, bc_halo, wbc_ref, bbc_ref).astype(BF16)
        bm, cm = bc[:, :BC_COLS], bc[:, BC_COLS:]
        xs = [conv(k, xs_ref, xs_halo, wx_ref, bx_ref, slice(g * GROUP_COLS, (g + 1) * GROUP_COLS))
              for g in range(SSM_GROUPS)]
        dt = dt_ref[0, rows, :]
        cum = _dot_exact_lhs(tri, _split(dt * a, 3))
        cum2 = cum * LOG2E
        src_t = cum2.T - jnp.log2(dt.T)
        total = cum[length - 1:length, :]
        ecum = expand(jnp.exp(cum))
        edtw = expand(dt * jnp.exp(total - cum))
        return bm, cm, xs, cum2, src_t, ecum, edtw

    def scan(k, prepared):
        bm, cm, xs, cum2, src_t, ecum, edtw = prepared
        rows = slice(k * length, (k + 1) * length)
        for g in range(SSM_GROUPS):
            gs = slice(g * SSM_STATE, (g + 1) * SSM_STATE)
            cs = slice(g * GROUP_COLS, (g + 1) * GROUP_COLS)
            ecum_g = _dot(ecum, sel2[:, cs])
            edtw_g = _dot(edtw, sel2[:, cs])
            xs_f = xs[g]
            b_g = bm[:, gs]
            c_g = cm[:, gs]
            xs_g = xs_f.astype(BF16)
            cb = _dot_nt(c_g, b_g)
            st = state_scr[g]
            y_g = _dot(c_g, st.astype(BF16)) * ecum_g
            xw = (xs_f * edtw_g).astype(BF16)
            b_t = b_g.astype(F32).T.astype(BF16)
            state_scr[g] = st * ecum_g[length - 1:length, :] + _dot(b_t, xw)
            y_parts = []
            for pair in range(SSM_HEADS_PER_GROUP // 2):
                ms = []
                for hh in range(2):
                    h = g * SSM_HEADS_PER_GROUP + pair * 2 + hh
                    seg = jnp.where(causal, cum2[:, h:h + 1] - src_t[h:h + 1, :], NEG_BIG)
                    ms.append((cb * jnp.exp2(seg)).astype(BF16))
                x_pair = xs_g[:, pair * LANES:(pair + 1) * LANES]
                zero = jnp.zeros_like(x_pair)
                rhs = jnp.concatenate([jnp.where(lo_half, x_pair, zero),
                                       jnp.where(lo_half, zero, x_pair)], axis=0)
                y_parts.append(_dot(jnp.concatenate(ms, axis=1), rhs))
            y_g = y_g + jnp.concatenate(y_parts, axis=1)
            y_g = y_g + xs_f * dskip_ref[:, cs]
            y_g = y_g * _silu(z_ref[0, rows, cs].astype(F32))
            msq = jnp.mean(y_g * y_g, axis=-1, keepdims=True)
            y_g = y_g * lax.rsqrt(msq + SUBLN_EPS) * ng_ref[:, cs]
            y_ref[0, rows, cs] = y_g.astype(y_ref.dtype)

    prepared = prepare(0)
    for k in range(sub_chunks):
        current = prepared
        if k + 1 < sub_chunks:
            prepared = prepare(k + 1)
        scan(k, current)

    last = sub_chunks * length
    xs_halo[...] = xs_ref[0, last - BF16_SUBLANES:last, :]
    bc_halo[...] = bc_ref[0, last - BF16_SUBLANES:last, :]


def _ssd(proj3, dt3, alog, wx, bx, wbc, bbc, dskip_x, ng, sel):
    b, s, _ = proj3.shape
    length = SSD_L
    rows = SSD_SUB * length
    nc = s // rows
    xs_blk, z_blk = OFF_XS // SSM_D_INNER, OFF_Z // SSM_D_INNER
    assert OFF_C == OFF_B + BC_COLS and OFF_B % (2 * BC_COLS) == 0
    bc_blk = OFF_B // (2 * BC_COLS)

    def main(width, blk):
        return pl.BlockSpec((1, rows, width), lambda i, c: (i, c, blk))

    def const(shape):
        return pl.BlockSpec(shape, lambda i, c: (0,) * len(shape))

    kern = functools.partial(_ssd_kernel, length=length, sub_chunks=SSD_SUB)
    return pl.pallas_call(
        kern,
        grid=(b, nc),
        in_specs=[
            main(SSM_D_INNER, xs_blk), main(2 * BC_COLS, bc_blk), main(SSM_D_INNER, z_blk),
            pl.BlockSpec((1, rows, LANES), lambda i, c: (i, c, 0)),
            const((1, LANES)),
            const((SSM_CONV, BF16_SUBLANES, SSM_D_INNER)), const((1, SSM_D_INNER)),
            const((SSM_CONV, BF16_SUBLANES, 2 * BC_COLS)), const((1, 2 * BC_COLS)),
            const((1, SSM_D_INNER)), const((1, SSM_D_INNER)),
            const((2 * LANES, SSM_D_INNER)),
        ],
        out_specs=pl.BlockSpec((1, rows, SSM_D_INNER), lambda i, c: (i, c, 0)),
        out_shape=jax.ShapeDtypeStruct((b, s, SSM_D_INNER), BF16),
        scratch_shapes=[
            pltpu.VMEM((SSM_GROUPS, SSM_STATE, GROUP_COLS), F32),
            pltpu.VMEM((BF16_SUBLANES, SSM_D_INNER), BF16),
            pltpu.VMEM((BF16_SUBLANES, 2 * BC_COLS), BF16),
        ],
        compiler_params=pltpu.CompilerParams(
            dimension_semantics=("parallel", "arbitrary"), vmem_limit_bytes=VMEM_LIMIT),
        name="ssd",
    )(proj3, proj3, proj3, dt3, alog, wx, bx, wbc, bbc, dskip_x, ng, sel)


def _merge_kernel(x_ref, ya_ref, ys_ref, ga_ref, gs_ref, wpa_ref, wps_ref, wo_ref, g_ref,
                  x1_ref, h2_ref):
    pa = _dot(ya_ref[...], wpa_ref[...])
    ps = _dot(ys_ref[...], wps_ref[...])
    ga = 1.0 / (1.0 + jnp.exp(-ga_ref[...].astype(F32)))
    gs = 1.0 / (1.0 + jnp.exp(-gs_ref[...].astype(F32)))
    mixed = (ga * pa + gs * ps).astype(BF16)
    x1 = x_ref[...] + _dot(mixed, wo_ref[...])
    x1_ref[...] = x1
    ms = jnp.mean(x1 * x1, axis=-1, keepdims=True)
    h2_ref[...] = (x1 * lax.rsqrt(ms + RMS_EPS) * g_ref[...]).astype(BF16)


def _merge(x2, y_attn, y_ssm, proj, wpa, wps, wo, g):
    m = x2.shape[0]
    tm = MERGE_TM
    ga_blk = OFF_GATE // D_MODEL

    def rows(width, blk=0):
        return pl.BlockSpec((tm, width), lambda i: (i, blk))

    def const(shape):
        return pl.BlockSpec(shape, lambda i: (0, 0))

    return pl.pallas_call(
        _merge_kernel,
        grid=(m // tm,),
        in_specs=[
            rows(D_MODEL), rows(V_COLS), rows(SSM_D_INNER),
            rows(D_MODEL, ga_blk), rows(D_MODEL, ga_blk + 1),
            const((V_COLS, D_MODEL)), const((SSM_D_INNER, D_MODEL)), const((D_MODEL, D_MODEL)),
            const((1, D_MODEL)),
        ],
        out_specs=[rows(D_MODEL), rows(D_MODEL)],
        out_shape=[jax.ShapeDtypeStruct((m, D_MODEL), F32),
                   jax.ShapeDtypeStruct((m, D_MODEL), BF16)],
        compiler_params=pltpu.CompilerParams(
            dimension_semantics=("parallel",), vmem_limit_bytes=VMEM_LIMIT),
        name="merge",
    )(x2, y_attn, y_ssm, proj, proj, wpa, wps, wo, g)


def _ffn_kernel(x1_ref, h_ref, hh_ref, wup_ref, cw_ref, cb_ref, wd_ref, o_ref, gscr, vscr,
                *, tm, tiles_per_seq):
    i = pl.program_id(0)
    first = (i % tiles_per_seq) == 0
    halo = hh_ref[...]
    halo = jnp.where(first, jnp.zeros_like(halo), halo)
    hrows = halo.shape[0]
    h_ext = jnp.concatenate([halo, h_ref[...]], axis=0)
    taps = cw_ref.shape[0]

    slots = gscr.shape[0]

    def up_project(idx):
        start, width = FFN_SPLITS[idx]
        for col0, scr in ((start, gscr), (D_FF + start, vscr)):
            scr[idx % slots, :, 0:width] = _dot(h_ext, wup_ref[:, col0:col0 + width])

    def conv(idx, col0, scr):
        width = FFN_SPLITS[idx][1]
        cols = slice(col0, col0 + width)
        slot = idx % slots
        out = cb_ref[:, cols] + cw_ref[taps - 1:taps, cols] * scr[slot, hrows:hrows + tm, 0:width]
        for k in range(taps - 1):
            off = hrows - (taps - 1) + k
            out = out + cw_ref[k:k + 1, cols] * scr[slot, off:off + tm, 0:width]
        return out

    def gated(idx):
        start = FFN_SPLITS[idx][0]
        return (_silu(conv(idx, start, gscr)) * conv(idx, D_FF + start, vscr)).astype(BF16)

    n_chunks = len(FFN_SPLITS)
    ahead = slots - 1
    acc = x1_ref[...]
    acts = {}
    for idx in range(min(ahead, n_chunks)):
        up_project(idx)
    for idx in range(n_chunks + 1):
        if idx + ahead < n_chunks:
            up_project(idx + ahead)
        if idx < n_chunks:
            acts[idx] = gated(idx)
        if idx >= 1:
            start, width = FFN_SPLITS[idx - 1]
            acc = acc + _dot(acts.pop(idx - 1), wd_ref[start:start + width, :])

    o_ref[...] = acc


def _ffn(x1, h2, w_up, cw, cb, w_down, seq):
    m = x1.shape[0]
    tm = FFN_TM
    hb = tm // BF16_SUBLANES
    kern = functools.partial(_ffn_kernel, tm=tm, tiles_per_seq=seq // tm)

    def resident(shape):
        return pl.BlockSpec(shape, lambda i: (0, 0), pipeline_mode=pl.Buffered(1))

    return pl.pallas_call(
        kern,
        grid=(m // tm,),
        in_specs=[
            pl.BlockSpec((tm, D_MODEL), lambda i: (i, 0)),
            pl.BlockSpec((tm, D_MODEL), lambda i: (i, 0)),
            pl.BlockSpec((BF16_SUBLANES, D_MODEL), lambda i: (jnp.maximum(i * hb - 1, 0), 0)),
            resident((D_MODEL, 2 * D_FF)),
            resident((FFN_CONV, 2 * D_FF)),
            resident((1, 2 * D_FF)),
            resident((D_FF, D_MODEL)),
        ],
        out_specs=pl.BlockSpec((tm, D_MODEL), lambda i: (i, 0)),
        out_shape=jax.ShapeDtypeStruct((m, D_MODEL), F32),
        scratch_shapes=[
            pltpu.VMEM((FFN_SLOTS, tm + BF16_SUBLANES, FFN_SCR), F32),
            pltpu.VMEM((FFN_SLOTS, tm + BF16_SUBLANES, FFN_SCR), F32),
        ],
        compiler_params=pltpu.CompilerParams(
            dimension_semantics=("parallel",), vmem_limit_bytes=VMEM_LIMIT),
        name="ffn",
    )(x1, h2, h2, w_up, cw, cb, w_down)


def _head_expand(rows, cols, chunk):
    r = lax.broadcasted_iota(jnp.int32, (rows, cols), 0)
    c = lax.broadcasted_iota(jnp.int32, (rows, cols), 1) // chunk
    return (r == c).astype(BF16)


def _layer(x, rel_bias, norm_mix_g, w_in, q_norm_g, k_norm_g, lq1, lk1, lq2, lk2, attn_subln_g,
           conv_ssm_w, conv_ssm_b, dt_bias, a_log, d_skip, ssm_norm_g, w_proj_attn, w_proj_ssm,
           w_out, norm_ffn_g, w_up, conv_ffn_w, conv_ffn_b, w_down):
    b, s, d = x.shape
    m = b * s
    x2 = x.reshape(m, d)

    o_dt = sum(width for name, width in W_SEGMENTS if name != "gate")

    def w_cols(lo, hi):
        parts = []
        if lo < o_dt:
            parts.append(w_in[:, lo:min(hi, o_dt)])
        if hi > o_dt:
            parts.append(w_in[:, max(lo, o_dt) + SSM_HEADS:hi + SSM_HEADS])
        return (parts[0] if len(parts) == 1 else jnp.concatenate(parts, axis=1)).astype(BF16)

    half_cols = PROJ_COLS // IN_HALVES
    w_halves = [w_cols(hv * half_cols, (hv + 1) * half_cols) for hv in range(IN_HALVES)]
    pad_h = LANES - SSM_HEADS
    w_dt = jnp.pad(w_in[:, o_dt:o_dt + SSM_HEADS], ((0, 0), (0, pad_h))).astype(BF16)
    dt_b = jnp.pad(dt_bias.astype(F32), (0, pad_h)).reshape(1, LANES)
    reps = Q_COLS // ATTN_HEAD_DIM
    qk_gain = jnp.stack([
        jnp.tile(q_norm_g.astype(F32) * (ATTN_HEAD_DIM ** -0.5 * LOG2E), reps),
        jnp.tile(k_norm_g.astype(F32), reps)]).reshape(2, 1, Q_COLS)

    proj, dt = _in_proj(x2, norm_mix_g.astype(F32).reshape(1, d), w_halves, w_dt, dt_b, qk_gain)
    proj3 = proj.reshape(b, s, PROJ_COLS)

    vec = lambda v: v.astype(F32).reshape(1, -1)
    bias, far, lam = _bias_build(rel_bias.astype(F32), vec(lq1), vec(lk1), vec(lq2), vec(lk2), ATTN_T)
    y_attn = _diff_attn(lam[0, :1], far[:, 0, 0], proj3, bias,
                        vec(attn_subln_g) * (1.0 - LAM_INIT))

    cw = jnp.broadcast_to(conv_ssm_w.astype(BF16)[:, None, :],
                          (SSM_CONV, BF16_SUBLANES, conv_ssm_w.shape[-1]))
    cbias = conv_ssm_b.astype(F32).reshape(1, -1)
    xs_sl = slice(0, SSM_D_INNER)
    bc_sl = slice(SSM_D_INNER, SSM_D_INNER + 2 * BC_COLS)
    y_ssm = _ssd(
        proj3, dt.reshape(b, s, LANES),
        jnp.pad(a_log.astype(F32), (0, pad_h)).reshape(1, LANES),
        cw[..., xs_sl], cbias[:, xs_sl], cw[..., bc_sl], cbias[:, bc_sl],
        jnp.repeat(d_skip.astype(F32), SSM_HEAD_DIM).reshape(1, SSM_D_INNER),
        vec(ssm_norm_g), jnp.tile(_head_expand(LANES, SSM_D_INNER, SSM_HEAD_DIM), (2, 1)))

    x1, h2 = _merge(x2, y_attn.reshape(m, V_COLS), y_ssm.reshape(m, SSM_D_INNER), proj,
                    w_proj_attn.astype(BF16), w_proj_ssm.astype(BF16), w_out.astype(BF16),
                    vec(norm_ffn_g))

    out = _ffn(x1, h2, w_up.astype(BF16), conv_ffn_w.astype(F32),
               conv_ffn_b.astype(F32).reshape(1, -1), w_down.astype(BF16), s)
    return out.reshape(b, s, d)


def kernel(x, rel_bias, norm_mix_g, w_in, q_norm_g, k_norm_g, lambda_q1, lambda_k1, lambda_q2,
           lambda_k2, attn_subln_g, conv_ssm_w, conv_ssm_b, dt_bias, a_log, d_skip, ssm_norm_g,
           w_proj_attn, w_proj_ssm, w_out, norm_ffn_g, w_up, conv_ffn_w, conv_ffn_b, w_down):
    depth = w_in.shape[0]
    assert depth == 1, "lambda_init and the stacked-parameter layout are specialised to depth 1"
    return _layer(x, rel_bias, norm_mix_g[0], w_in[0], q_norm_g[0], k_norm_g[0], lambda_q1[0],
                  lambda_k1[0], lambda_q2[0], lambda_k2[0], attn_subln_g[0], conv_ssm_w[0],
                  conv_ssm_b[0], dt_bias[0], a_log[0], d_skip[0], ssm_norm_g[0], w_proj_attn[0],
                  w_proj_ssm[0], w_out[0], norm_ffn_g[0], w_up[0], conv_ffn_w[0], conv_ffn_b[0],
                  w_down[0])
```

```python
import functools
import math

import jax
import jax.numpy as jnp
from jax import lax
from jax.experimental import pallas as pl
from jax.experimental.pallas import tpu as pltpu

F32 = jnp.float32
BF16 = jnp.bfloat16

D_MODEL = 1024
ATTN_HEADS = 8
ATTN_HEAD_DIM = 64
ATTN_V_DIM = 2 * ATTN_HEAD_DIM
NUM_BUCKETS = 32
MAX_DISTANCE = 128
SSM_D_INNER = 2048
SSM_HEAD_DIM = 64
SSM_HEADS = 32
SSM_GROUPS = 4
SSM_HEADS_PER_GROUP = SSM_HEADS // SSM_GROUPS
SSM_STATE = 128
SSM_CONV = 4
D_FF = 2816
FFN_CONV = 3
RMS_EPS = 1e-6
SUBLN_EPS = 1e-5
LAM_INIT = 0.8 - 0.6 * math.exp(0.0)

Q_COLS = ATTN_HEADS * 2 * ATTN_HEAD_DIM
V_COLS = ATTN_HEADS * ATTN_V_DIM
BC_COLS = SSM_GROUPS * SSM_STATE
GROUP_COLS = SSM_D_INNER // SSM_GROUPS

LANES = 128
SUBLANES = 8
BF16_SUBLANES = 16
VMEM_LIMIT = 56 * 1024 * 1024

LOG2E = math.log2(math.e)
NEG_BIG = -1e30

W_SEGMENTS = (("q", Q_COLS), ("k", Q_COLS), ("v", V_COLS), ("z", SSM_D_INNER), ("xs", SSM_D_INNER),
              ("b", BC_COLS), ("c", BC_COLS), ("gate", 2 * D_MODEL))
OFF_Z = 0
OFF_Q = OFF_Z + SSM_D_INNER
OFF_K = OFF_Q + Q_COLS
OFF_V = OFF_K + Q_COLS
OFF_B = OFF_V + V_COLS
OFF_C = OFF_B + BC_COLS
OFF_XS = OFF_C + BC_COLS
OFF_GATE = OFF_XS + SSM_D_INNER
PROJ_COLS = OFF_GATE + 2 * D_MODEL
OUT_OFFSETS = {"q": OFF_Q, "k": OFF_K, "v": OFF_V, "z": OFF_Z, "xs": OFF_XS, "b": OFF_B,
               "c": OFF_C, "gate": OFF_GATE}

IN_TM = 512
IN_TN = 512
IN_HALVES = 2
ATTN_T = 128
ATTN_HEADS_PER_STEP = 2
ATTN_LAGS = (2, 5)
SSD_L = 128
SSD_SUB = 4
MERGE_TM = 512
FFN_TM = 512
FFN_SPLITS = ((0, 512), (512, 512), (1024, 512), (1536, 512), (2048, 512), (2560, 256))
FFN_SCR = max(w for _, w in FFN_SPLITS)
FFN_SLOTS = 3


def _split(v, parts):
    out = []
    for _ in range(parts - 1):
        hi = v.astype(BF16)
        out.append(hi)
        v = v - hi.astype(F32)
    out.append(v.astype(BF16))
    return out


def _dot(a, b):
    return jnp.dot(a, b, preferred_element_type=F32)


def _dot_nt(a, b):
    return lax.dot_general(a, b, (((1,), (1,)), ((), ())), preferred_element_type=F32)


def _dot_exact_lhs(sel, parts):
    out = _dot(sel, parts[0])
    for p in parts[1:]:
        out = out + _dot(sel, p)
    return out


def _silu(v):
    h = 0.5 * v
    return h + h * jnp.tanh(h)


def _in_proj_kernel(x0_ref, xn_ref, g_ref, wdt_ref, dtb_ref, qkg_ref, *rest, tile_plan):
    w_refs, (o_ref, dt_ref, h_scr) = rest[:len(tile_plan)], rest[len(tile_plan):]
    i = pl.program_id(0)
    half = pl.program_id(1)
    slot = i % 2

    def normed(x):
        ms = jnp.mean(x * x, axis=-1, keepdims=True)
        return (x * lax.rsqrt(ms + RMS_EPS) * g_ref[...]).astype(BF16)

    @pl.when(jnp.logical_and(i == 0, half == 0))
    def _():
        h_scr[0] = normed(x0_ref[...])

    h = h_scr[slot]

    def tile_cols(t):
        return slice(t * IN_TN, (t + 1) * IN_TN)

    def store_plain(out0, acc):
        o_ref[:, out0:out0 + IN_TN] = acc.astype(BF16)

    def store_head_normed(out0, acc, gain_row, gain_col0):
        lo_half = lax.broadcasted_iota(jnp.int32, (IN_TM, LANES), 1) < ATTN_HEAD_DIM
        for cblk in range(IN_TN // LANES):
            a = acc[:, cblk * LANES:(cblk + 1) * LANES]
            sq = a * a
            s_lo = jnp.sum(jnp.where(lo_half, sq, 0.0), axis=-1, keepdims=True)
            s_hi = jnp.sum(jnp.where(lo_half, 0.0, sq), axis=-1, keepdims=True)
            ssum = jnp.where(lo_half, s_lo, s_hi)
            y = a * lax.rsqrt(ssum * (1.0 / ATTN_HEAD_DIM) + RMS_EPS)
            y = y * qkg_ref[gain_row, :, gain_col0 + cblk * LANES:gain_col0 + (cblk + 1) * LANES]
            o_ref[:, out0 + cblk * LANES:out0 + (cblk + 1) * LANES] = y.astype(BF16)

    for hv, plan in enumerate(tile_plan):
        @pl.when(half == hv)
        def _(plan=plan, hv=hv):
            w_ref = w_refs[hv]
            acc_next = _dot(h, w_ref[:, tile_cols(0)])
            if hv == 0:
                t = _dot(h, wdt_ref[...]) + dtb_ref[...]
                dt_ref[...] = jnp.maximum(t, 0.0) + jnp.log1p(jnp.exp(-jnp.abs(t)))
            if hv == len(tile_plan) - 1:
                h_scr[1 - slot] = normed(xn_ref[...])
            for t, (out0, head_norm) in enumerate(plan):
                acc = acc_next
                if t + 1 < len(plan):
                    acc_next = _dot(h, w_ref[:, tile_cols(t + 1)])
                if head_norm is None:
                    store_plain(out0, acc)
                else:
                    store_head_normed(out0, acc, *head_norm)


def _in_proj_plan():
    half_cols = PROJ_COLS // IN_HALVES
    gain_rows = {"q": 0, "k": 1}
    plan = [[] for _ in range(IN_HALVES)]
    w_off = 0
    for name, width in W_SEGMENTS:
        for col in range(0, width, IN_TN):
            hv, out_col = (w_off + col) // half_cols, OUT_OFFSETS[name] + col
            assert out_col // half_cols == hv and width % IN_TN == 0
            head_norm = (gain_rows[name], col) if name in gain_rows else None
            plan[hv].append((out_col - hv * half_cols, head_norm))
        w_off += width
    return tuple(tuple(p) for p in plan)


def _in_proj(x2, g, w_halves, w_dt, dt_bias, qk_gain):
    m = x2.shape[0]
    n_tiles = m // IN_TM
    half_cols = PROJ_COLS // IN_HALVES
    kern = functools.partial(_in_proj_kernel, tile_plan=_in_proj_plan())
    return pl.pallas_call(
        kern,
        grid=(m // IN_TM, IN_HALVES),
        in_specs=[
            pl.BlockSpec((IN_TM, D_MODEL), lambda i, hv: (0, 0)),
            pl.BlockSpec((IN_TM, D_MODEL), lambda i, hv: (jnp.minimum(i + 1, n_tiles - 1), 0)),
            pl.BlockSpec((1, D_MODEL), lambda i, hv: (0, 0)),
            pl.BlockSpec((D_MODEL, LANES), lambda i, hv: (0, 0)),
            pl.BlockSpec((1, LANES), lambda i, hv: (0, 0)),
            pl.BlockSpec(qk_gain.shape, lambda i, hv: (0, 0, 0)),
        ] + [pl.BlockSpec((D_MODEL, half_cols), lambda i, hv: (0, 0), pipeline_mode=pl.Buffered(1))
             for _ in w_halves],
        out_specs=[
            pl.BlockSpec((IN_TM, half_cols), lambda i, hv: (i, hv)),
            pl.BlockSpec((IN_TM, LANES), lambda i, hv: (i, 0)),
        ],
        out_shape=[
            jax.ShapeDtypeStruct((m, PROJ_COLS), BF16),
            jax.ShapeDtypeStruct((m, LANES), F32),
        ],
        scratch_shapes=[pltpu.VMEM((2, IN_TM, D_MODEL), BF16)],
        compiler_params=pltpu.CompilerParams(
            dimension_semantics=("arbitrary", "arbitrary"), vmem_limit_bytes=VMEM_LIMIT),
        name="in_proj",
    )(x2, x2, g, w_dt, dt_bias, qk_gain, *w_halves)


def _bias_kernel(rel_ref, lq1_ref, lk1_ref, lq2_ref, lk2_ref, bias_ref, far_ref, lam_ref, *, tile):
    h = pl.program_id(0)
    row = lax.broadcasted_iota(jnp.int32, (tile, tile), 0)
    col = lax.broadcasted_iota(jnp.int32, (tile, tile), 1)
    max_exact = NUM_BUCKETS // 2
    for t in range(2):
        d = row - col + t * tile
        n = jnp.maximum(d, 0)
        nf = jnp.maximum(n, 1).astype(F32)
        large = max_exact + (jnp.log(nf / max_exact) / math.log(MAX_DISTANCE / max_exact)
                             * (NUM_BUCKETS - max_exact)).astype(jnp.int32)
        large = jnp.minimum(large, NUM_BUCKETS - 1)
        bucket = jnp.where(n < max_exact, n, large)
        bias = jnp.zeros((tile, tile), F32)
        for bk in range(NUM_BUCKETS):
            bias = jnp.where(bucket == bk, rel_ref[bk, h], bias)
        bias = bias * LOG2E
        if t == 0:
            bias = jnp.where(d >= 0, bias, NEG_BIG)
        bias_ref[0, t] = bias
    far_ref[...] = jnp.full(far_ref.shape, rel_ref[NUM_BUCKETS - 1, h] * LOG2E, F32)
    s1 = jnp.sum(lq1_ref[...] * lk1_ref[...], axis=-1, keepdims=True)
    s2 = jnp.sum(lq2_ref[...] * lk2_ref[...], axis=-1, keepdims=True)
    lam = jnp.exp(s1) - jnp.exp(s2) + LAM_INIT
    lam_ref[...] = jnp.broadcast_to(lam, lam_ref.shape)


def _bias_build(rel_bias, lq1, lk1, lq2, lk2, tile):
    vec = pl.BlockSpec((1, ATTN_HEAD_DIM), lambda h: (0, 0))
    return pl.pallas_call(
        functools.partial(_bias_kernel, tile=tile),
        grid=(ATTN_HEADS,),
        in_specs=[pl.BlockSpec(memory_space=pltpu.SMEM), vec, vec, vec, vec],
        out_specs=[
            pl.BlockSpec((1, 2, tile, tile), lambda h: (h, 0, 0, 0)),
            pl.BlockSpec((1, SUBLANES, LANES), lambda h: (h, 0, 0)),
            pl.BlockSpec((SUBLANES, LANES), lambda h: (0, 0)),
        ],
        out_shape=[
            jax.ShapeDtypeStruct((ATTN_HEADS, 2, tile, tile), F32),
            jax.ShapeDtypeStruct((ATTN_HEADS, SUBLANES, LANES), F32),
            jax.ShapeDtypeStruct((SUBLANES, LANES), F32),
        ],
        compiler_params=pltpu.CompilerParams(dimension_semantics=("arbitrary",)),
        name="bias_build",
    )(rel_bias, lq1, lk1, lq2, lk2)


def _attn_kernel(lam_ref, far_ref, q_ref, k_ref, v_ref, bias_ref, g_ref, o_ref, vext_scr,
                 *, tile, nq, heads):
    h0 = pl.program_id(1) * heads
    lam = lam_ref[0]
    lane = lax.broadcasted_iota(jnp.int32, (tile, LANES), 1)
    lo_half = lane < ATTN_HEAD_DIM
    gain = g_ref[...]

    for j in range(heads):
        vals = v_ref[0, :, j * LANES:(j + 1) * LANES]
        vext_scr[j] = jnp.concatenate([vals, jnp.ones_like(vals)], axis=1)

    def scores(j, qi):
        q = q_ref[0, qi * tile:(qi + 1) * tile, j * LANES:(j + 1) * LANES]
        zero = jnp.zeros_like(q)
        q_both = jnp.concatenate([jnp.where(lo_half, q, zero), jnp.where(lo_half, zero, q)], axis=0)
        return _dot_nt(q_both, k_ref[0, 0:(qi + 1) * tile, j * LANES:(j + 1) * LANES])

    def softmax(j, qi, s):
        far = far_ref[h0 + j]
        nk = (qi + 1) * tile
        n_far = nk - min(qi + 1, 2) * tile
        if qi == 0:
            near_bias = bias_ref[h0 + j, 0]
        else:
            near_bias = jnp.concatenate([bias_ref[h0 + j, 1], bias_ref[h0 + j, 0]], axis=1)
        s_near = s[:, n_far:] + jnp.concatenate([near_bias, near_bias], axis=0)
        m = jnp.max(s_near, axis=-1, keepdims=True)
        if n_far:
            s_far = s[:, :n_far]
            m = jnp.maximum(m, jnp.max(s_far, axis=-1, keepdims=True) + far)
            p = jnp.concatenate([jnp.exp2(s_far - (m - far)), jnp.exp2(s_near - m)], axis=1)
        else:
            p = jnp.exp2(s_near - m)
        return p.astype(BF16)

    def weighted_values(j, qi, p):
        pv = _dot(p, vext_scr[j, 0:(qi + 1) * tile, :])
        return pv[:, :LANES] / pv[:, LANES:]

    units = [(j, qi) for qi in reversed(range(nq)) for j in range(heads)]
    n_units = len(units)
    s_q, p_q = {}, {}
    lag_s, lag_v = ATTN_LAGS
    for step in range(n_units + lag_v):
        if step < n_units:
            s_q[step] = scores(*units[step])
        if 0 <= step - lag_s < n_units:
            p_q[step - lag_s] = softmax(*units[step - lag_s], s_q.pop(step - lag_s))
        if 0 <= step - lag_v < n_units:
            j, qi = units[step - lag_v]
            both = weighted_values(j, qi, p_q.pop(step - lag_v))
            o = both[:tile] - lam * both[tile:]
            ms = jnp.mean(o * o, axis=-1, keepdims=True)
            o = o * lax.rsqrt(ms + SUBLN_EPS) * gain
            o_ref[0, qi * tile:(qi + 1) * tile, j * LANES:(j + 1) * LANES] = o.astype(o_ref.dtype)


def _diff_attn(lam, far, proj3, bias, gain):
    b, s, _ = proj3.shape
    tile = bias.shape[-1]
    nq = s // tile
    heads = ATTN_HEADS_PER_STEP
    width = heads * LANES
    qb, kb, vb = OFF_Q // width, OFF_K // width, OFF_V // width
    kern = functools.partial(_attn_kernel, tile=tile, nq=nq, heads=heads)
    return pl.pallas_call(
        kern,
        grid=(b, ATTN_HEADS // heads),
        in_specs=[
            pl.BlockSpec(memory_space=pltpu.SMEM),
            pl.BlockSpec(memory_space=pltpu.SMEM),
            pl.BlockSpec((1, s, width), lambda i, h: (i, 0, qb + h)),
            pl.BlockSpec((1, s, width), lambda i, h: (i, 0, kb + h)),
            pl.BlockSpec((1, s, width), lambda i, h: (i, 0, vb + h)),
            pl.BlockSpec((ATTN_HEADS, 2, tile, tile), lambda i, h: (0, 0, 0, 0)),
            pl.BlockSpec((1, LANES), lambda i, h: (0, 0)),
        ],
        out_specs=pl.BlockSpec((1, s, width), lambda i, h: (i, 0, h)),
        out_shape=jax.ShapeDtypeStruct((b, s, V_COLS), BF16),
        scratch_shapes=[pltpu.VMEM((heads, s, 2 * LANES), BF16)],
        compiler_params=pltpu.CompilerParams(
            dimension_semantics=("parallel", "parallel"), vmem_limit_bytes=VMEM_LIMIT),
        name="diff_attn",
    )(lam, far, proj3, proj3, proj3, bias, gain)


def _shift_matrices(taps, length, halo_rows):
    def build(rows, offset):
        t = lax.broadcasted_iota(jnp.int32, (rows, taps * rows), 0)
        c = lax.broadcasted_iota(jnp.int32, (rows, taps * rows), 1)
        tap = c // rows
        return ((c - tap * rows) == offset + t - (taps - 1 - tap)).astype(BF16)

    return build(length, 0), build(halo_rows, halo_rows)


def _conv_silu(x, halo, w_ref, b_ref, shifts, cols):
    shift_main, shift_head = shifts
    hrows = halo.shape[0]
    def weighted(v):
        v3 = v.reshape(v.shape[0] // hrows, hrows, v.shape[1])
        return jnp.concatenate([(v3 * w_ref[k, :, cols][None]).reshape(v.shape)
                                for k in range(w_ref.shape[0])], axis=0)

    acc = b_ref[:, cols] + _dot(shift_main, weighted(x))
    head = _dot(shift_head, weighted(halo))
    acc = jnp.concatenate([acc[0:hrows] + head, acc[hrows:]], axis=0)
    return _silu(acc)


def _ssd_kernel(xs_ref, bc_ref, z_ref, dt_ref, alog_ref, wx_ref, bx_ref, wbc_ref, bbc_ref,
                dskip_ref, ng_ref, e_ref, y_ref, state_scr, xs_halo, bc_halo,
                *, length, sub_chunks):
    c = pl.program_id(1)

    @pl.when(c == 0)
    def _():
        state_scr[...] = jnp.zeros_like(state_scr)
        xs_halo[...] = jnp.zeros_like(xs_halo)
        bc_halo[...] = jnp.zeros_like(bc_halo)

    shift = _shift_matrices(SSM_CONV, length, BF16_SUBLANES)
    a = -jnp.exp(alog_ref[...])
    row = lax.broadcasted_iota(jnp.int32, (length, length), 0)
    col = lax.broadcasted_iota(jnp.int32, (length, length), 1)
    causal = row >= col
    tri = causal.astype(BF16)
    sel2 = e_ref[...]
    lane = lax.broadcasted_iota(jnp.int32, (length, LANES), 1)
    lo_half = lane < SSM_HEAD_DIM

    def conv(k, main_ref, halo_scr, w_ref, b_ref, cols=slice(None)):
        r0 = k * length
        if k == 0:
            halo = halo_scr[:, cols]
        else:
            halo = main_ref[0, r0 - BF16_SUBLANES:r0, cols]
        return _conv_silu(main_ref[0, r0:r0 + length, cols], halo, w_ref, b_ref, shift, cols)

    def expand(v):
        return jnp.concatenate(_split(v, 2), axis=1)

    def prepare(k):
        rows = slice(k * length, (k + 1) * length)
        bc = conv(k, bc_ref, bc_halo, wbc_ref, bbc_ref).astype(BF16)
        bm, cm = bc[:, :BC_COLS], bc[:, BC_COLS:]
        xs = [conv(k, xs_ref, xs_halo, wx_ref, bx_ref, slice(g * GROUP_COLS, (g + 1) * GROUP_COLS))
              for g in range(SSM_GROUPS)]
        dt = dt_ref[0, rows, :]
        cum = _dot_exact_lhs(tri, _split(dt * a, 3))
        cum2 = cum * LOG2E
        src_t = cum2.T - jnp.log2(dt.T)
        total = cum[length - 1:length, :]
        ecum = expand(jnp.exp(cum))
        edtw = expand(dt * jnp.exp(total - cum))
        return bm, cm, xs, cum2, src_t, ecum, edtw

    def scan(k, prepared):
        bm, cm, xs, cum2, src_t, ecum, edtw = prepared
        rows = slice(k * length, (k + 1) * length)
        for g in range(SSM_GROUPS):
            gs = slice(g * SSM_STATE, (g + 1) * SSM_STATE)
            cs = slice(g * GROUP_COLS, (g + 1) * GROUP_COLS)
            ecum_g = _dot(ecum, sel2[:, cs])
            edtw_g = _dot(edtw, sel2[:, cs])
            xs_f = xs[g]
            b_g = bm[:, gs]
            c_g = cm[:, gs]
            xs_g = xs_f.astype(BF16)
            cb = _dot_nt(c_g, b_g)
            st = state_scr[g]
            y_g = _dot(c_g, st.astype(BF16)) * ecum_g
            xw = (xs_f * edtw_g).astype(BF16)
            b_t = b_g.astype(F32).T.astype(BF16)
            state_scr[g] = st * ecum_g[length - 1:length, :] + _dot(b_t, xw)
            y_parts = []
            for pair in range(SSM_HEADS_PER_GROUP // 2):
                ms = []
                for hh in range(2):
                    h = g * SSM_HEADS_PER_GROUP + pair * 2 + hh
                    seg = jnp.where(causal, cum2[:, h:h + 1] - src_t[h:h + 1, :], NEG_BIG)
                    ms.append((cb * jnp.exp2(seg)).astype(BF16))
                x_pair = xs_g[:, pair * LANES:(pair + 1) * LANES]
                zero = jnp.zeros_like(x_pair)
                rhs = jnp.concatenate([jnp.where(lo_half, x_pair, zero),
                                       jnp.where(lo_half, zero, x_pair)], axis=0)
                y_parts.append(_dot(jnp.concatenate(ms, axis=1), rhs))
            y_g = y_g + jnp.concatenate(y_parts, axis=1)
            y_g = y_g + xs_f * dskip_ref[:, cs]
            y_g = y_g * _silu(z_ref[0, rows, cs].astype(F32))
            msq = jnp.mean(y_g * y_g, axis=-1, keepdims=True)
            y_g = y_g * lax.rsqrt(msq + SUBLN_EPS) * ng_ref[:, cs]
            y_ref[0, rows, cs] = y_g.astype(y_ref.dtype)

    prepared = prepare(0)
    for k in range(sub_chunks):
        current = prepared
        if k + 1 < sub_chunks:
            prepared = prepare(k + 1)
        scan(k, current)

    last = sub_chunks * length
    xs_halo[...] = xs_ref[0, last - BF16_SUBLANES:last, :]
    bc_halo[...] = bc_ref[0, last - BF16_SUBLANES:last, :]


def _ssd(proj3, dt3, alog, wx, bx, wbc, bbc, dskip_x, ng, sel):
    b, s, _ = proj3.shape
    length = SSD_L
    rows = SSD_SUB * length
    nc = s // rows
    xs_blk, z_blk = OFF_XS // SSM_D_INNER, OFF_Z // SSM_D_INNER
    assert OFF_C == OFF_B + BC_COLS and OFF_B % (2 * BC_COLS) == 0
    bc_blk = OFF_B // (2 * BC_COLS)

    def main(width, blk):
        return pl.BlockSpec((1, rows, width), lambda i, c: (i, c, blk))

    def const(shape):
        return pl.BlockSpec(shape, lambda i, c: (0,) * len(shape))

    kern = functools.partial(_ssd_kernel, length=length, sub_chunks=SSD_SUB)
    return pl.pallas_call(
        kern,
        grid=(b, nc),
        in_specs=[
            main(SSM_D_INNER, xs_blk), main(2 * BC_COLS, bc_blk), main(SSM_D_INNER, z_blk),
            pl.BlockSpec((1, rows, LANES), lambda i, c: (i, c, 0)),
            const((1, LANES)),
            const((SSM_CONV, BF16_SUBLANES, SSM_D_INNER)), const((1, SSM_D_INNER)),
            const((SSM_CONV, BF16_SUBLANES, 2 * BC_COLS)), const((1, 2 * BC_COLS)),
            const((1, SSM_D_INNER)), const((1, SSM_D_INNER)),
            const((2 * LANES, SSM_D_INNER)),
        ],
        out_specs=pl.BlockSpec((1, rows, SSM_D_INNER), lambda i, c: (i, c, 0)),
        out_shape=jax.ShapeDtypeStruct((b, s, SSM_D_INNER), BF16),
        scratch_shapes=[
            pltpu.VMEM((SSM_GROUPS, SSM_STATE, GROUP_COLS), F32),
            pltpu.VMEM((BF16_SUBLANES, SSM_D_INNER), BF16),
            pltpu.VMEM((BF16_SUBLANES, 2 * BC_COLS), BF16),
        ],
        compiler_params=pltpu.CompilerParams(
            dimension_semantics=("parallel", "arbitrary"), vmem_limit_bytes=VMEM_LIMIT),
        name="ssd",
    )(proj3, proj3, proj3, dt3, alog, wx, bx, wbc, bbc, dskip_x, ng, sel)


def _merge_kernel(x_ref, ya_ref, ys_ref, gate_ref, wpa_ref, wps_ref, wo_ref, g_ref,
                  x1_ref, h2_ref):
    pa = _dot(ya_ref[...], wpa_ref[...])
    ps = _dot(ys_ref[...], wps_ref[...])
    ga = 1.0 / (1.0 + jnp.exp(-gate_ref[:, :D_MODEL].astype(F32)))
    gs = 1.0 / (1.0 + jnp.exp(-gate_ref[:, D_MODEL:].astype(F32)))
    mixed = (ga * pa + gs * ps).astype(BF16)
    x1 = x_ref[...] + _dot(mixed, wo_ref[...])
    x1_ref[...] = x1
    ms = jnp.mean(x1 * x1, axis=-1, keepdims=True)
    h2_ref[...] = (x1 * lax.rsqrt(ms + RMS_EPS) * g_ref[...]).astype(BF16)


def _merge(x2, y_attn, y_ssm, proj, wpa, wps, wo, g):
    m = x2.shape[0]
    tm = MERGE_TM
    gate_blk = OFF_GATE // (2 * D_MODEL)

    def rows(width, blk=0):
        return pl.BlockSpec((tm, width), lambda i: (i, blk))

    def const(shape):
        return pl.BlockSpec(shape, lambda i: (0, 0))

    return pl.pallas_call(
        _merge_kernel,
        grid=(m // tm,),
        in_specs=[
            rows(D_MODEL), rows(V_COLS), rows(SSM_D_INNER),
            rows(2 * D_MODEL, gate_blk),
            const((V_COLS, D_MODEL)), const((SSM_D_INNER, D_MODEL)), const((D_MODEL, D_MODEL)),
            const((1, D_MODEL)),
        ],
        out_specs=[rows(D_MODEL), rows(D_MODEL)],
        out_shape=[jax.ShapeDtypeStruct((m, D_MODEL), F32),
                   jax.ShapeDtypeStruct((m, D_MODEL), BF16)],
        compiler_params=pltpu.CompilerParams(
            dimension_semantics=("parallel",), vmem_limit_bytes=VMEM_LIMIT),
        name="merge",
    )(x2, y_attn, y_ssm, proj, wpa, wps, wo, g)


def _ffn_kernel(x1_ref, h_ref, hh_ref, wup_ref, cw_ref, cb_ref, wd_ref, o_ref, gscr, vscr, act_scr,
                *, tm, tiles_per_seq):
    i = pl.program_id(0)
    first = (i % tiles_per_seq) == 0
    halo = hh_ref[...]
    halo = jnp.where(first, jnp.zeros_like(halo), halo)
    hrows = halo.shape[0]
    h_ext = jnp.concatenate([halo, h_ref[...]], axis=0)
    taps = cw_ref.shape[0]

    slots = gscr.shape[0]

    def up_project(idx):
        start, width = FFN_SPLITS[idx]
        for col0, scr in ((start, gscr), (D_FF + start, vscr)):
            scr[idx % slots, :, 0:width] = _dot(h_ext, wup_ref[:, col0:col0 + width])

    def conv(idx, col0, scr):
        width = FFN_SPLITS[idx][1]
        cols = slice(col0, col0 + width)
        slot = idx % slots
        out = cb_ref[:, cols] + cw_ref[taps - 1:taps, cols] * scr[slot, hrows:hrows + tm, 0:width]
        for k in range(taps - 1):
            off = hrows - (taps - 1) + k
            out = out + cw_ref[k:k + 1, cols] * scr[slot, off:off + tm, 0:width]
        return out

    def gated(idx):
        start = FFN_SPLITS[idx][0]
        return (_silu(conv(idx, start, gscr)) * conv(idx, D_FF + start, vscr)).astype(BF16)

    n_chunks = len(FFN_SPLITS)
    ahead = slots - 1
    for idx in range(min(ahead, n_chunks)):
        up_project(idx)
    for idx in range(n_chunks):
        if idx + ahead < n_chunks:
            up_project(idx + ahead)
        start, width = FFN_SPLITS[idx]
        act_scr[:, start:start + width] = gated(idx)

    o_ref[...] = x1_ref[...] + _dot(act_scr[...], wd_ref[...])


def _ffn(x1, h2, w_up, cw, cb, w_down, seq):
    m = x1.shape[0]
    tm = FFN_TM
    hb = tm // BF16_SUBLANES
    kern = functools.partial(_ffn_kernel, tm=tm, tiles_per_seq=seq // tm)

    def resident(shape):
        return pl.BlockSpec(shape, lambda i: (0, 0), pipeline_mode=pl.Buffered(1))

    return pl.pallas_call(
        kern,
        grid=(m // tm,),
        in_specs=[
            pl.BlockSpec((tm, D_MODEL), lambda i: (i, 0)),
            pl.BlockSpec((tm, D_MODEL), lambda i: (i, 0)),
            pl.BlockSpec((BF16_SUBLANES, D_MODEL), lambda i: (jnp.maximum(i * hb - 1, 0), 0)),
            resident((D_MODEL, 2 * D_FF)),
            resident((FFN_CONV, 2 * D_FF)),
            resident((1, 2 * D_FF)),
            resident((D_FF, D_MODEL)),
        ],
        out_specs=pl.BlockSpec((tm, D_MODEL), lambda i: (i, 0)),
        out_shape=jax.ShapeDtypeStruct((m, D_MODEL), F32),
        scratch_shapes=[
            pltpu.VMEM((FFN_SLOTS, tm + BF16_SUBLANES, FFN_SCR), F32),
            pltpu.VMEM((FFN_SLOTS, tm + BF16_SUBLANES, FFN_SCR), F32),
            pltpu.VMEM((tm, D_FF), BF16),
        ],
        compiler_params=pltpu.CompilerParams(
            dimension_semantics=("parallel",), vmem_limit_bytes=VMEM_LIMIT),
        name="ffn",
    )(x1, h2, h2, w_up, cw, cb, w_down)


def _head_expand(rows, cols, chunk):
    r = lax.broadcasted_iota(jnp.int32, (rows, cols), 0)
    c = lax.broadcasted_iota(jnp.int32, (rows, cols), 1) // chunk
    return (r == c).astype(BF16)


def _layer(x, rel_bias, norm_mix_g, w_in, q_norm_g, k_norm_g, lq1, lk1, lq2, lk2, attn_subln_g,
           conv_ssm_w, conv_ssm_b, dt_bias, a_log, d_skip, ssm_norm_g, w_proj_attn, w_proj_ssm,
           w_out, norm_ffn_g, w_up, conv_ffn_w, conv_ffn_b, w_down):
    b, s, d = x.shape
    m = b * s
    x2 = x.reshape(m, d)

    o_dt = sum(width for name, width in W_SEGMENTS if name != "gate")

    def w_cols(lo, hi):
        parts = []
        if lo < o_dt:
            parts.append(w_in[:, lo:min(hi, o_dt)])
        if hi > o_dt:
            parts.append(w_in[:, max(lo, o_dt) + SSM_HEADS:hi + SSM_HEADS])
        return (parts[0] if len(parts) == 1 else jnp.concatenate(parts, axis=1)).astype(BF16)

    half_cols = PROJ_COLS // IN_HALVES
    w_halves = [w_cols(hv * half_cols, (hv + 1) * half_cols) for hv in range(IN_HALVES)]
    pad_h = LANES - SSM_HEADS
    w_dt = jnp.pad(w_in[:, o_dt:o_dt + SSM_HEADS], ((0, 0), (0, pad_h))).astype(BF16)
    dt_b = jnp.pad(dt_bias.astype(F32), (0, pad_h)).reshape(1, LANES)
    reps = Q_COLS // ATTN_HEAD_DIM
    qk_gain = jnp.stack([
        jnp.tile(q_norm_g.astype(F32) * (ATTN_HEAD_DIM ** -0.5 * LOG2E), reps),
        jnp.tile(k_norm_g.astype(F32), reps)]).reshape(2, 1, Q_COLS)

    proj, dt = _in_proj(x2, norm_mix_g.astype(F32).reshape(1, d), w_halves, w_dt, dt_b, qk_gain)
    proj3 = proj.reshape(b, s, PROJ_COLS)

    vec = lambda v: v.astype(F32).reshape(1, -1)
    bias, far, lam = _bias_build(rel_bias.astype(F32), vec(lq1), vec(lk1), vec(lq2), vec(lk2), ATTN_T)
    y_attn = _diff_attn(lam[0, :1], far[:, 0, 0], proj3, bias,
                        vec(attn_subln_g) * (1.0 - LAM_INIT))

    cw = jnp.broadcast_to(conv_ssm_w.astype(BF16)[:, None, :],
                          (SSM_CONV, BF16_SUBLANES, conv_ssm_w.shape[-1]))
    cbias = conv_ssm_b.astype(F32).reshape(1, -1)
    xs_sl = slice(0, SSM_D_INNER)
    bc_sl = slice(SSM_D_INNER, SSM_D_INNER + 2 * BC_COLS)
    y_ssm = _ssd(
        proj3, dt.reshape(b, s, LANES),
        jnp.pad(a_log.astype(F32), (0, pad_h)).reshape(1, LANES),
        cw[..., xs_sl], cbias[:, xs_sl], cw[..., bc_sl], cbias[:, bc_sl],
        jnp.repeat(d_skip.astype(F32), SSM_HEAD_DIM).reshape(1, SSM_D_INNER),
        vec(ssm_norm_g), jnp.tile(_head_expand(LANES, SSM_D_INNER, SSM_HEAD_DIM), (2, 1)))

    x1, h2 = _merge(x2, y_attn.reshape(m, V_COLS), y_ssm.reshape(m, SSM_D_INNER), proj,
                    w_proj_attn.astype(BF16), w_proj_ssm.astype(BF16), w_out.astype(BF16),
                    vec(norm_ffn_g))

    out = _ffn(x1, h2, w_up.astype(BF16), conv_ffn_w.astype(F32),
               conv_ffn_b.astype(F32).reshape(1, -1), w_down.astype(BF16), s)
    return out.reshape(b, s, d)


def kernel(x, rel_bias, norm_mix_g, w_in, q_norm_g, k_norm_g, lambda_q1, lambda_k1, lambda_q2,
           lambda_k2, attn_subln_g, conv_ssm_w, conv_ssm_b, dt_bias, a_log, d_skip, ssm_norm_g,
           w_proj_attn, w_proj_ssm, w_out, norm_ffn_g, w_up, conv_ffn_w, conv_ffn_b, w_down):
    depth = w_in.shape[0]
    assert depth == 1, "lambda_init and the stacked-parameter layout are specialised to depth 1"
    return _layer(x, rel_bias, norm_mix_g[0], w_in[0], q_norm_g[0], k_norm_g[0], lambda_q1[0],
                  lambda_k1[0], lambda_q2[0], lambda_k2[0], attn_subln_g[0], conv_ssm_w[0],
                  conv_ssm_b[0], dt_bias[0], a_log[0], d_skip[0], ssm_norm_g[0], w_proj_attn[0],
                  w_proj_ssm[0], w_out[0], norm_ffn_g[0], w_up[0], conv_ffn_w[0], conv_ffn_b[0],
                  w_down[0])
```

```python
import functools
import math

import jax
import jax.numpy as jnp
from jax import lax
from jax.experimental import pallas as pl
from jax.experimental.pallas import tpu as pltpu

F32 = jnp.float32
BF16 = jnp.bfloat16

D_MODEL = 1024
ATTN_HEADS = 8
ATTN_HEAD_DIM = 64
ATTN_V_DIM = 2 * ATTN_HEAD_DIM
NUM_BUCKETS = 32
MAX_DISTANCE = 128
SSM_D_INNER = 2048
SSM_HEAD_DIM = 64
SSM_HEADS = 32
SSM_GROUPS = 4
SSM_HEADS_PER_GROUP = SSM_HEADS // SSM_GROUPS
SSM_STATE = 128
SSM_CONV = 4
D_FF = 2816
FFN_CONV = 3
RMS_EPS = 1e-6
SUBLN_EPS = 1e-5
LAM_INIT = 0.8 - 0.6 * math.exp(0.0)

Q_COLS = ATTN_HEADS * 2 * ATTN_HEAD_DIM
V_COLS = ATTN_HEADS * ATTN_V_DIM
BC_COLS = SSM_GROUPS * SSM_STATE
GROUP_COLS = SSM_D_INNER // SSM_GROUPS

LANES = 128
SUBLANES = 8
BF16_SUBLANES = 16
VMEM_LIMIT = 56 * 1024 * 1024

LOG2E = math.log2(math.e)
NEG_BIG = -1e30

W_SEGMENTS = (("q", Q_COLS), ("k", Q_COLS), ("v", V_COLS), ("z", SSM_D_INNER), ("xs", SSM_D_INNER),
              ("b", BC_COLS), ("c", BC_COLS), ("gate", 2 * D_MODEL))
OFF_Z = 0
OFF_Q = OFF_Z + SSM_D_INNER
OFF_K = OFF_Q + Q_COLS
OFF_V = OFF_K + Q_COLS
OFF_B = OFF_V + V_COLS
OFF_C = OFF_B + BC_COLS
OFF_XS = OFF_C + BC_COLS
OFF_GATE = OFF_XS + SSM_D_INNER
PROJ_COLS = OFF_GATE + 2 * D_MODEL
OUT_OFFSETS = {"q": OFF_Q, "k": OFF_K, "v": OFF_V, "z": OFF_Z, "xs": OFF_XS, "b": OFF_B,
               "c": OFF_C, "gate": OFF_GATE}

IN_TM = 512
IN_TN = 512
IN_HALVES = 2
ATTN_T = 128
ATTN_HEADS_PER_STEP = 2
ATTN_LAGS = (2, 5)
SSD_L = 128
SSD_SUB = 4
MERGE_TM = 512
FFN_TM = 512
FFN_SPLITS = ((0, 512), (512, 512), (1024, 512), (1536, 512), (2048, 512), (2560, 256))
FFN_SCR = max(w for _, w in FFN_SPLITS)
FFN_SLOTS = 3


def _split(v, parts):
    out = []
    for _ in range(parts - 1):
        hi = v.astype(BF16)
        out.append(hi)
        v = v - hi.astype(F32)
    out.append(v.astype(BF16))
    return out


def _dot(a, b):
    return jnp.dot(a, b, preferred_element_type=F32)


def _dot_nt(a, b):
    return lax.dot_general(a, b, (((1,), (1,)), ((), ())), preferred_element_type=F32)


def _dot_exact_lhs(sel, parts):
    out = _dot(sel, parts[0])
    for p in parts[1:]:
        out = out + _dot(sel, p)
    return out


def _silu(v):
    h = 0.5 * v
    return h + h * jnp.tanh(h)


def _in_proj_kernel(x0_ref, xn_ref, g_ref, wdt_ref, dtb_ref, qkg_ref, *rest, tile_plan):
    w_refs, (o_ref, dt_ref, h_scr) = rest[:len(tile_plan)], rest[len(tile_plan):]
    i = pl.program_id(0)
    half = pl.program_id(1)
    slot = i % 2

    def normed(x):
        ms = jnp.mean(x * x, axis=-1, keepdims=True)
        return (x * lax.rsqrt(ms + RMS_EPS) * g_ref[...]).astype(BF16)

    @pl.when(jnp.logical_and(i == 0, half == 0))
    def _():
        h_scr[0] = normed(x0_ref[...])

    h = h_scr[slot]

    def tile_cols(t):
        return slice(t * IN_TN, (t + 1) * IN_TN)

    def store_plain(out0, acc):
        o_ref[:, out0:out0 + IN_TN] = acc.astype(BF16)

    def store_head_normed(out0, acc, gain_row, gain_col0):
        lo_half = lax.broadcasted_iota(jnp.int32, (IN_TM, LANES), 1) < ATTN_HEAD_DIM
        for cblk in range(IN_TN // LANES):
            a = acc[:, cblk * LANES:(cblk + 1) * LANES]
            sq = a * a
            s_lo = jnp.sum(jnp.where(lo_half, sq, 0.0), axis=-1, keepdims=True)
            s_hi = jnp.sum(jnp.where(lo_half, 0.0, sq), axis=-1, keepdims=True)
            ssum = jnp.where(lo_half, s_lo, s_hi)
            y = a * lax.rsqrt(ssum * (1.0 / ATTN_HEAD_DIM) + RMS_EPS)
            y = y * qkg_ref[gain_row, :, gain_col0 + cblk * LANES:gain_col0 + (cblk + 1) * LANES]
            o_ref[:, out0 + cblk * LANES:out0 + (cblk + 1) * LANES] = y.astype(BF16)

    for hv, plan in enumerate(tile_plan):
        @pl.when(half == hv)
        def _(plan=plan, hv=hv):
            w_ref = w_refs[hv]
            acc_next = _dot(h, w_ref[:, tile_cols(0)])
            if hv == 0:
                t = _dot(h, wdt_ref[...]) + dtb_ref[...]
                dt_ref[...] = jnp.maximum(t, 0.0) + jnp.log1p(jnp.exp(-jnp.abs(t)))
            if hv == len(tile_plan) - 1:
                h_scr[1 - slot] = normed(xn_ref[...])
            for t, (out0, head_norm) in enumerate(plan):
                acc = acc_next
                if t + 1 < len(plan):
                    acc_next = _dot(h, w_ref[:, tile_cols(t + 1)])
                if head_norm is None:
                    store_plain(out0, acc)
                else:
                    store_head_normed(out0, acc, *head_norm)


def _in_proj_plan():
    half_cols = PROJ_COLS // IN_HALVES
    gain_rows = {"q": 0, "k": 1}
    plan = [[] for _ in range(IN_HALVES)]
    w_off = 0
    for name, width in W_SEGMENTS:
        for col in range(0, width, IN_TN):
            hv, out_col = (w_off + col) // half_cols, OUT_OFFSETS[name] + col
            assert out_col // half_cols == hv and width % IN_TN == 0
            head_norm = (gain_rows[name], col) if name in gain_rows else None
            plan[hv].append((out_col - hv * half_cols, head_norm))
        w_off += width
    return tuple(tuple(p) for p in plan)


def _in_proj(x2, g, w_halves, w_dt, dt_bias, qk_gain):
    m = x2.shape[0]
    n_tiles = m // IN_TM
    half_cols = PROJ_COLS // IN_HALVES
    kern = functools.partial(_in_proj_kernel, tile_plan=_in_proj_plan())
    return pl.pallas_call(
        kern,
        grid=(m // IN_TM, IN_HALVES),
        in_specs=[
            pl.BlockSpec((IN_TM, D_MODEL), lambda i, hv: (0, 0)),
            pl.BlockSpec((IN_TM, D_MODEL), lambda i, hv: (jnp.minimum(i + 1, n_tiles - 1), 0)),
            pl.BlockSpec((1, D_MODEL), lambda i, hv: (0, 0)),
            pl.BlockSpec((D_MODEL, LANES), lambda i, hv: (0, 0)),
            pl.BlockSpec((1, LANES), lambda i, hv: (0, 0)),
            pl.BlockSpec(qk_gain.shape, lambda i, hv: (0, 0, 0)),
        ] + [pl.BlockSpec((D_MODEL, half_cols), lambda i, hv: (0, 0), pipeline_mode=pl.Buffered(1))
             for _ in w_halves],
        out_specs=[
            pl.BlockSpec((IN_TM, half_cols), lambda i, hv: (i, hv)),
            pl.BlockSpec((IN_TM, LANES), lambda i, hv: (i, 0)),
        ],
        out_shape=[
            jax.ShapeDtypeStruct((m, PROJ_COLS), BF16),
            jax.ShapeDtypeStruct((m, LANES), F32),
        ],
        scratch_shapes=[pltpu.VMEM((2, IN_TM, D_MODEL), BF16)],
        compiler_params=pltpu.CompilerParams(
            dimension_semantics=("arbitrary", "arbitrary"), vmem_limit_bytes=VMEM_LIMIT),
        name="in_proj",
    )(x2, x2, g, w_dt, dt_bias, qk_gain, *w_halves)


def _bias_kernel(rel_ref, lq1_ref, lk1_ref, lq2_ref, lk2_ref, bias_ref, far_ref, lam_ref, *, tile):
    h = pl.program_id(0)
    row = lax.broadcasted_iota(jnp.int32, (tile, tile), 0)
    col = lax.broadcasted_iota(jnp.int32, (tile, tile), 1)
    max_exact = NUM_BUCKETS // 2
    for t in range(2):
        d = row - col + t * tile
        n = jnp.maximum(d, 0)
        nf = jnp.maximum(n, 1).astype(F32)
        large = max_exact + (jnp.log(nf / max_exact) / math.log(MAX_DISTANCE / max_exact)
                             * (NUM_BUCKETS - max_exact)).astype(jnp.int32)
        large = jnp.minimum(large, NUM_BUCKETS - 1)
        bucket = jnp.where(n < max_exact, n, large)
        bias = jnp.zeros((tile, tile), F32)
        for bk in range(NUM_BUCKETS):
            bias = jnp.where(bucket == bk, rel_ref[bk, h], bias)
        bias = bias * LOG2E
        if t == 0:
            bias = jnp.where(d >= 0, bias, NEG_BIG)
        bias_ref[0, t] = bias
    far_ref[...] = jnp.full(far_ref.shape, rel_ref[NUM_BUCKETS - 1, h] * LOG2E, F32)
    s1 = jnp.sum(lq1_ref[...] * lk1_ref[...], axis=-1, keepdims=True)
    s2 = jnp.sum(lq2_ref[...] * lk2_ref[...], axis=-1, keepdims=True)
    lam = jnp.exp(s1) - jnp.exp(s2) + LAM_INIT
    lam_ref[...] = jnp.broadcast_to(lam, lam_ref.shape)


def _bias_build(rel_bias, lq1, lk1, lq2, lk2, tile):
    vec = pl.BlockSpec((1, ATTN_HEAD_DIM), lambda h: (0, 0))
    return pl.pallas_call(
        functools.partial(_bias_kernel, tile=tile),
        grid=(ATTN_HEADS,),
        in_specs=[pl.BlockSpec(memory_space=pltpu.SMEM), vec, vec, vec, vec],
        out_specs=[
            pl.BlockSpec((1, 2, tile, tile), lambda h: (h, 0, 0, 0)),
            pl.BlockSpec((1, SUBLANES, LANES), lambda h: (h, 0, 0)),
            pl.BlockSpec((SUBLANES, LANES), lambda h: (0, 0)),
        ],
        out_shape=[
            jax.ShapeDtypeStruct((ATTN_HEADS, 2, tile, tile), F32),
            jax.ShapeDtypeStruct((ATTN_HEADS, SUBLANES, LANES), F32),
            jax.ShapeDtypeStruct((SUBLANES, LANES), F32),
        ],
        compiler_params=pltpu.CompilerParams(dimension_semantics=("arbitrary",)),
        name="bias_build",
    )(rel_bias, lq1, lk1, lq2, lk2)


def _attn_kernel(lam_ref, far_ref, q_ref, k_ref, v_ref, bias_ref, g_ref, o_ref, vext_scr,
                 *, tile, nq, heads):
    h0 = pl.program_id(1) * heads
    lam = lam_ref[0]
    lane = lax.broadcasted_iota(jnp.int32, (tile, LANES), 1)
    lo_half = lane < ATTN_HEAD_DIM
    gain = g_ref[...]

    for j in range(heads):
        vals = v_ref[0, :, j * LANES:(j + 1) * LANES]
        vext_scr[j] = jnp.concatenate([vals, jnp.ones_like(vals)], axis=1)

    def scores(j, qi):
        q = q_ref[0, qi * tile:(qi + 1) * tile, j * LANES:(j + 1) * LANES]
        zero = jnp.zeros_like(q)
        q_both = jnp.concatenate([jnp.where(lo_half, q, zero), jnp.where(lo_half, zero, q)], axis=0)
        return _dot_nt(q_both, k_ref[0, 0:(qi + 1) * tile, j * LANES:(j + 1) * LANES])

    def softmax(j, qi, s):
        far = far_ref[h0 + j]
        nk = (qi + 1) * tile
        n_far = nk - min(qi + 1, 2) * tile
        if qi == 0:
            near_bias = bias_ref[h0 + j, 0]
        else:
            near_bias = jnp.concatenate([bias_ref[h0 + j, 1], bias_ref[h0 + j, 0]], axis=1)
        s_near = s[:, n_far:] + jnp.concatenate([near_bias, near_bias], axis=0)
        m = jnp.max(s_near, axis=-1, keepdims=True)
        if n_far:
            s_far = s[:, :n_far]
            m = jnp.maximum(m, jnp.max(s_far, axis=-1, keepdims=True) + far)
            p = jnp.concatenate([jnp.exp2(s_far - (m - far)), jnp.exp2(s_near - m)], axis=1)
        else:
            p = jnp.exp2(s_near - m)
        return p.astype(BF16)

    def weighted_values(j, qi, p):
        pv = _dot(p, vext_scr[j, 0:(qi + 1) * tile, :])
        return pv[:, :LANES] / pv[:, LANES:]

    units = [(j, qi) for qi in reversed(range(nq)) for j in range(heads)]
    n_units = len(units)
    s_q, p_q = {}, {}
    lag_s, lag_v = ATTN_LAGS
    for step in range(n_units + lag_v):
        if step < n_units:
            s_q[step] = scores(*units[step])
        if 0 <= step - lag_s < n_units:
            p_q[step - lag_s] = softmax(*units[step - lag_s], s_q.pop(step - lag_s))
        if 0 <= step - lag_v < n_units:
            j, qi = units[step - lag_v]
            both = weighted_values(j, qi, p_q.pop(step - lag_v))
            o = both[:tile] - lam * both[tile:]
            ms = jnp.mean(o * o, axis=-1, keepdims=True)
            o = o * lax.rsqrt(ms + SUBLN_EPS) * gain
            o_ref[0, qi * tile:(qi + 1) * tile, j * LANES:(j + 1) * LANES] = o.astype(o_ref.dtype)


def _diff_attn(lam, far, proj3, bias, gain):
    b, s, _ = proj3.shape
    tile = bias.shape[-1]
    nq = s // tile
    heads = ATTN_HEADS_PER_STEP
    width = heads * LANES
    qb, kb, vb = OFF_Q // width, OFF_K // width, OFF_V // width
    kern = functools.partial(_attn_kernel, tile=tile, nq=nq, heads=heads)
    return pl.pallas_call(
        kern,
        grid=(b, ATTN_HEADS // heads),
        in_specs=[
            pl.BlockSpec(memory_space=pltpu.SMEM),
            pl.BlockSpec(memory_space=pltpu.SMEM),
            pl.BlockSpec((1, s, width), lambda i, h: (i, 0, qb + h)),
            pl.BlockSpec((1, s, width), lambda i, h: (i, 0, kb + h)),
            pl.BlockSpec((1, s, width), lambda i, h: (i, 0, vb + h)),
            pl.BlockSpec((ATTN_HEADS, 2, tile, tile), lambda i, h: (0, 0, 0, 0)),
            pl.BlockSpec((1, LANES), lambda i, h: (0, 0)),
        ],
        out_specs=pl.BlockSpec((1, s, width), lambda i, h: (i, 0, h)),
        out_shape=jax.ShapeDtypeStruct((b, s, V_COLS), BF16),
        scratch_shapes=[pltpu.VMEM((heads, s, 2 * LANES), BF16)],
        compiler_params=pltpu.CompilerParams(
            dimension_semantics=("parallel", "parallel"), vmem_limit_bytes=VMEM_LIMIT),
        name="diff_attn",
    )(lam, far, proj3, proj3, proj3, bias, gain)


def _shift_matrices(taps, length, halo_rows):
    def build(rows, offset):
        t = lax.broadcasted_iota(jnp.int32, (rows, taps * rows), 0)
        c = lax.broadcasted_iota(jnp.int32, (rows, taps * rows), 1)
        tap = c // rows
        return ((c - tap * rows) == offset + t - (taps - 1 - tap)).astype(BF16)

    return build(length, 0), build(halo_rows, halo_rows)


def _conv_silu(x, halo, w_ref, b_ref, shifts, cols):
    shift_main, shift_head = shifts
    hrows = halo.shape[0]
    def weighted(v):
        v3 = v.reshape(v.shape[0] // hrows, hrows, v.shape[1])
        return jnp.concatenate([(v3 * w_ref[k, :, cols][None]).reshape(v.shape)
                                for k in range(w_ref.shape[0])], axis=0)

    acc = b_ref[:, cols] + _dot(shift_main, weighted(x))
    head = _dot(shift_head, weighted(halo))
    acc = jnp.concatenate([acc[0:hrows] + head, acc[hrows:]], axis=0)
    return _silu(acc)


def _ssd_kernel(xs_ref, bc_ref, z_ref, dt_ref, alog_ref, wx_ref, bx_ref, wbc_ref, bbc_ref,
                dskip_ref, ng_ref, e_ref, y_ref, state_scr, xs_halo, bc_halo,
                *, length, sub_chunks):
    c = pl.program_id(1)

    @pl.when(c == 0)
    def _():
        state_scr[...] = jnp.zeros_like(state_scr)
        xs_halo[...] = jnp.zeros_like(xs_halo)
        bc_halo[...] = jnp.zeros_like(bc_halo)

    shift = _shift_matrices(SSM_CONV, length, BF16_SUBLANES)
    a = -jnp.exp(alog_ref[...])
    row = lax.broadcasted_iota(jnp.int32, (length, length), 0)
    col = lax.broadcasted_iota(jnp.int32, (length, length), 1)
    causal = row >= col
    tri = causal.astype(BF16)
    sel2 = e_ref[...]
    lane = lax.broadcasted_iota(jnp.int32, (length, LANES), 1)
    lo_half = lane < SSM_HEAD_DIM

    def conv(k, main_ref, halo_scr, w_ref, b_ref, cols=slice(None)):
        r0 = k * length
        if k == 0:
            halo = halo_scr[:, cols]
        else:
            halo = main_ref[0, r0 - BF16_SUBLANES:r0, cols]
        return _conv_silu(main_ref[0, r0:r0 + length, cols], halo, w_ref, b_ref, shift, cols)

    def expand(v):
        return jnp.concatenate(_split(v, 2), axis=1)

    def conv_x(k, g):
        return conv(k, xs_ref, xs_halo, wx_ref, bx_ref, slice(g * GROUP_COLS, (g + 1) * GROUP_COLS))

    def prepare(k):
        rows = slice(k * length, (k + 1) * length)
        bc = conv(k, bc_ref, bc_halo, wbc_ref, bbc_ref).astype(BF16)
        bm, cm = bc[:, :BC_COLS], bc[:, BC_COLS:]
        xs = conv_x(k, 0)
        dt = dt_ref[0, rows, :]
        cum = _dot_exact_lhs(tri, _split(dt * a, 3))
        cum2 = cum * LOG2E
        src_t = cum2.T - jnp.log2(dt.T)
        total = cum[length - 1:length, :]
        ecum = expand(jnp.exp(cum))
        edtw = expand(dt * jnp.exp(total - cum))
        return bm, cm, xs, cum2, src_t, ecum, edtw

    def scan(k, prepared):
        bm, cm, xs, cum2, src_t, ecum, edtw = prepared
        rows = slice(k * length, (k + 1) * length)
        x_next = xs
        for g in range(SSM_GROUPS):
            gs = slice(g * SSM_STATE, (g + 1) * SSM_STATE)
            cs = slice(g * GROUP_COLS, (g + 1) * GROUP_COLS)
            ecum_g = _dot(ecum, sel2[:, cs])
            edtw_g = _dot(edtw, sel2[:, cs])
            xs_f = x_next
            if g + 1 < SSM_GROUPS:
                x_next = conv_x(k, g + 1)
            b_g = bm[:, gs]
            c_g = cm[:, gs]
            xs_g = xs_f.astype(BF16)
            cb = _dot_nt(c_g, b_g)
            st = state_scr[g]
            y_g = _dot(c_g, st.astype(BF16)) * ecum_g
            xw = (xs_f * edtw_g).astype(BF16)
            b_t = b_g.astype(F32).T.astype(BF16)
            state_scr[g] = st * ecum_g[length - 1:length, :] + _dot(b_t, xw)
            y_parts = []
            for pair in range(SSM_HEADS_PER_GROUP // 2):
                ms = []
                for hh in range(2):
                    h = g * SSM_HEADS_PER_GROUP + pair * 2 + hh
                    seg = jnp.where(causal, cum2[:, h:h + 1] - src_t[h:h + 1, :], NEG_BIG)
                    ms.append((cb * jnp.exp2(seg)).astype(BF16))
                x_pair = xs_g[:, pair * LANES:(pair + 1) * LANES]
                zero = jnp.zeros_like(x_pair)
                rhs = jnp.concatenate([jnp.where(lo_half, x_pair, zero),
                                       jnp.where(lo_half, zero, x_pair)], axis=0)
                y_parts.append(_dot(jnp.concatenate(ms, axis=1), rhs))
            y_g = y_g + jnp.concatenate(y_parts, axis=1)
            y_g = y_g + xs_f * dskip_ref[:, cs]
            y_g = y_g * _silu(z_ref[0, rows, cs].astype(F32))
            msq = jnp.mean(y_g * y_g, axis=-1, keepdims=True)
            y_g = y_g * lax.rsqrt(msq + SUBLN_EPS) * ng_ref[:, cs]
            y_ref[0, rows, cs] = y_g.astype(y_ref.dtype)

    prepared = prepare(0)
    for k in range(sub_chunks):
        current = prepared
        if k + 1 < sub_chunks:
            prepared = prepare(k + 1)
        scan(k, current)

    last = sub_chunks * length
    xs_halo[...] = xs_ref[0, last - BF16_SUBLANES:last, :]
    bc_halo[...] = bc_ref[0, last - BF16_SUBLANES:last, :]


def _ssd(proj3, dt3, alog, wx, bx, wbc, bbc, dskip_x, ng, sel):
    b, s, _ = proj3.shape
    length = SSD_L
    rows = SSD_SUB * length
    nc = s // rows
    xs_blk, z_blk = OFF_XS // SSM_D_INNER, OFF_Z // SSM_D_INNER
    assert OFF_C == OFF_B + BC_COLS and OFF_B % (2 * BC_COLS) == 0
    bc_blk = OFF_B // (2 * BC_COLS)

    def main(width, blk):
        return pl.BlockSpec((1, rows, width), lambda i, c: (i, c, blk))

    def const(shape):
        return pl.BlockSpec(shape, lambda i, c: (0,) * len(shape))

    kern = functools.partial(_ssd_kernel, length=length, sub_chunks=SSD_SUB)
    return pl.pallas_call(
        kern,
        grid=(b, nc),
        in_specs=[
            main(SSM_D_INNER, xs_blk), main(2 * BC_COLS, bc_blk), main(SSM_D_INNER, z_blk),
            pl.BlockSpec((1, rows, LANES), lambda i, c: (i, c, 0)),
            const((1, LANES)),
            const((SSM_CONV, BF16_SUBLANES, SSM_D_INNER)), const((1, SSM_D_INNER)),
            const((SSM_CONV, BF16_SUBLANES, 2 * BC_COLS)), const((1, 2 * BC_COLS)),
            const((1, SSM_D_INNER)), const((1, SSM_D_INNER)),
            const((2 * LANES, SSM_D_INNER)),
        ],
        out_specs=pl.BlockSpec((1, rows, SSM_D_INNER), lambda i, c: (i, c, 0)),
        out_shape=jax.ShapeDtypeStruct((b, s, SSM_D_INNER), BF16),
        scratch_shapes=[
            pltpu.VMEM((SSM_GROUPS, SSM_STATE, GROUP_COLS), F32),
            pltpu.VMEM((BF16_SUBLANES, SSM_D_INNER), BF16),
            pltpu.VMEM((BF16_SUBLANES, 2 * BC_COLS), BF16),
        ],
        compiler_params=pltpu.CompilerParams(
            dimension_semantics=("parallel", "arbitrary"), vmem_limit_bytes=VMEM_LIMIT),
        name="ssd",
    )(proj3, proj3, proj3, dt3, alog, wx, bx, wbc, bbc, dskip_x, ng, sel)


def _merge_kernel(x_ref, ya_ref, ys_ref, gate_ref, wpa_ref, wps_ref, wo_ref, g_ref,
                  x1_ref, h2_ref):
    pa = _dot(ya_ref[...], wpa_ref[...])
    ps = _dot(ys_ref[...], wps_ref[...])
    ga = 1.0 / (1.0 + jnp.exp(-gate_ref[:, :D_MODEL].astype(F32)))
    gs = 1.0 / (1.0 + jnp.exp(-gate_ref[:, D_MODEL:].astype(F32)))
    mixed = (ga * pa + gs * ps).astype(BF16)
    x1 = x_ref[...] + _dot(mixed, wo_ref[...])
    x1_ref[...] = x1
    ms = jnp.mean(x1 * x1, axis=-1, keepdims=True)
    h2_ref[...] = (x1 * lax.rsqrt(ms + RMS_EPS) * g_ref[...]).astype(BF16)


def _merge(x2, y_attn, y_ssm, proj, wpa, wps, wo, g):
    m = x2.shape[0]
    tm = MERGE_TM
    gate_blk = OFF_GATE // (2 * D_MODEL)

    def rows(width, blk=0):
        return pl.BlockSpec((tm, width), lambda i: (i, blk))

    def const(shape):
        return pl.BlockSpec(shape, lambda i: (0, 0))

    return pl.pallas_call(
        _merge_kernel,
        grid=(m // tm,),
        in_specs=[
            rows(D_MODEL), rows(V_COLS), rows(SSM_D_INNER),
            rows(2 * D_MODEL, gate_blk),
            const((V_COLS, D_MODEL)), const((SSM_D_INNER, D_MODEL)), const((D_MODEL, D_MODEL)),
            const((1, D_MODEL)),
        ],
        out_specs=[rows(D_MODEL), rows(D_MODEL)],
        out_shape=[jax.ShapeDtypeStruct((m, D_MODEL), F32),
                   jax.ShapeDtypeStruct((m, D_MODEL), BF16)],
        compiler_params=pltpu.CompilerParams(
            dimension_semantics=("parallel",), vmem_limit_bytes=VMEM_LIMIT),
        name="merge",
    )(x2, y_attn, y_ssm, proj, wpa, wps, wo, g)


def _ffn_kernel(x1_ref, h_ref, hh_ref, wup_ref, cw_ref, cb_ref, wd_ref, o_ref, gscr, vscr, act_scr,
                *, tm, tiles_per_seq):
    i = pl.program_id(0)
    first = (i % tiles_per_seq) == 0
    halo = hh_ref[...]
    halo = jnp.where(first, jnp.zeros_like(halo), halo)
    hrows = halo.shape[0]
    h_ext = jnp.concatenate([halo, h_ref[...]], axis=0)
    taps = cw_ref.shape[0]

    slots = gscr.shape[0]

    def up_project(idx):
        start, width = FFN_SPLITS[idx]
        for col0, scr in ((start, gscr), (D_FF + start, vscr)):
            scr[idx % slots, :, 0:width] = _dot(h_ext, wup_ref[:, col0:col0 + width])

    def conv(idx, col0, scr):
        width = FFN_SPLITS[idx][1]
        cols = slice(col0, col0 + width)
        slot = idx % slots
        out = cb_ref[:, cols] + cw_ref[taps - 1:taps, cols] * scr[slot, hrows:hrows + tm, 0:width]
        for k in range(taps - 1):
            off = hrows - (taps - 1) + k
            out = out + cw_ref[k:k + 1, cols] * scr[slot, off:off + tm, 0:width]
        return out

    def gated(idx):
        start = FFN_SPLITS[idx][0]
        return (_silu(conv(idx, start, gscr)) * conv(idx, D_FF + start, vscr)).astype(BF16)

    n_chunks = len(FFN_SPLITS)
    ahead = slots - 1
    for idx in range(min(ahead, n_chunks)):
        up_project(idx)
    for idx in range(n_chunks):
        if idx + ahead < n_chunks:
            up_project(idx + ahead)
        start, width = FFN_SPLITS[idx]
        act_scr[:, start:start + width] = gated(idx)

    o_ref[...] = x1_ref[...] + _dot(act_scr[...], wd_ref[...])


def _ffn(x1, h2, w_up, cw, cb, w_down, seq):
    m = x1.shape[0]
    tm = FFN_TM
    hb = tm // BF16_SUBLANES
    kern = functools.partial(_ffn_kernel, tm=tm, tiles_per_seq=seq // tm)

    def resident(shape):
        return pl.BlockSpec(shape, lambda i: (0, 0), pipeline_mode=pl.Buffered(1))

    return pl.pallas_call(
        kern,
        grid=(m // tm,),
        in_specs=[
            pl.BlockSpec((tm, D_MODEL), lambda i: (i, 0)),
            pl.BlockSpec((tm, D_MODEL), lambda i: (i, 0)),
            pl.BlockSpec((BF16_SUBLANES, D_MODEL), lambda i: (jnp.maximum(i * hb - 1, 0), 0)),
            resident((D_MODEL, 2 * D_FF)),
            resident((FFN_CONV, 2 * D_FF)),
            resident((1, 2 * D_FF)),
            resident((D_FF, D_MODEL)),
        ],
        out_specs=pl.BlockSpec((tm, D_MODEL), lambda i: (i, 0)),
        out_shape=jax.ShapeDtypeStruct((m, D_MODEL), F32),
        scratch_shapes=[
            pltpu.VMEM((FFN_SLOTS, tm + BF16_SUBLANES, FFN_SCR), F32),
            pltpu.VMEM((FFN_SLOTS, tm + BF16_SUBLANES, FFN_SCR), F32),
            pltpu.VMEM((tm, D_FF), BF16),
        ],
        compiler_params=pltpu.CompilerParams(
            dimension_semantics=("parallel",), vmem_limit_bytes=VMEM_LIMIT),
        name="ffn",
    )(x1, h2, h2, w_up, cw, cb, w_down)


def _head_expand(rows, cols, chunk):
    r = lax.broadcasted_iota(jnp.int32, (rows, cols), 0)
    c = lax.broadcasted_iota(jnp.int32, (rows, cols), 1) // chunk
    return (r == c).astype(BF16)


def _layer(x, rel_bias, norm_mix_g, w_in, q_norm_g, k_norm_g, lq1, lk1, lq2, lk2, attn_subln_g,
           conv_ssm_w, conv_ssm_b, dt_bias, a_log, d_skip, ssm_norm_g, w_proj_attn, w_proj_ssm,
           w_out, norm_ffn_g, w_up, conv_ffn_w, conv_ffn_b, w_down):
    b, s, d = x.shape
    m = b * s
    x2 = x.reshape(m, d)

    o_dt = sum(width for name, width in W_SEGMENTS if name != "gate")

    def w_cols(lo, hi):
        parts = []
        if lo < o_dt:
            parts.append(w_in[:, lo:min(hi, o_dt)])
        if hi > o_dt:
            parts.append(w_in[:, max(lo, o_dt) + SSM_HEADS:hi + SSM_HEADS])
        return (parts[0] if len(parts) == 1 else jnp.concatenate(parts, axis=1)).astype(BF16)

    half_cols = PROJ_COLS // IN_HALVES
    w_halves = [w_cols(hv * half_cols, (hv + 1) * half_cols) for hv in range(IN_HALVES)]
    pad_h = LANES - SSM_HEADS
    w_dt = jnp.pad(w_in[:, o_dt:o_dt + SSM_HEADS], ((0, 0), (0, pad_h))).astype(BF16)
    dt_b = jnp.pad(dt_bias.astype(F32), (0, pad_h)).reshape(1, LANES)
    reps = Q_COLS // ATTN_HEAD_DIM
    qk_gain = jnp.stack([
        jnp.tile(q_norm_g.astype(F32) * (ATTN_HEAD_DIM ** -0.5 * LOG2E), reps),
        jnp.tile(k_norm_g.astype(F32), reps)]).reshape(2, 1, Q_COLS)

    proj, dt = _in_proj(x2, norm_mix_g.astype(F32).reshape(1, d), w_halves, w_dt, dt_b, qk_gain)
    proj3 = proj.reshape(b, s, PROJ_COLS)

    vec = lambda v: v.astype(F32).reshape(1, -1)
    bias, far, lam = _bias_build(rel_bias.astype(F32), vec(lq1), vec(lk1), vec(lq2), vec(lk2), ATTN_T)
    y_attn = _diff_attn(lam[0, :1], far[:, 0, 0], proj3, bias,
                        vec(attn_subln_g) * (1.0 - LAM_INIT))

    cw = jnp.broadcast_to(conv_ssm_w.astype(BF16)[:, None, :],
                          (SSM_CONV, BF16_SUBLANES, conv_ssm_w.shape[-1]))
    cbias = conv_ssm_b.astype(F32).reshape(1, -1)
    xs_sl = slice(0, SSM_D_INNER)
    bc_sl = slice(SSM_D_INNER, SSM_D_INNER + 2 * BC_COLS)
    y_ssm = _ssd(
        proj3, dt.reshape(b, s, LANES),
        jnp.pad(a_log.astype(F32), (0, pad_h)).reshape(1, LANES),
        cw[..., xs_sl], cbias[:, xs_sl], cw[..., bc_sl], cbias[:, bc_sl],
        jnp.repeat(d_skip.astype(F32), SSM_HEAD_DIM).reshape(1, SSM_D_INNER),
        vec(ssm_norm_g), jnp.tile(_head_expand(LANES, SSM_D_INNER, SSM_HEAD_DIM), (2, 1)))

    x1, h2 = _merge(x2, y_attn.reshape(m, V_COLS), y_ssm.reshape(m, SSM_D_INNER), proj,
                    w_proj_attn.astype(BF16), w_proj_ssm.astype(BF16), w_out.astype(BF16),
                    vec(norm_ffn_g))

    out = _ffn(x1, h2, w_up.astype(BF16), conv_ffn_w.astype(F32),
               conv_ffn_b.astype(F32).reshape(1, -1), w_down.astype(BF16), s)
    return out.reshape(b, s, d)


def kernel(x, rel_bias, norm_mix_g, w_in, q_norm_g, k_norm_g, lambda_q1, lambda_k1, lambda_q2,
           lambda_k2, attn_subln_g, conv_ssm_w, conv_ssm_b, dt_bias, a_log, d_skip, ssm_norm_g,
           w_proj_attn, w_proj_ssm, w_out, norm_ffn_g, w_up, conv_ffn_w, conv_ffn_b, w_down):
    depth = w_in.shape[0]
    assert depth == 1, "lambda_init and the stacked-parameter layout are specialised to depth 1"
    return _layer(x, rel_bias, norm_mix_g[0], w_in[0], q_norm_g[0], k_norm_g[0], lambda_q1[0],
                  lambda_k1[0], lambda_q2[0], lambda_k2[0], attn_subln_g[0], conv_ssm_w[0],
                  conv_ssm_b[0], dt_bias[0], a_log[0], d_skip[0], ssm_norm_g[0], w_proj_attn[0],
                  w_proj_ssm[0], w_out[0], norm_ffn_g[0], w_up[0], conv_ffn_w[0], conv_ffn_b[0],
                  w_down[0])
```

```python
import functools
import math

import jax
import jax.numpy as jnp
from jax import lax
from jax.experimental import pallas as pl
from jax.experimental.pallas import tpu as pltpu

F32 = jnp.float32
BF16 = jnp.bfloat16

D_MODEL = 1024
ATTN_HEADS = 8
ATTN_HEAD_DIM = 64
ATTN_V_DIM = 2 * ATTN_HEAD_DIM
NUM_BUCKETS = 32
MAX_DISTANCE = 128
SSM_D_INNER = 2048
SSM_HEAD_DIM = 64
SSM_HEADS = 32
SSM_GROUPS = 4
SSM_HEADS_PER_GROUP = SSM_HEADS // SSM_GROUPS
SSM_STATE = 128
SSM_CONV = 4
D_FF = 2816
FFN_CONV = 3
RMS_EPS = 1e-6
SUBLN_EPS = 1e-5
LAM_INIT = 0.8 - 0.6 * math.exp(0.0)

Q_COLS = ATTN_HEADS * 2 * ATTN_HEAD_DIM
V_COLS = ATTN_HEADS * ATTN_V_DIM
BC_COLS = SSM_GROUPS * SSM_STATE
GROUP_COLS = SSM_D_INNER // SSM_GROUPS

LANES = 128
SUBLANES = 8
BF16_SUBLANES = 16
VMEM_LIMIT = 56 * 1024 * 1024

LOG2E = math.log2(math.e)
NEG_BIG = -1e30

W_SEGMENTS = (("q", Q_COLS), ("k", Q_COLS), ("v", V_COLS), ("z", SSM_D_INNER), ("xs", SSM_D_INNER),
              ("b", BC_COLS), ("c", BC_COLS), ("gate", 2 * D_MODEL))
OFF_Z = 0
OFF_Q = OFF_Z + SSM_D_INNER
OFF_K = OFF_Q + Q_COLS
OFF_V = OFF_K + Q_COLS
OFF_B = OFF_V + V_COLS
OFF_C = OFF_B + BC_COLS
OFF_XS = OFF_C + BC_COLS
OFF_GATE = OFF_XS + SSM_D_INNER
PROJ_COLS = OFF_GATE + 2 * D_MODEL
OUT_OFFSETS = {"q": OFF_Q, "k": OFF_K, "v": OFF_V, "z": OFF_Z, "xs": OFF_XS, "b": OFF_B,
               "c": OFF_C, "gate": OFF_GATE}

IN_TM = 512
IN_TN = 512
IN_HALVES = 2
ATTN_T = 128
ATTN_HEADS_PER_STEP = 2
ATTN_LAGS = (2, 5)
SSD_L = 128
SSD_SUB = 4
MERGE_TM = 512
FFN_TM = 512
FFN_SPLITS = ((0, 512), (512, 512), (1024, 512), (1536, 512), (2048, 512), (2560, 256))
FFN_SCR = max(w for _, w in FFN_SPLITS)
FFN_SLOTS = 3


def _split(v, parts):
    out = []
    for _ in range(parts - 1):
        hi = v.astype(BF16)
        out.append(hi)
        v = v - hi.astype(F32)
    out.append(v.astype(BF16))
    return out


def _dot(a, b):
    return jnp.dot(a, b, preferred_element_type=F32)


def _dot_nt(a, b):
    return lax.dot_general(a, b, (((1,), (1,)), ((), ())), preferred_element_type=F32)


def _dot_exact_lhs(sel, parts):
    out = _dot(sel, parts[0])
    for p in parts[1:]:
        out = out + _dot(sel, p)
    return out


def _silu(v):
    h = 0.5 * v
    return h + h * jnp.tanh(h)


def _in_proj_kernel(x0_ref, xn_ref, g_ref, wdt_ref, dtb_ref, qkg_ref, *rest, tile_plan):
    w_refs, (o_ref, dt_ref, h_scr) = rest[:len(tile_plan)], rest[len(tile_plan):]
    i = pl.program_id(0)
    half = pl.program_id(1)
    slot = i % 2

    def normed(x):
        ms = jnp.mean(x * x, axis=-1, keepdims=True)
        return (x * lax.rsqrt(ms + RMS_EPS) * g_ref[...]).astype(BF16)

    @pl.when(jnp.logical_and(i == 0, half == 0))
    def _():
        h_scr[0] = normed(x0_ref[...])

    h = h_scr[slot]

    def tile_cols(t):
        return slice(t * IN_TN, (t + 1) * IN_TN)

    def store_plain(out0, acc):
        o_ref[:, out0:out0 + IN_TN] = acc.astype(BF16)

    def store_head_normed(out0, acc, gain_row, gain_col0):
        lo_half = lax.broadcasted_iota(jnp.int32, (IN_TM, LANES), 1) < ATTN_HEAD_DIM
        for cblk in range(IN_TN // LANES):
            a = acc[:, cblk * LANES:(cblk + 1) * LANES]
            sq = a * a
            s_lo = jnp.sum(jnp.where(lo_half, sq, 0.0), axis=-1, keepdims=True)
            s_hi = jnp.sum(jnp.where(lo_half, 0.0, sq), axis=-1, keepdims=True)
            ssum = jnp.where(lo_half, s_lo, s_hi)
            y = a * lax.rsqrt(ssum * (1.0 / ATTN_HEAD_DIM) + RMS_EPS)
            y = y * qkg_ref[gain_row, :, gain_col0 + cblk * LANES:gain_col0 + (cblk + 1) * LANES]
            o_ref[:, out0 + cblk * LANES:out0 + (cblk + 1) * LANES] = y.astype(BF16)

    for hv, plan in enumerate(tile_plan):
        @pl.when(half == hv)
        def _(plan=plan, hv=hv):
            w_ref = w_refs[hv]
            acc_next = _dot(h, w_ref[:, tile_cols(0)])
            if hv == 0:
                t = _dot(h, wdt_ref[...]) + dtb_ref[...]
                dt_ref[...] = jnp.maximum(t, 0.0) + jnp.log1p(jnp.exp(-jnp.abs(t)))
            if hv == len(tile_plan) - 1:
                h_scr[1 - slot] = normed(xn_ref[...])
            for t, (out0, head_norm) in enumerate(plan):
                acc = acc_next
                if t + 1 < len(plan):
                    acc_next = _dot(h, w_ref[:, tile_cols(t + 1)])
                if head_norm is None:
                    store_plain(out0, acc)
                else:
                    store_head_normed(out0, acc, *head_norm)


def _in_proj_plan():
    half_cols = PROJ_COLS // IN_HALVES
    gain_rows = {"q": 0, "k": 1}
    plan = [[] for _ in range(IN_HALVES)]
    w_off = 0
    for name, width in W_SEGMENTS:
        for col in range(0, width, IN_TN):
            hv, out_col = (w_off + col) // half_cols, OUT_OFFSETS[name] + col
            assert out_col // half_cols == hv and width % IN_TN == 0
            head_norm = (gain_rows[name], col) if name in gain_rows else None
            plan[hv].append((out_col - hv * half_cols, head_norm))
        w_off += width
    return tuple(tuple(p) for p in plan)


def _in_proj(x2, g, w_halves, w_dt, dt_bias, qk_gain):
    m = x2.shape[0]
    n_tiles = m // IN_TM
    half_cols = PROJ_COLS // IN_HALVES
    kern = functools.partial(_in_proj_kernel, tile_plan=_in_proj_plan())
    return pl.pallas_call(
        kern,
        grid=(m // IN_TM, IN_HALVES),
        in_specs=[
            pl.BlockSpec((IN_TM, D_MODEL), lambda i, hv: (0, 0)),
            pl.BlockSpec((IN_TM, D_MODEL), lambda i, hv: (jnp.minimum(i + 1, n_tiles - 1), 0)),
            pl.BlockSpec((1, D_MODEL), lambda i, hv: (0, 0)),
            pl.BlockSpec((D_MODEL, LANES), lambda i, hv: (0, 0)),
            pl.BlockSpec((1, LANES), lambda i, hv: (0, 0)),
            pl.BlockSpec(qk_gain.shape, lambda i, hv: (0, 0, 0)),
        ] + [pl.BlockSpec((D_MODEL, half_cols), lambda i, hv: (0, 0), pipeline_mode=pl.Buffered(1))
             for _ in w_halves],
        out_specs=[
            pl.BlockSpec((IN_TM, half_cols), lambda i, hv: (i, hv)),
            pl.BlockSpec((IN_TM, LANES), lambda i, hv: (i, 0)),
        ],
        out_shape=[
            jax.ShapeDtypeStruct((m, PROJ_COLS), BF16),
            jax.ShapeDtypeStruct((m, LANES), F32),
        ],
        scratch_shapes=[pltpu.VMEM((2, IN_TM, D_MODEL), BF16)],
        compiler_params=pltpu.CompilerParams(
            dimension_semantics=("arbitrary", "arbitrary"), vmem_limit_bytes=VMEM_LIMIT),
        name="in_proj",
    )(x2, x2, g, w_dt, dt_bias, qk_gain, *w_halves)


def _bias_kernel(rel_ref, lq1_ref, lk1_ref, lq2_ref, lk2_ref, bias_ref, far_ref, lam_ref, *, tile):
    h = pl.program_id(0)
    row = lax.broadcasted_iota(jnp.int32, (tile, tile), 0)
    col = lax.broadcasted_iota(jnp.int32, (tile, tile), 1)
    max_exact = NUM_BUCKETS // 2
    for t in range(2):
        d = row - col + t * tile
        n = jnp.maximum(d, 0)
        nf = jnp.maximum(n, 1).astype(F32)
        large = max_exact + (jnp.log(nf / max_exact) / math.log(MAX_DISTANCE / max_exact)
                             * (NUM_BUCKETS - max_exact)).astype(jnp.int32)
        large = jnp.minimum(large, NUM_BUCKETS - 1)
        bucket = jnp.where(n < max_exact, n, large)
        bias = jnp.zeros((tile, tile), F32)
        for bk in range(NUM_BUCKETS):
            bias = jnp.where(bucket == bk, rel_ref[bk, h], bias)
        bias = bias * LOG2E
        if t == 0:
            bias = jnp.where(d >= 0, bias, NEG_BIG)
        bias_ref[0, t] = bias
    far_ref[...] = jnp.full(far_ref.shape, rel_ref[NUM_BUCKETS - 1, h] * LOG2E, F32)
    s1 = jnp.sum(lq1_ref[...] * lk1_ref[...], axis=-1, keepdims=True)
    s2 = jnp.sum(lq2_ref[...] * lk2_ref[...], axis=-1, keepdims=True)
    lam = jnp.exp(s1) - jnp.exp(s2) + LAM_INIT
    lam_ref[...] = jnp.broadcast_to(lam, lam_ref.shape)


def _bias_build(rel_bias, lq1, lk1, lq2, lk2, tile):
    vec = pl.BlockSpec((1, ATTN_HEAD_DIM), lambda h: (0, 0))
    return pl.pallas_call(
        functools.partial(_bias_kernel, tile=tile),
        grid=(ATTN_HEADS,),
        in_specs=[pl.BlockSpec(memory_space=pltpu.SMEM), vec, vec, vec, vec],
        out_specs=[
            pl.BlockSpec((1, 2, tile, tile), lambda h: (h, 0, 0, 0)),
            pl.BlockSpec((1, SUBLANES, LANES), lambda h: (h, 0, 0)),
            pl.BlockSpec((SUBLANES, LANES), lambda h: (0, 0)),
        ],
        out_shape=[
            jax.ShapeDtypeStruct((ATTN_HEADS, 2, tile, tile), F32),
            jax.ShapeDtypeStruct((ATTN_HEADS, SUBLANES, LANES), F32),
            jax.ShapeDtypeStruct((SUBLANES, LANES), F32),
        ],
        compiler_params=pltpu.CompilerParams(dimension_semantics=("arbitrary",)),
        name="bias_build",
    )(rel_bias, lq1, lk1, lq2, lk2)


def _attn_kernel(lam_ref, far_ref, q_ref, k_ref, v_ref, bias_ref, g_ref, o_ref, vext_scr,
                 *, tile, nq, heads):
    h0 = pl.program_id(1) * heads
    lam = lam_ref[0]
    lane = lax.broadcasted_iota(jnp.int32, (tile, LANES), 1)
    lo_half = lane < ATTN_HEAD_DIM
    gain = g_ref[...]

    for j in range(heads):
        vals = v_ref[0, :, j * LANES:(j + 1) * LANES]
        vext_scr[j] = jnp.concatenate([vals, jnp.ones_like(vals)], axis=1)

    def scores(j, qi):
        q = q_ref[0, qi * tile:(qi + 1) * tile, j * LANES:(j + 1) * LANES]
        zero = jnp.zeros_like(q)
        q_both = jnp.concatenate([jnp.where(lo_half, q, zero), jnp.where(lo_half, zero, q)], axis=0)
        return _dot_nt(q_both, k_ref[0, 0:(qi + 1) * tile, j * LANES:(j + 1) * LANES])

    def softmax(j, qi, s):
        far = far_ref[h0 + j]
        nk = (qi + 1) * tile
        n_far = nk - min(qi + 1, 2) * tile
        if qi == 0:
            near_bias = bias_ref[h0 + j, 0]
        else:
            near_bias = jnp.concatenate([bias_ref[h0 + j, 1], bias_ref[h0 + j, 0]], axis=1)
        s_near = s[:, n_far:] + jnp.concatenate([near_bias, near_bias], axis=0)
        m = jnp.max(s_near, axis=-1, keepdims=True)
        if n_far:
            s_far = s[:, :n_far]
            m = jnp.maximum(m, jnp.max(s_far, axis=-1, keepdims=True) + far)
            p = jnp.concatenate([jnp.exp2(s_far - (m - far)), jnp.exp2(s_near - m)], axis=1)
        else:
            p = jnp.exp2(s_near - m)
        return p.astype(BF16)

    def weighted_values(j, qi, p):
        pv = _dot(p, vext_scr[j, 0:(qi + 1) * tile, :])
        return pv[:, :LANES] / pv[:, LANES:]

    units = [(j, qi) for qi in reversed(range(nq)) for j in range(heads)]
    n_units = len(units)
    s_q, p_q = {}, {}
    lag_s, lag_v = ATTN_LAGS
    for step in range(n_units + lag_v):
        if step < n_units:
            s_q[step] = scores(*units[step])
        if 0 <= step - lag_s < n_units:
            p_q[step - lag_s] = softmax(*units[step - lag_s], s_q.pop(step - lag_s))
        if 0 <= step - lag_v < n_units:
            j, qi = units[step - lag_v]
            both = weighted_values(j, qi, p_q.pop(step - lag_v))
            o = both[:tile] - lam * both[tile:]
            ms = jnp.mean(o * o, axis=-1, keepdims=True)
            o = o * lax.rsqrt(ms + SUBLN_EPS) * gain
            o_ref[0, qi * tile:(qi + 1) * tile, j * LANES:(j + 1) * LANES] = o.astype(o_ref.dtype)


def _diff_attn(lam, far, proj3, bias, gain):
    b, s, _ = proj3.shape
    tile = bias.shape[-1]
    nq = s // tile
    heads = ATTN_HEADS_PER_STEP
    width = heads * LANES
    qb, kb, vb = OFF_Q // width, OFF_K // width, OFF_V // width
    kern = functools.partial(_attn_kernel, tile=tile, nq=nq, heads=heads)
    return pl.pallas_call(
        kern,
        grid=(b, ATTN_HEADS // heads),
        in_specs=[
            pl.BlockSpec(memory_space=pltpu.SMEM),
            pl.BlockSpec(memory_space=pltpu.SMEM),
            pl.BlockSpec((1, s, width), lambda i, h: (i, 0, qb + h)),
            pl.BlockSpec((1, s, width), lambda i, h: (i, 0, kb + h)),
            pl.BlockSpec((1, s, width), lambda i, h: (i, 0, vb + h)),
            pl.BlockSpec((ATTN_HEADS, 2, tile, tile), lambda i, h: (0, 0, 0, 0)),
            pl.BlockSpec((1, LANES), lambda i, h: (0, 0)),
        ],
        out_specs=pl.BlockSpec((1, s, width), lambda i, h: (i, 0, h)),
        out_shape=jax.ShapeDtypeStruct((b, s, V_COLS), BF16),
        scratch_shapes=[pltpu.VMEM((heads, s, 2 * LANES), BF16)],
        compiler_params=pltpu.CompilerParams(
            dimension_semantics=("parallel", "parallel"), vmem_limit_bytes=VMEM_LIMIT),
        name="diff_attn",
    )(lam, far, proj3, proj3, proj3, bias, gain)


def _shift_matrices(taps, length, halo_rows):
    def build(rows, offset):
        t = lax.broadcasted_iota(jnp.int32, (rows, taps * rows), 0)
        c = lax.broadcasted_iota(jnp.int32, (rows, taps * rows), 1)
        tap = c // rows
        return ((c - tap * rows) == offset + t - (taps - 1 - tap)).astype(BF16)

    return build(length, 0), build(halo_rows, halo_rows)


def _conv_silu(x, halo, w_ref, b_ref, shifts, cols):
    shift_main, shift_head = shifts
    hrows = halo.shape[0]
    def weighted(v):
        v3 = v.reshape(v.shape[0] // hrows, hrows, v.shape[1])
        return jnp.concatenate([(v3 * w_ref[k, :, cols][None]).reshape(v.shape)
                                for k in range(w_ref.shape[0])], axis=0)

    acc = b_ref[:, cols] + _dot(shift_main, weighted(x))
    head = _dot(shift_head, weighted(halo))
    acc = jnp.concatenate([acc[0:hrows] + head, acc[hrows:]], axis=0)
    return _silu(acc)


def _ssd_kernel(xs_ref, bc_ref, z_ref, dt_ref, alog_ref, wx_ref, bx_ref, wbc_ref, bbc_ref,
                dskip_ref, ng_ref, e_ref, y_ref, state_scr, xs_halo, bc_halo,
                *, length, sub_chunks):
    c = pl.program_id(1)

    @pl.when(c == 0)
    def _():
        state_scr[...] = jnp.zeros_like(state_scr)
        xs_halo[...] = jnp.zeros_like(xs_halo)
        bc_halo[...] = jnp.zeros_like(bc_halo)

    shift = _shift_matrices(SSM_CONV, length, BF16_SUBLANES)
    a = -jnp.exp(alog_ref[...])
    row = lax.broadcasted_iota(jnp.int32, (length, length), 0)
    col = lax.broadcasted_iota(jnp.int32, (length, length), 1)
    causal = row >= col
    tri = causal.astype(BF16)
    sel2 = e_ref[...]
    lane = lax.broadcasted_iota(jnp.int32, (length, LANES), 1)
    lo_half = lane < SSM_HEAD_DIM

    def conv(k, main_ref, halo_scr, w_ref, b_ref, cols=slice(None)):
        r0 = k * length
        if k == 0:
            halo = halo_scr[:, cols]
        else:
            halo = main_ref[0, r0 - BF16_SUBLANES:r0, cols]
        return _conv_silu(main_ref[0, r0:r0 + length, cols], halo, w_ref, b_ref, shift, cols)

    def expand(v):
        return jnp.concatenate(_split(v, 2), axis=1)

    def conv_x(k, g):
        return conv(k, xs_ref, xs_halo, wx_ref, bx_ref, slice(g * GROUP_COLS, (g + 1) * GROUP_COLS))

    def prepare(k):
        rows = slice(k * length, (k + 1) * length)
        bc = conv(k, bc_ref, bc_halo, wbc_ref, bbc_ref).astype(BF16)
        bm, cm = bc[:, :BC_COLS], bc[:, BC_COLS:]
        xs = conv_x(k, 0)
        dt = dt_ref[0, rows, :]
        cum = _dot_exact_lhs(tri, _split(dt * a, 3))
        cum2 = cum * LOG2E
        src_t = cum2.T - jnp.log2(dt.T)
        total = cum[length - 1:length, :]
        ecum = expand(jnp.exp(cum))
        edtw = expand(dt * jnp.exp(total - cum))
        return bm, cm, xs, cum2, src_t, ecum, edtw

    def scan(k, prepared):
        bm, cm, xs, cum2, src_t, ecum, edtw = prepared
        rows = slice(k * length, (k + 1) * length)
        x_next = xs
        for g in range(SSM_GROUPS):
            gs = slice(g * SSM_STATE, (g + 1) * SSM_STATE)
            cs = slice(g * GROUP_COLS, (g + 1) * GROUP_COLS)
            ecum_g = _dot(ecum, sel2[:, cs])
            edtw_g = _dot(edtw, sel2[:, cs])
            xs_f = x_next
            if g + 1 < SSM_GROUPS:
                x_next = conv_x(k, g + 1)
            b_g = bm[:, gs]
            c_g = cm[:, gs]
            xs_g = xs_f.astype(BF16)
            cb = _dot_nt(c_g, b_g)
            st = state_scr[g]
            y_g = _dot(c_g, st.astype(BF16)) * ecum_g
            xw = (xs_f * edtw_g).astype(BF16)
            b_t = b_g.astype(F32).T.astype(BF16)
            state_scr[g] = st * ecum_g[length - 1:length, :] + _dot(b_t, xw)
            y_parts = []
            for pair in range(SSM_HEADS_PER_GROUP // 2):
                ms = []
                for hh in range(2):
                    h = g * SSM_HEADS_PER_GROUP + pair * 2 + hh
                    seg = jnp.where(causal, cum2[:, h:h + 1] - src_t[h:h + 1, :], NEG_BIG)
                    ms.append((cb * jnp.exp2(seg)).astype(BF16))
                x_pair = xs_g[:, pair * LANES:(pair + 1) * LANES]
                zero = jnp.zeros_like(x_pair)
                rhs = jnp.concatenate([jnp.where(lo_half, x_pair, zero),
                                       jnp.where(lo_half, zero, x_pair)], axis=0)
                y_parts.append(_dot(jnp.concatenate(ms, axis=1), rhs))
            y_g = y_g + jnp.concatenate(y_parts, axis=1)
            y_g = y_g + xs_f * dskip_ref[:, cs]
            y_g = y_g * _silu(z_ref[0, rows, cs].astype(F32))
            msq = jnp.mean(y_g * y_g, axis=-1, keepdims=True)
            y_g = y_g * lax.rsqrt(msq + SUBLN_EPS) * ng_ref[:, cs]
            y_ref[0, rows, cs] = y_g.astype(y_ref.dtype)

    for k in range(sub_chunks):
        scan(k, prepare(k))

    last = sub_chunks * length
    xs_halo[...] = xs_ref[0, last - BF16_SUBLANES:last, :]
    bc_halo[...] = bc_ref[0, last - BF16_SUBLANES:last, :]


def _ssd(proj3, dt3, alog, wx, bx, wbc, bbc, dskip_x, ng, sel):
    b, s, _ = proj3.shape
    length = SSD_L
    rows = SSD_SUB * length
    nc = s // rows
    xs_blk, z_blk = OFF_XS // SSM_D_INNER, OFF_Z // SSM_D_INNER
    assert OFF_C == OFF_B + BC_COLS and OFF_B % (2 * BC_COLS) == 0
    bc_blk = OFF_B // (2 * BC_COLS)

    def main(width, blk):
        return pl.BlockSpec((1, rows, width), lambda i, c: (i, c, blk))

    def const(shape):
        return pl.BlockSpec(shape, lambda i, c: (0,) * len(shape))

    kern = functools.partial(_ssd_kernel, length=length, sub_chunks=SSD_SUB)
    return pl.pallas_call(
        kern,
        grid=(b, nc),
        in_specs=[
            main(SSM_D_INNER, xs_blk), main(2 * BC_COLS, bc_blk), main(SSM_D_INNER, z_blk),
            pl.BlockSpec((1, rows, LANES), lambda i, c: (i, c, 0)),
            const((1, LANES)),
            const((SSM_CONV, BF16_SUBLANES, SSM_D_INNER)), const((1, SSM_D_INNER)),
            const((SSM_CONV, BF16_SUBLANES, 2 * BC_COLS)), const((1, 2 * BC_COLS)),
            const((1, SSM_D_INNER)), const((1, SSM_D_INNER)),
            const((2 * LANES, SSM_D_INNER)),
        ],
        out_specs=pl.BlockSpec((1, rows, SSM_D_INNER), lambda i, c: (i, c, 0)),
        out_shape=jax.ShapeDtypeStruct((b, s, SSM_D_INNER), BF16),
        scratch_shapes=[
            pltpu.VMEM((SSM_GROUPS, SSM_STATE, GROUP_COLS), F32),
            pltpu.VMEM((BF16_SUBLANES, SSM_D_INNER), BF16),
            pltpu.VMEM((BF16_SUBLANES, 2 * BC_COLS), BF16),
        ],
        compiler_params=pltpu.CompilerParams(
            dimension_semantics=("parallel", "arbitrary"), vmem_limit_bytes=VMEM_LIMIT),
        name="ssd",
    )(proj3, proj3, proj3, dt3, alog, wx, bx, wbc, bbc, dskip_x, ng, sel)


def _merge_kernel(x_ref, ya_ref, ys_ref, gate_ref, wpa_ref, wps_ref, wo_ref, g_ref,
                  x1_ref, h2_ref):
    pa = _dot(ya_ref[...], wpa_ref[...])
    ps = _dot(ys_ref[...], wps_ref[...])
    ga = 1.0 / (1.0 + jnp.exp(-gate_ref[:, :D_MODEL].astype(F32)))
    gs = 1.0 / (1.0 + jnp.exp(-gate_ref[:, D_MODEL:].astype(F32)))
    mixed = (ga * pa + gs * ps).astype(BF16)
    x1 = x_ref[...] + _dot(mixed, wo_ref[...])
    x1_ref[...] = x1
    ms = jnp.mean(x1 * x1, axis=-1, keepdims=True)
    h2_ref[...] = (x1 * lax.rsqrt(ms + RMS_EPS) * g_ref[...]).astype(BF16)


def _merge(x2, y_attn, y_ssm, proj, wpa, wps, wo, g):
    m = x2.shape[0]
    tm = MERGE_TM
    gate_blk = OFF_GATE // (2 * D_MODEL)

    def rows(width, blk=0):
        return pl.BlockSpec((tm, width), lambda i: (i, blk))

    def const(shape):
        return pl.BlockSpec(shape, lambda i: (0, 0))

    return pl.pallas_call(
        _merge_kernel,
        grid=(m // tm,),
        in_specs=[
            rows(D_MODEL), rows(V_COLS), rows(SSM_D_INNER),
            rows(2 * D_MODEL, gate_blk),
            const((V_COLS, D_MODEL)), const((SSM_D_INNER, D_MODEL)), const((D_MODEL, D_MODEL)),
            const((1, D_MODEL)),
        ],
        out_specs=[rows(D_MODEL), rows(D_MODEL)],
        out_shape=[jax.ShapeDtypeStruct((m, D_MODEL), F32),
                   jax.ShapeDtypeStruct((m, D_MODEL), BF16)],
        compiler_params=pltpu.CompilerParams(
            dimension_semantics=("parallel",), vmem_limit_bytes=VMEM_LIMIT),
        name="merge",
    )(x2, y_attn, y_ssm, proj, wpa, wps, wo, g)


def _ffn_kernel(x1_ref, h_ref, hh_ref, wup_ref, cw_ref, cb_ref, wd_ref, o_ref, gscr, vscr, act_scr,
                *, tm, tiles_per_seq):
    i = pl.program_id(0)
    first = (i % tiles_per_seq) == 0
    halo = hh_ref[...]
    halo = jnp.where(first, jnp.zeros_like(halo), halo)
    hrows = halo.shape[0]
    h_ext = jnp.concatenate([halo, h_ref[...]], axis=0)
    taps = cw_ref.shape[0]

    slots = gscr.shape[0]

    def up_project(idx):
        start, width = FFN_SPLITS[idx]
        for col0, scr in ((start, gscr), (D_FF + start, vscr)):
            scr[idx % slots, :, 0:width] = _dot(h_ext, wup_ref[:, col0:col0 + width])

    def conv(idx, col0, scr):
        width = FFN_SPLITS[idx][1]
        cols = slice(col0, col0 + width)
        slot = idx % slots
        out = cb_ref[:, cols] + cw_ref[taps - 1:taps, cols] * scr[slot, hrows:hrows + tm, 0:width]
        for k in range(taps - 1):
            off = hrows - (taps - 1) + k
            out = out + cw_ref[k:k + 1, cols] * scr[slot, off:off + tm, 0:width]
        return out

    def gated(idx):
        start = FFN_SPLITS[idx][0]
        return (_silu(conv(idx, start, gscr)) * conv(idx, D_FF + start, vscr)).astype(BF16)

    n_chunks = len(FFN_SPLITS)
    ahead = slots - 1
    for idx in range(min(ahead, n_chunks)):
        up_project(idx)
    for idx in range(n_chunks):
        if idx + ahead < n_chunks:
            up_project(idx + ahead)
        start, width = FFN_SPLITS[idx]
        act_scr[:, start:start + width] = gated(idx)

    o_ref[...] = x1_ref[...] + _dot(act_scr[...], wd_ref[...])


def _ffn(x1, h2, w_up, cw, cb, w_down, seq):
    m = x1.shape[0]
    tm = FFN_TM
    hb = tm // BF16_SUBLANES
    kern = functools.partial(_ffn_kernel, tm=tm, tiles_per_seq=seq // tm)

    def resident(shape):
        return pl.BlockSpec(shape, lambda i: (0, 0), pipeline_mode=pl.Buffered(1))

    return pl.pallas_call(
        kern,
        grid=(m // tm,),
        in_specs=[
            pl.BlockSpec((tm, D_MODEL), lambda i: (i, 0)),
            pl.BlockSpec((tm, D_MODEL), lambda i: (i, 0)),
            pl.BlockSpec((BF16_SUBLANES, D_MODEL), lambda i: (jnp.maximum(i * hb - 1, 0), 0)),
            resident((D_MODEL, 2 * D_FF)),
            resident((FFN_CONV, 2 * D_FF)),
            resident((1, 2 * D_FF)),
            resident((D_FF, D_MODEL)),
        ],
        out_specs=pl.BlockSpec((tm, D_MODEL), lambda i: (i, 0)),
        out_shape=jax.ShapeDtypeStruct((m, D_MODEL), F32),
        scratch_shapes=[
            pltpu.VMEM((FFN_SLOTS, tm + BF16_SUBLANES, FFN_SCR), F32),
            pltpu.VMEM((FFN_SLOTS, tm + BF16_SUBLANES, FFN_SCR), F32),
            pltpu.VMEM((tm, D_FF), BF16),
        ],
        compiler_params=pltpu.CompilerParams(
            dimension_semantics=("parallel",), vmem_limit_bytes=VMEM_LIMIT),
        name="ffn",
    )(x1, h2, h2, w_up, cw, cb, w_down)


def _head_expand(rows, cols, chunk):
    r = lax.broadcasted_iota(jnp.int32, (rows, cols), 0)
    c = lax.broadcasted_iota(jnp.int32, (rows, cols), 1) // chunk
    return (r == c).astype(BF16)


def _layer(x, rel_bias, norm_mix_g, w_in, q_norm_g, k_norm_g, lq1, lk1, lq2, lk2, attn_subln_g,
           conv_ssm_w, conv_ssm_b, dt_bias, a_log, d_skip, ssm_norm_g, w_proj_attn, w_proj_ssm,
           w_out, norm_ffn_g, w_up, conv_ffn_w, conv_ffn_b, w_down):
    b, s, d = x.shape
    m = b * s
    x2 = x.reshape(m, d)

    o_dt = sum(width for name, width in W_SEGMENTS if name != "gate")

    def w_cols(lo, hi):
        parts = []
        if lo < o_dt:
            parts.append(w_in[:, lo:min(hi, o_dt)])
        if hi > o_dt:
            parts.append(w_in[:, max(lo, o_dt) + SSM_HEADS:hi + SSM_HEADS])
        return (parts[0] if len(parts) == 1 else jnp.concatenate(parts, axis=1)).astype(BF16)

    half_cols = PROJ_COLS // IN_HALVES
    w_halves = [w_cols(hv * half_cols, (hv + 1) * half_cols) for hv in range(IN_HALVES)]
    pad_h = LANES - SSM_HEADS
    w_dt = jnp.pad(w_in[:, o_dt:o_dt + SSM_HEADS], ((0, 0), (0, pad_h))).astype(BF16)
    dt_b = jnp.pad(dt_bias.astype(F32), (0, pad_h)).reshape(1, LANES)
    reps = Q_COLS // ATTN_HEAD_DIM
    qk_gain = jnp.stack([
        jnp.tile(q_norm_g.astype(F32) * (ATTN_HEAD_DIM ** -0.5 * LOG2E), reps),
        jnp.tile(k_norm_g.astype(F32), reps)]).reshape(2, 1, Q_COLS)

    proj, dt = _in_proj(x2, norm_mix_g.astype(F32).reshape(1, d), w_halves, w_dt, dt_b, qk_gain)
    proj3 = proj.reshape(b, s, PROJ_COLS)

    vec = lambda v: v.astype(F32).reshape(1, -1)
    bias, far, lam = _bias_build(rel_bias.astype(F32), vec(lq1), vec(lk1), vec(lq2), vec(lk2), ATTN_T)
    y_attn = _diff_attn(lam[0, :1], far[:, 0, 0], proj3, bias,
                        vec(attn_subln_g) * (1.0 - LAM_INIT))

    cw = jnp.broadcast_to(conv_ssm_w.astype(BF16)[:, None, :],
                          (SSM_CONV, BF16_SUBLANES, conv_ssm_w.shape[-1]))
    cbias = conv_ssm_b.astype(F32).reshape(1, -1)
    xs_sl = slice(0, SSM_D_INNER)
    bc_sl = slice(SSM_D_INNER, SSM_D_INNER + 2 * BC_COLS)
    y_ssm = _ssd(
        proj3, dt.reshape(b, s, LANES),
        jnp.pad(a_log.astype(F32), (0, pad_h)).reshape(1, LANES),
        cw[..., xs_sl], cbias[:, xs_sl], cw[..., bc_sl], cbias[:, bc_sl],
        jnp.repeat(d_skip.astype(F32), SSM_HEAD_DIM).reshape(1, SSM_D_INNER),
        vec(ssm_norm_g), jnp.tile(_head_expand(LANES, SSM_D_INNER, SSM_HEAD_DIM), (2, 1)))

    x1, h2 = _merge(x2, y_attn.reshape(m, V_COLS), y_ssm.reshape(m, SSM_D_INNER), proj,
                    w_proj_attn.astype(BF16), w_proj_ssm.astype(BF16), w_out.astype(BF16),
                    vec(norm_ffn_g))

    out = _ffn(x1, h2, w_up.astype(BF16), conv_ffn_w.astype(F32),
               conv_ffn_b.astype(F32).reshape(1, -1), w_down.astype(BF16), s)
    return out.reshape(b, s, d)


def kernel(x, rel_bias, norm_mix_g, w_in, q_norm_g, k_norm_g, lambda_q1, lambda_k1, lambda_q2,
           lambda_k2, attn_subln_g, conv_ssm_w, conv_ssm_b, dt_bias, a_log, d_skip, ssm_norm_g,
           w_proj_attn, w_proj_ssm, w_out, norm_ffn_g, w_up, conv_ffn_w, conv_ffn_b, w_down):
    depth = w_in.shape[0]
    assert depth == 1, "lambda_init and the stacked-parameter layout are specialised to depth 1"
    return _layer(x, rel_bias, norm_mix_g[0], w_in[0], q_norm_g[0], k_norm_g[0], lambda_q1[0],
                  lambda_k1[0], lambda_q2[0], lambda_k2[0], attn_subln_g[0], conv_ssm_w[0],
                  conv_ssm_b[0], dt_bias[0], a_log[0], d_skip[0], ssm_norm_g[0], w_proj_attn[0],
                  w_proj_ssm[0], w_out[0], norm_ffn_g[0], w_up[0], conv_ffn_w[0], conv_ffn_b[0],
                  w_down[0])
```
